```python
import jax
import jax.numpy as jnp
from jax import lax
import numpy as np

D_MODEL = 1024
BATCH = 8
SEQ = 4096
DEPTH = 2

CTX_LEN = 256
GRID_W = 64
BRANCH_WIDTH = D_MODEL // 2
N_BRANCHES = 3
RET_HEADS = 4
RET_V_DIM = BRANCH_WIDTH // RET_HEADS
RET_QK_DIM = RET_V_DIM // 2
RET_CHUNK = 128
SGU_WIDTH = BRANCH_WIDTH
SGU_GROUPS = 4
SGU_CHUNK = 128
ATT_HEAD_DIM = 64
ATT_HEADS = BRANCH_WIDTH // ATT_HEAD_DIM
ATT_KV_HEADS = 2
ATT_GROUP = ATT_HEADS // ATT_KV_HEADS
WINDOW = 128
ATT_BLOCK = 128
ROPE_BASE = 10000.0
N_EXPERTS = 16
EXPERT_FF = D_MODEL
CAPACITY_FACTOR = 2
EPS = 1e-6
NEG_INF = -1e30
IN_LAYOUT = (
    ('ret_q', RET_HEADS * RET_QK_DIM),
    ('ret_k', RET_HEADS * RET_QK_DIM),
    ('ret_v', RET_HEADS * RET_V_DIM),
    ('ret_g', RET_HEADS * RET_V_DIM),
    ('sgu_u', SGU_WIDTH),
    ('sgu_v', SGU_WIDTH),
    ('att_q', ATT_HEADS * ATT_HEAD_DIM),
    ('att_k', ATT_KV_HEADS * ATT_HEAD_DIM),
    ('att_v', ATT_KV_HEADS * ATT_HEAD_DIM),
    ('gates', N_BRANCHES * D_MODEL),
)
IN_WIDTH = 2 * RET_HEADS * RET_QK_DIM + 2 * RET_HEADS * RET_V_DIM + 2 * SGU_WIDTH + (ATT_HEADS + 2 * ATT_KV_HEADS) * ATT_HEAD_DIM + N_BRANCHES * D_MODEL
CTX_KV_NAMES = ('ret_k', 'ret_v', 'att_k', 'att_v')

kernel_name = 'hybrid_retention_sgu_swa_ecmoe_dit'


def _rms_norm(x, gain):
    xf = x.astype(jnp.float32)
    y = xf * lax.rsqrt(jnp.mean(xf * xf, axis=-1, keepdims=True) + EPS)
    return y.astype(x.dtype) * gain


def _layer_norm(x):
    xf = x.astype(jnp.float32)
    xc = xf - jnp.mean(xf, axis=-1, keepdims=True)
    return (xc * lax.rsqrt(jnp.mean(xc * xc, axis=-1, keepdims=True) + EPS)).astype(x.dtype)


def _layout_ranges():
    ranges, start = {}, 0
    for name, width in IN_LAYOUT:
        ranges[name] = (start, start + width)
        start += width
    return ranges


def _project(h, w_in, names):
    ranges = _layout_ranges()
    w = jnp.concatenate([w_in[:, ranges[n][0]:ranges[n][1]] for n in names], axis=1)
    z = h @ w
    widths = [ranges[n][1] - ranges[n][0] for n in names]
    offs = np.cumsum(widths)[:-1].tolist()
    return dict(zip(names, jnp.split(z, offs, axis=-1)))


def _heads(t, n, d):
    return t.reshape(t.shape[:-1] + (n, d))


def _axial_rope(rows, head_dim, dtype):
    f32 = jnp.float32
    row = jnp.repeat(jnp.arange(rows), GRID_W).astype(f32)
    col = jnp.tile(jnp.arange(GRID_W), rows).astype(f32)
    n_freq = head_dim // 4
    inv = jnp.power(ROPE_BASE, -jnp.arange(n_freq, dtype=f32) / n_freq)
    ang = jnp.concatenate([row[:, None] * inv, col[:, None] * inv], axis=-1)
    return jnp.cos(ang).astype(dtype), jnp.sin(ang).astype(dtype)


def _apply_rope(x, cos, sin):
    half = x.shape[-1] // 2
    shape = (x.shape[1],) + (1,) * (x.ndim - 3) + (half,)
    cos = cos.reshape(shape)
    sin = sin.reshape(shape)
    x1, x2 = x[..., :half], x[..., half:]
    return jnp.concatenate([x1 * cos - x2 * sin, x1 * sin + x2 * cos], axis=-1)


def _retention_dir(q, k, v, log_gamma, state0, include_diag):
    f32 = jnp.float32
    Bn, T, H, dk = q.shape
    dv = v.shape[-1]
    C = RET_CHUNK
    N = T // C
    qc = q.astype(f32).reshape(Bn, N, C, H, dk)
    kc = k.astype(f32).reshape(Bn, N, C, H, dk)
    vc = v.astype(f32).reshape(Bn, N, C, H, dv)
    idx = jnp.arange(C, dtype=f32)
    diff = idx[:, None] - idx[None, :]
    mask = (diff >= 0) if include_diag else (diff > 0)
    dec = jnp.where(mask[None], jnp.exp(jnp.where(mask, diff, 0.0)[None] * log_gamma[:, None, None]), 0.0)
    scores = jnp.einsum('bnihd,bnjhd->bnhij', qc, kc) * dec[None, None]
    inner = jnp.einsum('bnhij,bnjhe->bnihe', scores, vc)
    w_k = jnp.exp((C - 1 - idx)[:, None] * log_gamma[None, :])
    upd = jnp.einsum('bnjhd,bnjhe->nbhde', kc * w_k[None, None, :, :, None], vc)
    chunk_decay = jnp.exp(C * log_gamma)[None, :, None, None]

    def step(s, u):
        return chunk_decay * s + u, s

    final, s_prev = lax.scan(step, state0, upd)
    q_dec = jnp.exp((idx + 1.0)[:, None] * log_gamma[None, :])
    cross = jnp.einsum('bnihd,nbhde->bnihe', qc * q_dec[None, None, :, :, None], s_prev)
    return (inner + cross).reshape(Bn, T, H, dv), final


def _context_state(k, v, log_gamma):
    T = k.shape[1]
    w = jnp.exp((T - 1 - jnp.arange(T, dtype=jnp.float32))[:, None] * log_gamma[None, :])
    return jnp.einsum('bthd,bthe->bhde', k.astype(jnp.float32) * w[None, :, :, None], v.astype(jnp.float32))


def _retention_out(o, g):
    Bn, T = o.shape[:2]
    y = _layer_norm(o).reshape(Bn, T, -1).astype(g.dtype)
    return y * jax.nn.silu(g)


def _sgu(zu, zv, w_s, b_s):
    Bn, T, _ = zu.shape
    u = jax.nn.gelu(zu)
    v = _layer_norm(jax.nn.gelu(zv))
    n = T // SGU_CHUNK
    vg = v.reshape(Bn, n, SGU_CHUNK, SGU_GROUPS, SGU_WIDTH // SGU_GROUPS)
    mixed = jnp.einsum('gij,bnjgc->bnigc', w_s, vg) + b_s.T[:, :, None]
    return u * mixed.reshape(Bn, T, SGU_WIDTH)


def _window_attention(q, k, v, kc, vc, sink):
    f32 = jnp.float32
    Bn, S = q.shape[:2]
    nb = S // ATT_BLOCK
    L = kc.shape[1]
    scale = ATT_HEAD_DIM ** -0.5
    qb = q.reshape(Bn, nb, ATT_BLOCK, ATT_KV_HEADS, ATT_GROUP, ATT_HEAD_DIM)

    def band(t):
        tp = jnp.pad(t, ((0, 0), (ATT_BLOCK, ATT_BLOCK), (0, 0), (0, 0)))
        tp = tp.reshape(Bn, nb + 2, ATT_BLOCK, ATT_KV_HEADS, ATT_HEAD_DIM)
        return jnp.concatenate([tp[:, :-2], tp[:, 1:-1], tp[:, 2:]], axis=2)

    kb, vb = band(k), band(v)
    s_loc = jnp.einsum('bnihgd,bnjhd->bhgnij', qb, kb).astype(f32) * scale
    blk = jnp.arange(nb)[:, None, None] * ATT_BLOCK
    q_pos = blk + jnp.arange(ATT_BLOCK)[None, :, None]
    k_pos = blk - ATT_BLOCK + jnp.arange(3 * ATT_BLOCK)[None, None, :]
    valid = (jnp.abs(q_pos - k_pos) <= WINDOW) & (k_pos >= 0) & (k_pos < S)
    s_loc = jnp.where(valid, s_loc, NEG_INF)
    s_ctx = jnp.einsum('bnihgd,bjhd->bhgnij', qb, kc).astype(f32) * scale
    s_sink = jnp.broadcast_to(sink.astype(f32).reshape(1, ATT_KV_HEADS, ATT_GROUP, 1, 1, 1), s_loc.shape[:-1] + (1,))
    probs = jax.nn.softmax(jnp.concatenate([s_loc, s_ctx, s_sink], axis=-1), axis=-1).astype(v.dtype)
    n_loc = 3 * ATT_BLOCK
    o = (jnp.einsum('bhgnij,bnjhd->bnihgd', probs[..., :n_loc], vb)
         + jnp.einsum('bhgnij,bjhd->bnihgd', probs[..., n_loc:n_loc + L], vc))
    return o.reshape(Bn, S, ATT_HEADS * ATT_HEAD_DIM)


def _context_attention(q, kc, vc, sink):
    f32 = jnp.float32
    Bn, L = q.shape[:2]
    s = jnp.einsum('bihgd,bjhd->bhgij', q, kc).astype(f32) * (ATT_HEAD_DIM ** -0.5)
    s_sink = jnp.broadcast_to(sink.astype(f32).reshape(1, ATT_KV_HEADS, ATT_GROUP, 1, 1), s.shape[:-1] + (1,))
    probs = jax.nn.softmax(jnp.concatenate([s, s_sink], axis=-1), axis=-1).astype(vc.dtype)
    o = jnp.einsum('bhgij,bjhd->bihgd', probs[..., :L], vc)
    return o.reshape(Bn, L, ATT_HEADS * ATT_HEAD_DIM)


def _merge(branches, gate_logits, w_branch, w_out):
    Bn, T, _ = gate_logits.shape
    gates = jax.nn.sigmoid(gate_logits.reshape(Bn, T, N_BRANCHES, D_MODEL))
    merged = gates[:, :, 0] * (branches[0] @ w_branch[0])
    for i in range(1, N_BRANCHES):
        merged = merged + gates[:, :, i] * (branches[i] @ w_branch[i])
    return merged @ w_out


def _expert_choice_ffn(h, w_router, w_gate, w_up, w_down):
    Bn, T, D = h.shape
    cap = CAPACITY_FACTOR * T // N_EXPERTS
    aff = jax.nn.softmax((h @ w_router).astype(jnp.float32), axis=-1)
    top_w, top_idx = lax.top_k(jnp.swapaxes(aff, 1, 2), cap)
    xs = jax.vmap(lambda hb, ib: hb[ib])(h, top_idx)
    a = jnp.einsum('becd,edf->becf', xs, w_gate)
    u = jnp.einsum('becd,edf->becf', xs, w_up)
    y = jnp.einsum('becf,efd->becd', jax.nn.silu(a) * u, w_down) * top_w[..., None].astype(h.dtype)
    return jax.vmap(lambda ib, yb: jnp.zeros((T, D), h.dtype).at[ib.reshape(-1)].add(yb.reshape(-1, D)))(top_idx, y)


def _layer(x, xc, mod_lat, mod_ctx, rope_ret, rope_att, p, last):
    f32 = jnp.float32
    Bn = x.shape[0]
    sh1, sc1, g1, sh2, sc2, g2 = jnp.split(mod_lat, 6, axis=-1)
    csh1, csc1, cg1, csh2, csc2, cg2 = jnp.split(mod_ctx, 6, axis=-1)
    h = _rms_norm(x, p['norm1_g']) * (1 + sc1) + sh1
    hc = _rms_norm(xc, p['norm1_g']) * (1 + csc1) + csh1
    all_names = tuple(name for name, _ in IN_LAYOUT)
    z = _project(h, p['w_in'], all_names)
    zc = _project(hc, p['w_in'], CTX_KV_NAMES if last else all_names)

    cos_r, sin_r = rope_ret
    lg_f = jax.nn.log_sigmoid(p['ret_decay_fwd'].astype(f32))
    lg_b = jax.nn.log_sigmoid(p['ret_decay_bwd'].astype(f32))
    k_scale = RET_QK_DIM ** -0.5
    rq = _apply_rope(_heads(z['ret_q'], RET_HEADS, RET_QK_DIM), cos_r, sin_r)
    rk = _apply_rope(_heads(z['ret_k'], RET_HEADS, RET_QK_DIM), cos_r, sin_r) * k_scale
    rv = _heads(z['ret_v'], RET_HEADS, RET_V_DIM)
    rkc = _heads(zc['ret_k'], RET_HEADS, RET_QK_DIM) * k_scale
    rvc = _heads(zc['ret_v'], RET_HEADS, RET_V_DIM)
    if last:
        s_f = _context_state(rkc, rvc, lg_f)
        s_b = _context_state(rkc[:, ::-1], rvc[:, ::-1], lg_b)
    else:
        rqc = _heads(zc['ret_q'], RET_HEADS, RET_QK_DIM)
        zero = jnp.zeros((Bn, RET_HEADS, RET_QK_DIM, RET_V_DIM), f32)
        oc_f, s_f = _retention_dir(rqc, rkc, rvc, lg_f, zero, True)
        oc_b, s_b = _retention_dir(rqc[:, ::-1], rkc[:, ::-1], rvc[:, ::-1], lg_b, zero, False)
        ret_c = _retention_out(oc_f + oc_b[:, ::-1], zc['ret_g'])
    o_f, _ = _retention_dir(rq, rk, rv, lg_f, s_f, True)
    o_b, _ = _retention_dir(rq[:, ::-1], rk[:, ::-1], rv[:, ::-1], lg_b, s_b, False)
    ret = _retention_out(o_f + o_b[:, ::-1], z['ret_g'])

    sgu = _sgu(z['sgu_u'], z['sgu_v'], p['sgu_w'], p['sgu_b'])

    cos_a, sin_a = rope_att
    aq = _rms_norm(_heads(z['att_q'], ATT_KV_HEADS, ATT_GROUP * ATT_HEAD_DIM).reshape(z['att_q'].shape[:2] + (ATT_KV_HEADS, ATT_GROUP, ATT_HEAD_DIM)), p['q_norm_g'])
    aq = _apply_rope(aq, cos_a, sin_a)
    ak = _apply_rope(_rms_norm(_heads(z['att_k'], ATT_KV_HEADS, ATT_HEAD_DIM), p['k_norm_g']), cos_a, sin_a)
    av = _heads(z['att_v'], ATT_KV_HEADS, ATT_HEAD_DIM)
    akc = _rms_norm(_heads(zc['att_k'], ATT_KV_HEADS, ATT_HEAD_DIM), p['k_norm_g'])
    avc = _heads(zc['att_v'], ATT_KV_HEADS, ATT_HEAD_DIM)
    att = _window_attention(aq, ak, av, akc, avc, p['attn_sink'])

    x = x + g1 * _merge((ret, sgu, att), z['gates'], p['w_branch'], p['w_out'])
    h2 = _rms_norm(x, p['norm2_g']) * (1 + sc2) + sh2
    x = x + g2 * _expert_choice_ffn(h2, p['w_router'], p['w_exp_gate'], p['w_exp_up'], p['w_exp_down'])
    if last:
        return x, None

    sgu_c = _sgu(zc['sgu_u'], zc['sgu_v'], p['sgu_w'], p['sgu_b'])
    aqc = _rms_norm(zc['att_q'].reshape(zc['att_q'].shape[:2] + (ATT_KV_HEADS, ATT_GROUP, ATT_HEAD_DIM)), p['q_norm_g'])
    att_c = _context_attention(aqc, akc, avc, p['attn_sink'])
    xc = xc + cg1 * _merge((ret_c, sgu_c, att_c), zc['gates'], p['w_branch'], p['w_out'])
    h2c = _rms_norm(xc, p['norm2_g']) * (1 + csc2) + csh2
    xc = xc + cg2 * _expert_choice_ffn(h2c, p['w_router'], p['w_exp_gate'], p['w_exp_up'], p['w_exp_down'])
    return x, xc


def setup_inputs(seed: int = 0) -> dict:
    key = jax.random.key(seed)
    ks = jax.random.split(key, 24)
    f32 = jnp.float32
    D = D_MODEL

    def nrm(k, shape, s):
        return jax.random.normal(k, shape, f32) * s

    hidx = jnp.arange(RET_HEADS, dtype=f32)
    base_logit = jnp.log(jnp.power(2.0, 5.0 + hidx) - 1.0)
    return {
        'x': nrm(ks[0], (BATCH, SEQ, D), 1.0),
        'c': nrm(ks[1], (BATCH, D), 1.0),
        'ctx': nrm(ks[2], (BATCH, CTX_LEN, D), 1.0),
        'c_ctx': nrm(ks[3], (D,), 1.0),
        'w_mod': nrm(ks[4], (DEPTH, D, 6 * D), 0.5 * D ** -0.5),
        'b_mod': nrm(ks[5], (DEPTH, 6 * D), 0.02),
        'norm1_g': 1.0 + nrm(ks[6], (DEPTH, D), 0.02),
        'norm2_g': 1.0 + nrm(ks[7], (DEPTH, D), 0.02),
        'w_in': nrm(ks[8], (DEPTH, D, IN_WIDTH), D ** -0.5),
        'ret_decay_fwd': base_logit + nrm(ks[9], (DEPTH, RET_HEADS), 0.05),
        'ret_decay_bwd': base_logit + nrm(ks[10], (DEPTH, RET_HEADS), 0.05),
        'sgu_w': nrm(ks[11], (DEPTH, SGU_GROUPS, SGU_CHUNK, SGU_CHUNK), SGU_CHUNK ** -0.5),
        'sgu_b': 1.0 + nrm(ks[12], (DEPTH, SGU_GROUPS, SGU_CHUNK), 0.02),
        'q_norm_g': 1.0 + nrm(ks[13], (DEPTH, ATT_HEAD_DIM), 0.02),
        'k_norm_g': 1.0 + nrm(ks[14], (DEPTH, ATT_HEAD_DIM), 0.02),
        'attn_sink': nrm(ks[15], (DEPTH, ATT_HEADS), 0.5),
        'w_branch': nrm(ks[16], (DEPTH, N_BRANCHES, BRANCH_WIDTH, D), BRANCH_WIDTH ** -0.5),
        'w_out': nrm(ks[17], (DEPTH, D, D), D ** -0.5),
        'w_router': nrm(ks[18], (DEPTH, D, N_EXPERTS), D ** -0.5),
        'w_exp_gate': nrm(ks[19], (DEPTH, N_EXPERTS, D, EXPERT_FF), D ** -0.5),
        'w_exp_up': nrm(ks[20], (DEPTH, N_EXPERTS, D, EXPERT_FF), D ** -0.5),
        'w_exp_down': nrm(ks[21], (DEPTH, N_EXPERTS, EXPERT_FF, D), EXPERT_FF ** -0.5),
    }


def reference(x, c, ctx, c_ctx, w_mod, b_mod, norm1_g, norm2_g, w_in, ret_decay_fwd, ret_decay_bwd, sgu_w, sgu_b, q_norm_g, k_norm_g, attn_sink, w_branch, w_out, w_router, w_exp_gate, w_exp_up, w_exp_down):
    rows = x.shape[1] // GRID_W
    rope_ret = _axial_rope(rows, RET_QK_DIM, x.dtype)
    rope_att = _axial_rope(rows, ATT_HEAD_DIM, x.dtype)
    c_act = jax.nn.silu(c)
    c_ctx_act = jax.nn.silu(c_ctx)
    xc = ctx
    for l in range(DEPTH):
        p = {
            'norm1_g': norm1_g[l], 'norm2_g': norm2_g[l], 'w_in': w_in[l],
            'ret_decay_fwd': ret_decay_fwd[l], 'ret_decay_bwd': ret_decay_bwd[l],
            'sgu_w': sgu_w[l], 'sgu_b': sgu_b[l],
            'q_norm_g': q_norm_g[l], 'k_norm_g': k_norm_g[l], 'attn_sink': attn_sink[l],
            'w_branch': w_branch[l], 'w_out': w_out[l], 'w_router': w_router[l],
            'w_exp_gate': w_exp_gate[l], 'w_exp_up': w_exp_up[l], 'w_exp_down': w_exp_down[l],
        }
        mod_lat = (c_act @ w_mod[l] + b_mod[l])[:, None, :]
        mod_ctx = (c_ctx_act @ w_mod[l] + b_mod[l])[None, None, :]
        x, xc = _layer(x, xc, mod_lat, mod_ctx, rope_ret, rope_att, p, l == DEPTH - 1)
    return x
```

```python
import functools

import jax
import jax.numpy as jnp
import numpy as np
from jax import lax
from jax.experimental import pallas as pl
from jax.experimental.pallas import tpu as pltpu

GRID_W = 64
N_BRANCHES = 3
RET_HEADS = 4
SGU_GROUPS = 4
CHUNK = 128
ATT_HEAD_DIM = 64
ATT_KV_HEADS = 2
ROPE_BASE = 10000.0
N_EXPERTS = 16
CAPACITY_FACTOR = 2
EPS = 1e-6
NEG_INF = -1e30

LANES = 128
VMEM_LIMIT_BYTES = 48 * 1024 * 1024

f32 = jnp.float32
bf16 = jnp.bfloat16


def _layout(d_model):
    bw = d_model // 2
    qk = bw // RET_HEADS // 2
    widths = (
        ('gates', N_BRANCHES * d_model),
        ('ret_q', RET_HEADS * qk), ('ret_k', RET_HEADS * qk),
        ('ret_v', bw), ('ret_g', bw), ('sgu_u', bw), ('sgu_v', bw),
        ('att_q', bw), ('att_k', ATT_KV_HEADS * ATT_HEAD_DIM), ('att_v', ATT_KV_HEADS * ATT_HEAD_DIM),
    )
    out, start = {}, 0
    for name, w in widths:
        out[name] = (start, w)
        start += w
    return out, start


def _ref_layout(d_model):
    bw = d_model // 2
    qk = bw // RET_HEADS // 2
    widths = (
        ('ret_q', RET_HEADS * qk), ('ret_k', RET_HEADS * qk), ('ret_v', bw), ('ret_g', bw),
        ('sgu_u', bw), ('sgu_v', bw), ('att_q', bw),
        ('att_k', ATT_KV_HEADS * ATT_HEAD_DIM), ('att_v', ATT_KV_HEADS * ATT_HEAD_DIM),
        ('gates', N_BRANCHES * d_model),
    )
    out, start = {}, 0
    for name, w in widths:
        out[name] = (start, w)
        start += w
    return out


def _pick_tile(n, pref):
    t = min(n, pref)
    while n % t:
        t //= 2
    return t


def _proj_in_body(x_ref, scale_ref, shift_ref, w_ref, o_ref, h_ref):
    @pl.when(pl.program_id(2) == 0)
    def _():
        x = x_ref[0]
        y = x * lax.rsqrt(jnp.mean(x * x, axis=-1, keepdims=True) + EPS)
        h_ref[...] = (y * scale_ref[0] + shift_ref[0]).astype(bf16)

    o_ref[0] = jnp.dot(h_ref[...], w_ref[...], preferred_element_type=f32).astype(o_ref.dtype)


def _proj_in(x, scale, shift, w):
    B, T, D = x.shape
    N = w.shape[1]
    tm = _pick_tile(T, 1024)
    tn = 1280 if N % 1280 == 0 else _pick_tile(N, 1024)
    return pl.pallas_call(
        _proj_in_body,
        grid=(B, T // tm, N // tn),
        in_specs=[
            pl.BlockSpec((1, tm, D), lambda b, i, j: (b, i, 0)),
            pl.BlockSpec((1, 1, D), lambda b, i, j: (b, 0, 0)),
            pl.BlockSpec((1, 1, D), lambda b, i, j: (b, 0, 0)),
            pl.BlockSpec((D, tn), lambda b, i, j: (0, j)),
        ],
        out_specs=pl.BlockSpec((1, tm, tn), lambda b, i, j: (b, i, j)),
        out_shape=jax.ShapeDtypeStruct((B, T, N), bf16),
        scratch_shapes=[pltpu.VMEM((tm, D), bf16)],
        compiler_params=pltpu.CompilerParams(
            dimension_semantics=("parallel", "parallel", "arbitrary"),
            vmem_limit_bytes=VMEM_LIMIT_BYTES),
        name="proj_in",
    )(x, scale, shift, w)


def _merge_body(ret_ref, sgu_ref, att_ref, g0_ref, g1_ref, g2_ref, x_ref, wb_ref, wo_ref, gate_ref,
                scale_ref, shift_ref, wr_ref, xo_ref, h2_ref, aff_ref):
    merged = None
    for br_ref, g_ref, i in ((ret_ref, g0_ref, 0), (sgu_ref, g1_ref, 1), (att_ref, g2_ref, 2)):
        p = jnp.dot(br_ref[0], wb_ref[i], preferred_element_type=f32)
        t = jax.nn.sigmoid(g_ref[0].astype(f32)) * p
        merged = t if merged is None else merged + t
    o = jnp.dot(merged.astype(bf16), wo_ref[...], preferred_element_type=f32)
    x = x_ref[0] + gate_ref[0] * o
    xo_ref[0] = x
    y = x * lax.rsqrt(jnp.mean(x * x, axis=-1, keepdims=True) + EPS)
    h2 = (y * scale_ref[0] + shift_ref[0]).astype(bf16)
    h2_ref[0] = h2
    logits = jnp.dot(h2, wr_ref[...], preferred_element_type=f32)
    lane = lax.broadcasted_iota(jnp.int32, logits.shape, 1)
    logits = jnp.where(lane < N_EXPERTS, logits, NEG_INF)
    m = jnp.max(logits, axis=-1, keepdims=True)
    e = jnp.exp(logits - m)
    aff_ref[0] = e / jnp.sum(e, axis=-1, keepdims=True)


def _merge(ret, sgu, att, z, x, wb, wo, gate, scale, shift, wr):
    B, T, D = x.shape
    bw = ret.shape[-1]
    tm = _pick_tile(T, 512)
    tok = lambda b, i: (b, i, 0)
    vec = lambda b, i: (b, 0, 0)
    return pl.pallas_call(
        _merge_body,
        grid=(B, T // tm),
        in_specs=[
            pl.BlockSpec((1, tm, bw), tok), pl.BlockSpec((1, tm, bw), tok), pl.BlockSpec((1, tm, bw), tok),
            pl.BlockSpec((1, tm, D), lambda b, i: (b, i, 0)),
            pl.BlockSpec((1, tm, D), lambda b, i: (b, i, 1)),
            pl.BlockSpec((1, tm, D), lambda b, i: (b, i, 2)),
            pl.BlockSpec((1, tm, D), tok),
            pl.BlockSpec((N_BRANCHES, bw, D), lambda b, i: (0, 0, 0)),
            pl.BlockSpec((D, D), lambda b, i: (0, 0)),
            pl.BlockSpec((1, 1, D), vec), pl.BlockSpec((1, 1, D), vec), pl.BlockSpec((1, 1, D), vec),
            pl.BlockSpec((D, LANES), lambda b, i: (0, 0)),
        ],
        out_specs=[
            pl.BlockSpec((1, tm, D), tok), pl.BlockSpec((1, tm, D), tok), pl.BlockSpec((1, tm, LANES), tok),
        ],
        out_shape=[
            jax.ShapeDtypeStruct((B, T, D), f32),
            jax.ShapeDtypeStruct((B, T, D), bf16),
            jax.ShapeDtypeStruct((B, T, LANES), f32),
        ],
        compiler_params=pltpu.CompilerParams(
            dimension_semantics=("parallel", "parallel"), vmem_limit_bytes=VMEM_LIMIT_BYTES),
        name="merge",
    )(ret, sgu, att, z, z, z, x, wb, wo, gate, scale, shift, wr)


def _expert_body(xs_ref, wg_ref, wu_ref, wd_ref, tw_ref, y_ref):
    xs = xs_ref[0, 0]
    a = jnp.dot(xs, wg_ref[0], preferred_element_type=f32)
    u = jnp.dot(xs, wu_ref[0], preferred_element_type=f32)
    hmid = (jax.nn.silu(a) * u).astype(bf16)
    y_ref[0, 0] = jnp.dot(hmid, wd_ref[0], preferred_element_type=f32) * tw_ref[0, 0]


def _expert_ffn(xs, wg, wu, wd, tw):
    B, E, C, D = xs.shape
    F = wg.shape[-1]
    blk = lambda e, b: (b, e, 0, 0)
    wsp = lambda e, b: (e, 0, 0)
    return pl.pallas_call(
        _expert_body,
        grid=(E, B),
        in_specs=[
            pl.BlockSpec((1, 1, C, D), blk),
            pl.BlockSpec((1, D, F), wsp), pl.BlockSpec((1, D, F), wsp), pl.BlockSpec((1, F, D), wsp),
            pl.BlockSpec((1, 1, C, 1), blk),
        ],
        out_specs=pl.BlockSpec((1, 1, C, D), blk),
        out_shape=jax.ShapeDtypeStruct((B, E, C, D), f32),
        compiler_params=pltpu.CompilerParams(
            dimension_semantics=("parallel", "parallel"), vmem_limit_bytes=VMEM_LIMIT_BYTES),
        name="expert_ffn",
    )(xs, wg, wu, wd, tw)


def _layer_norm(x):
    xc = x - jnp.mean(x, axis=-1, keepdims=True)
    return xc * lax.rsqrt(jnp.mean(xc * xc, axis=-1, keepdims=True) + EPS)


def _rms(x, gain):
    return x * lax.rsqrt(jnp.mean(x * x, axis=-1, keepdims=True) + EPS) * gain


def _heads(t, n, d):
    return t.reshape(t.shape[:-1] + (n, d))


def _axial_rope(rows, head_dim):
    row = jnp.repeat(jnp.arange(rows), GRID_W).astype(f32)
    col = jnp.tile(jnp.arange(GRID_W), rows).astype(f32)
    n_freq = head_dim // 4
    inv = jnp.power(ROPE_BASE, -jnp.arange(n_freq, dtype=f32) / n_freq)
    ang = jnp.concatenate([row[:, None] * inv, col[:, None] * inv], axis=-1)
    return jnp.cos(ang), jnp.sin(ang)


def _apply_rope(x, cos, sin):
    half = x.shape[-1] // 2
    shape = (x.shape[1],) + (1,) * (x.ndim - 3) + (half,)
    cos = cos.reshape(shape)
    sin = sin.reshape(shape)
    x1, x2 = x[..., :half], x[..., half:]
    return jnp.concatenate([x1 * cos - x2 * sin, x1 * sin + x2 * cos], axis=-1)


def _retention_dir(q, k, v, log_gamma, state0, include_diag):
    Bn, T, H, dk = q.shape
    dv = v.shape[-1]
    C = CHUNK
    N = T // C
    qc = q.reshape(Bn, N, C, H, dk)
    kc = k.reshape(Bn, N, C, H, dk)
    vc = v.reshape(Bn, N, C, H, dv)
    idx = jnp.arange(C, dtype=f32)
    diff = idx[:, None] - idx[None, :]
    mask = (diff >= 0) if include_diag else (diff > 0)
    dec = jnp.where(mask[None], jnp.exp(jnp.where(mask, diff, 0.0)[None] * log_gamma[:, None, None]), 0.0)
    scores = jnp.einsum('bnihd,bnjhd->bnhij', qc, kc) * dec[None, None]
    inner = jnp.einsum('bnhij,bnjhe->bnihe', scores, vc)
    w_k = jnp.exp((C - 1 - idx)[:, None] * log_gamma[None, :])
    upd = jnp.einsum('bnjhd,bnjhe->nbhde', kc * w_k[None, None, :, :, None], vc)
    chunk_decay = jnp.exp(C * log_gamma)[None, :, None, None]

    def step(s, u):
        return chunk_decay * s + u, s

    final, s_prev = lax.scan(step, state0, upd)
    q_dec = jnp.exp((idx + 1.0)[:, None] * log_gamma[None, :])
    cross = jnp.einsum('bnihd,nbhde->bnihe', qc * q_dec[None, None, :, :, None], s_prev)
    return (inner + cross).reshape(Bn, T, H, dv), final


def _context_state(k, v, log_gamma):
    T = k.shape[1]
    w = jnp.exp((T - 1 - jnp.arange(T, dtype=f32))[:, None] * log_gamma[None, :])
    return jnp.einsum('bthd,bthe->bhde', k * w[None, :, :, None], v)


def _retention_out(o, g):
    Bn, T = o.shape[:2]
    return _layer_norm(o).reshape(Bn, T, -1) * jax.nn.silu(g)


def _sgu(zu, zv, w_s, b_s):
    Bn, T, W = zu.shape
    u = jax.nn.gelu(zu)
    v = _layer_norm(jax.nn.gelu(zv))
    n = T // CHUNK
    vg = v.reshape(Bn, n, CHUNK, SGU_GROUPS, W // SGU_GROUPS)
    mixed = jnp.einsum('gij,bnjgc->bnigc', w_s, vg) + b_s.T[:, :, None]
    return u * mixed.reshape(Bn, T, W)


def _window_attention(q, k, v, kc, vc, sink):
    Bn, S = q.shape[:2]
    G = q.shape[3]
    nb = S // CHUNK
    L = kc.shape[1]
    scale = ATT_HEAD_DIM ** -0.5
    qb = q.reshape(Bn, nb, CHUNK, ATT_KV_HEADS, G, ATT_HEAD_DIM)

    def band(t):
        tp = jnp.pad(t, ((0, 0), (CHUNK, CHUNK), (0, 0), (0, 0)))
        tp = tp.reshape(Bn, nb + 2, CHUNK, ATT_KV_HEADS, ATT_HEAD_DIM)
        return jnp.concatenate([tp[:, :-2], tp[:, 1:-1], tp[:, 2:]], axis=2)

    kb, vb = band(k), band(v)
    s_loc = jnp.einsum('bnihgd,bnjhd->bhgnij', qb, kb) * scale
    blk = jnp.arange(nb)[:, None, None] * CHUNK
    q_pos = blk + jnp.arange(CHUNK)[None, :, None]
    k_pos = blk - CHUNK + jnp.arange(3 * CHUNK)[None, None, :]
    valid = (jnp.abs(q_pos - k_pos) <= CHUNK) & (k_pos >= 0) & (k_pos < S)
    s_loc = jnp.where(valid, s_loc, NEG_INF)
    s_ctx = jnp.einsum('bnihgd,bjhd->bhgnij', qb, kc) * scale
    s_sink = jnp.broadcast_to(sink.reshape(1, ATT_KV_HEADS, G, 1, 1, 1), s_loc.shape[:-1] + (1,))
    probs = jax.nn.softmax(jnp.concatenate([s_loc, s_ctx, s_sink], axis=-1), axis=-1)
    n_loc = 3 * CHUNK
    o = (jnp.einsum('bhgnij,bnjhd->bnihgd', probs[..., :n_loc], vb)
         + jnp.einsum('bhgnij,bjhd->bnihgd', probs[..., n_loc:n_loc + L], vc))
    return o.reshape(Bn, S, -1)


def _context_attention(q, kc, vc, sink):
    Bn, L = q.shape[:2]
    G = q.shape[3]
    s = jnp.einsum('bihgd,bjhd->bhgij', q, kc) * (ATT_HEAD_DIM ** -0.5)
    s_sink = jnp.broadcast_to(sink.reshape(1, ATT_KV_HEADS, G, 1, 1), s.shape[:-1] + (1,))
    probs = jax.nn.softmax(jnp.concatenate([s, s_sink], axis=-1), axis=-1)
    o = jnp.einsum('bhgij,bjhd->bihgd', probs[..., :L], vc)
    return o.reshape(Bn, L, -1)


def _expert_choice(h2, aff, x, gate2, p):
    B, T, D = x.shape
    cap = CAPACITY_FACTOR * T // N_EXPERTS
    aff_t = jnp.swapaxes(aff[..., :N_EXPERTS], 1, 2)
    top_w, top_idx = lax.top_k(aff_t, cap)
    xs = jax.vmap(lambda hb, ib: hb[ib])(h2, top_idx)
    y = _expert_ffn(xs, p['w_exp_gate'], p['w_exp_up'], p['w_exp_down'], top_w[..., None])
    moe = jax.vmap(lambda ib, yb: jnp.zeros((T, D), f32).at[ib.reshape(-1)].add(yb.reshape(-1, D)))(top_idx, y)
    return x + gate2 * moe


def _layer(x, xc, mod_lat, mod_ctx, rope_ret, rope_att, p, last):
    B, S, D = x.shape
    bw = D // 2
    ret_qk = bw // RET_HEADS // 2
    ret_v = bw // RET_HEADS
    att_heads = bw // ATT_HEAD_DIM
    att_group = att_heads // ATT_KV_HEADS
    lay, _ = _layout(D)
    sh1, sc1, g1, sh2, sc2, g2 = jnp.split(mod_lat, 6, axis=-1)
    csh1, csc1, cg1, csh2, csc2, cg2 = jnp.split(mod_ctx, 6, axis=-1)
    n1 = p['norm1_g'][None, None, :]
    n2 = p['norm2_g'][None, None, :]

    def bcast(t):
        return jnp.broadcast_to(t, (B, 1, D))

    z_all = _proj_in(x, n1 * (1 + sc1), sh1, p['w_in'])
    zc_all = _proj_in(xc, bcast(n1 * (1 + csc1)), bcast(csh1), p['w_in'])

    def part(t, name):
        s, w = lay[name]
        return t[..., s:s + w].astype(f32)

    z = {n: part(z_all, n) for n in lay if n != 'gates'}
    zc = {n: part(zc_all, n) for n in lay if n != 'gates'}

    cos_r, sin_r = rope_ret
    lg_f = jax.nn.log_sigmoid(p['ret_decay_fwd'].astype(f32))
    lg_b = jax.nn.log_sigmoid(p['ret_decay_bwd'].astype(f32))
    k_scale = ret_qk ** -0.5
    rq = _apply_rope(_heads(z['ret_q'], RET_HEADS, ret_qk), cos_r, sin_r)
    rk = _apply_rope(_heads(z['ret_k'], RET_HEADS, ret_qk), cos_r, sin_r) * k_scale
    rv = _heads(z['ret_v'], RET_HEADS, ret_v)
    rkc = _heads(zc['ret_k'], RET_HEADS, ret_qk) * k_scale
    rvc = _heads(zc['ret_v'], RET_HEADS, ret_v)
    if last:
        s_f = _context_state(rkc, rvc, lg_f)
        s_b = _context_state(rkc[:, ::-1], rvc[:, ::-1], lg_b)
    else:
        rqc = _heads(zc['ret_q'], RET_HEADS, ret_qk)
        zero = jnp.zeros((B, RET_HEADS, ret_qk, ret_v), f32)
        oc_f, s_f = _retention_dir(rqc, rkc, rvc, lg_f, zero, True)
        oc_b, s_b = _retention_dir(rqc[:, ::-1], rkc[:, ::-1], rvc[:, ::-1], lg_b, zero, False)
        ret_c = _retention_out(oc_f + oc_b[:, ::-1], zc['ret_g'])
    o_f, _ = _retention_dir(rq, rk, rv, lg_f, s_f, True)
    o_b, _ = _retention_dir(rq[:, ::-1], rk[:, ::-1], rv[:, ::-1], lg_b, s_b, False)
    ret = _retention_out(o_f + o_b[:, ::-1], z['ret_g'])

    sgu = _sgu(z['sgu_u'], z['sgu_v'], p['sgu_w'], p['sgu_b'])

    cos_a, sin_a = rope_att
    aq = _rms(z['att_q'].reshape(B, S, ATT_KV_HEADS, att_group, ATT_HEAD_DIM), p['q_norm_g'])
    aq = _apply_rope(aq, cos_a, sin_a)
    ak = _apply_rope(_rms(_heads(z['att_k'], ATT_KV_HEADS, ATT_HEAD_DIM), p['k_norm_g']), cos_a, sin_a)
    av = _heads(z['att_v'], ATT_KV_HEADS, ATT_HEAD_DIM)
    akc = _rms(_heads(zc['att_k'], ATT_KV_HEADS, ATT_HEAD_DIM), p['k_norm_g'])
    avc = _heads(zc['att_v'], ATT_KV_HEADS, ATT_HEAD_DIM)
    att = _window_attention(aq, ak, av, akc, avc, p['attn_sink'])

    x, h2, aff = _merge(ret.astype(bf16), sgu.astype(bf16), att.astype(bf16), z_all, x,
                        p['w_branch'], p['w_out'], g1, n2 * (1 + sc2), sh2, p['w_router'])
    x = _expert_choice(h2, aff, x, g2, p)
    if last:
        return x, None

    L = xc.shape[1]
    sgu_c = _sgu(zc['sgu_u'], zc['sgu_v'], p['sgu_w'], p['sgu_b'])
    aqc = _rms(zc['att_q'].reshape(B, L, ATT_KV_HEADS, att_group, ATT_HEAD_DIM), p['q_norm_g'])
    att_c = _context_attention(aqc, akc, avc, p['attn_sink'])
    xc, h2c, affc = _merge(ret_c.astype(bf16), sgu_c.astype(bf16), att_c.astype(bf16), zc_all, xc,
                           p['w_branch'], p['w_out'], bcast(cg1), bcast(n2 * (1 + csc2)), bcast(csh2),
                           p['w_router'])
    xc = _expert_choice(h2c, affc, xc, cg2, p)
    return x, xc


def kernel(x, c, ctx, c_ctx, w_mod, b_mod, norm1_g, norm2_g, w_in, ret_decay_fwd, ret_decay_bwd, sgu_w, sgu_b,
           q_norm_g, k_norm_g, attn_sink, w_branch, w_out, w_router, w_exp_gate, w_exp_up, w_exp_down):
    B, S, D = x.shape
    depth = w_in.shape[0]
    rows = S // GRID_W
    bw = D // 2
    rope_ret = _axial_rope(rows, bw // RET_HEADS // 2)
    rope_att = _axial_rope(rows, ATT_HEAD_DIM)
    c_act = jax.nn.silu(c)
    c_ctx_act = jax.nn.silu(c_ctx)
    ref_lay = _ref_layout(D)
    lay, _ = _layout(D)
    xc = ctx
    hp = lax.Precision.HIGHEST
    for l in range(depth):
        w_in_perm = jnp.concatenate(
            [w_in[l][:, ref_lay[n][0]:ref_lay[n][0] + ref_lay[n][1]] for n in lay], axis=1).astype(bf16)
        wr = jnp.zeros((D, LANES), f32).at[:, :N_EXPERTS].set(w_router[l]).astype(bf16)
        p = {
            'norm1_g': norm1_g[l], 'norm2_g': norm2_g[l], 'w_in': w_in_perm,
            'ret_decay_fwd': ret_decay_fwd[l], 'ret_decay_bwd': ret_decay_bwd[l],
            'sgu_w': sgu_w[l], 'sgu_b': sgu_b[l],
            'q_norm_g': q_norm_g[l], 'k_norm_g': k_norm_g[l], 'attn_sink': attn_sink[l],
            'w_branch': w_branch[l].astype(bf16), 'w_out': w_out[l].astype(bf16), 'w_router': wr,
            'w_exp_gate': w_exp_gate[l].astype(bf16), 'w_exp_up': w_exp_up[l].astype(bf16),
            'w_exp_down': w_exp_down[l].astype(bf16),
        }
        mod_lat = (jnp.dot(c_act, w_mod[l], precision=hp) + b_mod[l])[:, None, :]
        mod_ctx = (jnp.dot(c_ctx_act, w_mod[l], precision=hp) + b_mod[l])[None, None, :]
        x, xc = _layer(x, xc, mod_lat, mod_ctx, rope_ret, rope_att, p, l == depth - 1)
    return x
```

```python
import functools

import jax
import jax.numpy as jnp
from jax import lax
from jax.experimental import pallas as pl
from jax.experimental.pallas import tpu as pltpu

GRID_W = 64
N_BRANCHES = 3
RET_HEADS = 4
SGU_GROUPS = 4
CHUNK = 128
HEAD_DIM = 64
ATT_KV_HEADS = 2
ROPE_BASE = 10000.0
N_EXPERTS = 16
CAPACITY_FACTOR = 2
EPS = 1e-6
NEG_INF = -1e30

LANES = 128
VMEM_LIMIT_BYTES = 48 * 1024 * 1024

f32 = jnp.float32
bf16 = jnp.bfloat16


def _layout(d_model, permuted):
    bw = d_model // 2
    qk = RET_HEADS * HEAD_DIM
    kv = ATT_KV_HEADS * HEAD_DIM
    gates = ('gates', N_BRANCHES * d_model)
    rest = (('ret_q', qk), ('ret_k', qk), ('ret_v', bw), ('ret_g', bw), ('sgu_u', bw), ('sgu_v', bw),
            ('att_q', bw), ('att_k', kv), ('att_v', kv))
    widths = ((gates,) + rest) if permuted else (rest + (gates,))
    out, start = {}, 0
    for name, w in widths:
        out[name] = (start, w)
        start += w
    return out


def _pick_tile(n, pref):
    t = min(n, pref)
    while n % t:
        t //= 2
    return t


def _params(*sem):
    return pltpu.CompilerParams(dimension_semantics=sem, vmem_limit_bytes=VMEM_LIMIT_BYTES)


def _proj_in_body(x_ref, scale_ref, shift_ref, w_ref, o_ref, h_ref):
    @pl.when(pl.program_id(2) == 0)
    def _():
        x = x_ref[0]
        y = x * lax.rsqrt(jnp.mean(x * x, axis=-1, keepdims=True) + EPS)
        h_ref[...] = (y * scale_ref[0] + shift_ref[0]).astype(bf16)

    o_ref[0] = jnp.dot(h_ref[...], w_ref[...], preferred_element_type=f32).astype(o_ref.dtype)


def _proj_in(x, scale, shift, w):
    B, T, D = x.shape
    N = w.shape[1]
    tm = _pick_tile(T, 1024)
    tn = 1280 if N % 1280 == 0 else _pick_tile(N, 1024)
    return pl.pallas_call(
        _proj_in_body,
        grid=(B, T // tm, N // tn),
        in_specs=[
            pl.BlockSpec((1, tm, D), lambda b, i, j: (b, i, 0)),
            pl.BlockSpec((1, 1, D), lambda b, i, j: (b, 0, 0)),
            pl.BlockSpec((1, 1, D), lambda b, i, j: (b, 0, 0)),
            pl.BlockSpec((D, tn), lambda b, i, j: (0, j)),
        ],
        out_specs=pl.BlockSpec((1, tm, tn), lambda b, i, j: (b, i, j)),
        out_shape=jax.ShapeDtypeStruct((B, T, N), bf16),
        scratch_shapes=[pltpu.VMEM((tm, D), bf16)],
        compiler_params=_params("parallel", "parallel", "arbitrary"),
        name="proj_in",
    )(x, scale, shift, w)


def _lane(shape):
    return lax.broadcasted_iota(jnp.int32, shape, 1)


def _swap_halves(x):
    return jnp.where((_lane(x.shape) & (HEAD_DIM // 2)) == 0,
                     pltpu.roll(x, LANES - HEAD_DIM // 2, 1), pltpu.roll(x, HEAD_DIM // 2, 1))


def _rope(x, cos, sin_signed):
    return x * cos + _swap_halves(x) * sin_signed


def _head_sumsq(x, ones_blockdiag):
    x2 = x * x
    hi = x2.astype(bf16)
    lo = (x2 - hi.astype(f32)).astype(bf16)
    return (jnp.dot(hi, ones_blockdiag, preferred_element_type=f32)
            + jnp.dot(lo, ones_blockdiag, preferred_element_type=f32))


def _dot_nt(a, b):
    return lax.dot_general(a, b, (((1,), (1,)), ((), ())), preferred_element_type=f32)


def _dot_tn(a, b):
    return lax.dot_general(a, b, (((0,), (0,)), ((), ())), preferred_element_type=f32)


def _ret_state_body(kf_ref, vf_ref, kb_ref, vb_ref, cosf_ref, sinf_ref, cosb_ref, sinb_ref,
                    wkf_ref, wkb_ref, cdf_ref, cdb_ref, bm_ref, s0f_ref, s0b_ref,
                    sf_ref, sb_ref, ff_ref, fb_ref, accf, accb):
    n = pl.program_id(1)

    @pl.when(n == 0)
    def _():
        accf[...] = s0f_ref[0]
        accb[...] = s0b_ref[0]

    def one_dir(k_ref, v_ref, cos_ref, sin_ref, wk_ref, cd_ref, out_ref, acc):
        k = k_ref[0].astype(f32)
        v = v_ref[0]
        for p in range(RET_HEADS // 2):
            kp = _rope(k[:, LANES * p:LANES * (p + 1)], cos_ref[...], sin_ref[...])
            kw = (kp * wk_ref[p]).astype(bf16)
            upd = _dot_tn(kw, v[:, 2 * LANES * p:2 * LANES * (p + 1)])
            s = acc[p]
            out_ref[0, 0, p] = s.astype(out_ref.dtype)
            acc[p] = cd_ref[p] * s + bm_ref[...] * upd

    one_dir(kf_ref, vf_ref, cosf_ref, sinf_ref, wkf_ref, cdf_ref, sf_ref, accf)
    one_dir(kb_ref, vb_ref, cosb_ref, sinb_ref, wkb_ref, cdb_ref, sb_ref, accb)

    @pl.when(n == pl.num_programs(1) - 1)
    def _():
        ff_ref[0] = accf[...]
        fb_ref[0] = accb[...]


def _ret_states(z, lay, cos, sin, tabs, s0f, s0b):
    B, T, _ = z.shape
    nc = T // CHUNK
    kq = lay['ret_k'][0] // (RET_HEADS * HEAD_DIM)
    vq = lay['ret_v'][0] // lay['ret_v'][1]
    vw = lay['ret_v'][1]
    kw = RET_HEADS * HEAD_DIM
    npair = RET_HEADS // 2
    fwd = lambda b, n: (b, n, 0)
    st_shape = (npair, LANES, 2 * LANES)
    const3 = lambda b, n: (0, 0, 0)
    return pl.pallas_call(
        _ret_state_body,
        grid=(B, nc),
        in_specs=[
            pl.BlockSpec((1, CHUNK, kw), lambda b, n: (b, n, kq)),
            pl.BlockSpec((1, CHUNK, vw), lambda b, n: (b, n, vq)),
            pl.BlockSpec((1, CHUNK, kw), lambda b, n: (b, nc - 1 - n, kq)),
            pl.BlockSpec((1, CHUNK, vw), lambda b, n: (b, nc - 1 - n, vq)),
            pl.BlockSpec((CHUNK, LANES), lambda b, n: (n, 0)),
            pl.BlockSpec((CHUNK, LANES), lambda b, n: (n, 0)),
            pl.BlockSpec((CHUNK, LANES), lambda b, n: (nc - 1 - n, 0)),
            pl.BlockSpec((CHUNK, LANES), lambda b, n: (nc - 1 - n, 0)),
            pl.BlockSpec((npair, CHUNK, LANES), const3),
            pl.BlockSpec((npair, CHUNK, LANES), const3),
            pl.BlockSpec(st_shape, const3),
            pl.BlockSpec(st_shape, const3),
            pl.BlockSpec((LANES, 2 * LANES), lambda b, n: (0, 0)),
            pl.BlockSpec((1,) + st_shape, lambda b, n: (b, 0, 0, 0)),
            pl.BlockSpec((1,) + st_shape, lambda b, n: (b, 0, 0, 0)),
        ],
        out_specs=[
            pl.BlockSpec((1, 1) + st_shape, lambda b, n: (b, n, 0, 0, 0)),
            pl.BlockSpec((1, 1) + st_shape, lambda b, n: (b, nc - 1 - n, 0, 0, 0)),
            pl.BlockSpec((1,) + st_shape, lambda b, n: (b, 0, 0, 0)),
            pl.BlockSpec((1,) + st_shape, lambda b, n: (b, 0, 0, 0)),
        ],
        out_shape=[
            jax.ShapeDtypeStruct((B, nc) + st_shape, bf16),
            jax.ShapeDtypeStruct((B, nc) + st_shape, bf16),
            jax.ShapeDtypeStruct((B,) + st_shape, f32),
            jax.ShapeDtypeStruct((B,) + st_shape, f32),
        ],
        scratch_shapes=[pltpu.VMEM(st_shape, f32), pltpu.VMEM(st_shape, f32)],
        compiler_params=_params("parallel", "arbitrary"),
        name="ret_states",
    )(z, z, z, z, cos, sin, cos, sin, tabs['wkf'], tabs['wkb'], tabs['cdf'], tabs['cdb'], tabs['bm'], s0f, s0b)


def _ret_sgu_body(q_ref, k_ref, v_ref, g_ref, u_ref, w_ref, cos_ref, sin_ref, sf_ref, sb_ref,
                  dm_ref, qdf_ref, qdb_ref, sw_ref, sbias_ref, ret_ref, sgu_ref):
    q = q_ref[0].astype(f32)
    k = k_ref[0].astype(f32)
    v = v_ref[0]
    g = g_ref[0].astype(f32)
    cos = cos_ref[...]
    sin = sin_ref[...]
    first_half = _lane((CHUNK, LANES)) < HEAD_DIM
    for p in range(RET_HEADS // 2):
        qp = _rope(q[:, LANES * p:LANES * (p + 1)], cos, sin)
        kp = _rope(k[:, LANES * p:LANES * (p + 1)], cos, sin).astype(bf16)
        cross = (jnp.dot((qp * qdf_ref[p]).astype(bf16), sf_ref[0, 0, p], preferred_element_type=f32)
                 + jnp.dot((qp * qdb_ref[p]).astype(bf16), sb_ref[0, 0, p], preferred_element_type=f32))
        for a in range(2):
            h = 2 * p + a
            qm = jnp.where(first_half if a == 0 else ~first_half, qp, 0.0).astype(bf16)
            sc = (_dot_nt(qm, kp) * dm_ref[h]).astype(bf16)
            o = jnp.dot(sc, v[:, LANES * h:LANES * (h + 1)], preferred_element_type=f32)
            o = o + cross[:, LANES * a:LANES * (a + 1)]
            oc = o - jnp.mean(o, axis=-1, keepdims=True)
            y = oc * lax.rsqrt(jnp.mean(oc * oc, axis=-1, keepdims=True) + EPS)
            ret_ref[0, :, LANES * h:LANES * (h + 1)] = (
                y * jax.nn.silu(g[:, LANES * h:LANES * (h + 1)])).astype(ret_ref.dtype)

    u = jax.nn.gelu(u_ref[0].astype(f32))
    w = jax.nn.gelu(w_ref[0].astype(f32))
    wc = w - jnp.mean(w, axis=-1, keepdims=True)
    wn = (wc * lax.rsqrt(jnp.mean(wc * wc, axis=-1, keepdims=True) + EPS)).astype(bf16)
    gw = wn.shape[-1] // SGU_GROUPS
    for i in range(SGU_GROUPS):
        sl = slice(gw * i, gw * (i + 1))
        mixed = jnp.dot(sw_ref[i], wn[:, sl], preferred_element_type=f32) + sbias_ref[:, sl]
        sgu_ref[0, :, sl] = (u[:, sl] * mixed).astype(sgu_ref.dtype)


def _ret_sgu(z, lay, cos, sin, sf, sb, tabs, sgu_w, sgu_bias):
    B, T, _ = z.shape
    nc = T // CHUNK
    bw = lay['ret_v'][1]
    qkw = RET_HEADS * HEAD_DIM

    def zs(name):
        s, w = lay[name]
        return pl.BlockSpec((1, CHUNK, w), lambda b, n, j=s // w: (b, n, j))

    st_shape = (RET_HEADS // 2, LANES, 2 * LANES)
    const3 = lambda b, n: (0, 0, 0)
    out_spec = pl.BlockSpec((1, CHUNK, bw), lambda b, n: (b, n, 0))
    return pl.pallas_call(
        _ret_sgu_body,
        grid=(B, nc),
        in_specs=[
            zs('ret_q'), zs('ret_k'), zs('ret_v'), zs('ret_g'), zs('sgu_u'), zs('sgu_v'),
            pl.BlockSpec((CHUNK, LANES), lambda b, n: (n, 0)),
            pl.BlockSpec((CHUNK, LANES), lambda b, n: (n, 0)),
            pl.BlockSpec((1, 1) + st_shape, lambda b, n: (b, n, 0, 0, 0)),
            pl.BlockSpec((1, 1) + st_shape, lambda b, n: (b, n, 0, 0, 0)),
            pl.BlockSpec((RET_HEADS, CHUNK, CHUNK), const3),
            pl.BlockSpec((RET_HEADS // 2, CHUNK, LANES), const3),
            pl.BlockSpec((RET_HEADS // 2, CHUNK, LANES), const3),
            pl.BlockSpec((SGU_GROUPS, CHUNK, CHUNK), const3),
            pl.BlockSpec((CHUNK, bw), lambda b, n: (0, 0)),
        ],
        out_specs=[out_spec, out_spec],
        out_shape=[jax.ShapeDtypeStruct((B, T, bw), bf16), jax.ShapeDtypeStruct((B, T, bw), bf16)],
        compiler_params=_params("parallel", "parallel"),
        name="ret_sgu",
    )(z, z, z, z, z, z, cos, sin, sf, sb, tabs['dm'], tabs['qdf'], tabs['qdb'], sgu_w, sgu_bias)


def _att_body(local, sink_ref, q_ref, *refs):
    if local:
        (kp_ref, kc_ref, kn_ref, vp_ref, vc_ref, vn_ref, qa_ref, qb_ref,
         kap_ref, kbp_ref, kac_ref, kbc_ref, kan_ref, kbn_ref, ck_ref, cv_ref, gk_ref, ones_ref, o_ref) = refs
    else:
        qa_ref, ck_ref, cv_ref, gk_ref, ones_ref, o_ref = refs
    n = pl.program_id(1)
    nb = pl.num_programs(1)
    ones = ones_ref[...]
    inv_hd = 1.0 / HEAD_DIM

    def norm_rope(x_ref_val, ta, tb):
        x = x_ref_val.astype(f32)
        r = lax.rsqrt(_head_sumsq(x, ones) * inv_hd + EPS)
        y = x * ta if tb is None else x * ta + _swap_halves(x) * tb
        return y * r

    kctx = norm_rope(ck_ref[0], gk_ref[...], None)
    vctx = cv_ref[0].astype(f32)
    L = kctx.shape[0]

    def dup(x):
        lo = _lane(x.shape) < HEAD_DIM
        xr = pltpu.roll(x, HEAD_DIM, 1)
        return jnp.where(lo, x, xr).astype(bf16), jnp.where(lo, xr, x).astype(bf16)

    def place(x):
        lo = _lane(x.shape) < HEAD_DIM
        xr = pltpu.roll(x, HEAD_DIM, 1)
        return {(0, 0): jnp.where(lo, x, 0.0).astype(bf16), (1, 1): jnp.where(lo, 0.0, x).astype(bf16),
                (0, 1): jnp.where(lo, 0.0, xr).astype(bf16), (1, 0): jnp.where(lo, xr, 0.0).astype(bf16)}

    kctx_d = dup(kctx)
    vctx_p = place(vctx)
    if local:
        kloc = jnp.concatenate([norm_rope(kp_ref[0], kap_ref[...], kbp_ref[...]),
                                norm_rope(kc_ref[0], kac_ref[...], kbc_ref[...]),
                                norm_rope(kn_ref[0], kan_ref[...], kbn_ref[...])], axis=0)
        vloc = jnp.concatenate([vp_ref[0], vc_ref[0], vn_ref[0]], axis=0).astype(f32)
        kloc_d = dup(kloc)
        vloc_p = place(vloc)
        i = lax.broadcasted_iota(jnp.int32, (CHUNK, 3 * CHUNK), 0)
        j = lax.broadcasted_iota(jnp.int32, (CHUNK, 3 * CHUNK), 1)
        has_prev = (n > 0).astype(jnp.int32)
        has_next = (n < nb - 1).astype(jnp.int32)
        valid = (j >= i * has_prev + CHUNK * (1 - has_prev)) & (j <= 2 * CHUNK - 1 + has_next * (i + 1))

    scale = HEAD_DIM ** -0.5
    first_half = _lane((CHUNK, LANES)) < HEAD_DIM
    n_groups = q_ref.shape[-1] // LANES
    heads_per_kv = 2 * n_groups // ATT_KV_HEADS
    for c in range(n_groups):
        xq = q_ref[0, :, LANES * c:LANES * (c + 1)]
        qn = norm_rope(xq, qa_ref[...], qb_ref[...] if local else None) * scale
        acc = None
        for a in range(2):
            head = 2 * c + a
            h = head // heads_per_kv
            qm = jnp.where(first_half if a == 0 else ~first_half, qn, 0.0).astype(bf16)
            sk = sink_ref[head]
            s_ctx = _dot_nt(qm, kctx_d[h])
            m = jnp.maximum(jnp.max(s_ctx, axis=-1, keepdims=True), sk)
            if local:
                s_loc = jnp.where(valid, _dot_nt(qm, kloc_d[h]), NEG_INF)
                m = jnp.maximum(m, jnp.max(s_loc, axis=-1, keepdims=True))
            p_ctx = jnp.exp(s_ctx - m)
            d = jnp.sum(p_ctx, axis=-1, keepdims=True) + jnp.exp(sk - m)
            o = jnp.dot(p_ctx.astype(bf16), vctx_p[(h, a)], preferred_element_type=f32)
            if local:
                p_loc = jnp.exp(s_loc - m)
                d = d + jnp.sum(p_loc, axis=-1, keepdims=True)
                o = o + jnp.dot(p_loc.astype(bf16), vloc_p[(h, a)], preferred_element_type=f32)
            o = o * (1.0 / d)
            acc = o if acc is None else acc + o
        o_ref[0, :, LANES * c:LANES * (c + 1)] = acc.astype(o_ref.dtype)


def _attention(z, zc, lay, att_tabs, sink, local):
    B, T, _ = z.shape
    L = zc.shape[1]
    nb = T // CHUNK
    qs, qw = lay['att_q']
    ks, kw = lay['att_k']
    vs, _ = lay['att_v']
    kj, vj = ks // kw, vs // kw
    prev = lambda n: jnp.maximum(n - 1, 0)
    nxt = lambda n: jnp.minimum(n + 1, nb - 1)
    tab = lambda f: pl.BlockSpec((CHUNK, LANES), lambda b, n: (f(n), 0))
    same = lambda n: n
    in_specs = [pl.BlockSpec(memory_space=pltpu.SMEM),
                pl.BlockSpec((1, CHUNK, qw), lambda b, n: (b, n, qs // qw))]
    args = [sink, z]
    if local:
        in_specs += [pl.BlockSpec((1, CHUNK, kw), lambda b, n, f=f, j=j: (b, f(n), j))
                     for j in (kj, vj) for f in (prev, same, nxt)]
        args += [z] * 6
        in_specs += [tab(same), tab(same), tab(prev), tab(prev), tab(same), tab(same), tab(nxt), tab(nxt)]
        args += [att_tabs['qa'], att_tabs['qb'], att_tabs['ka'], att_tabs['kb'], att_tabs['ka'], att_tabs['kb'],
                 att_tabs['ka'], att_tabs['kb']]
    else:
        in_specs += [pl.BlockSpec((1, LANES), lambda b, n: (0, 0))]
        args += [att_tabs['gq']]
    in_specs += [pl.BlockSpec((1, L, kw), lambda b, n: (b, 0, kj)),
                 pl.BlockSpec((1, L, kw), lambda b, n: (b, 0, vj)),
                 pl.BlockSpec((1, LANES), lambda b, n: (0, 0)),
                 pl.BlockSpec((LANES, LANES), lambda b, n: (0, 0))]
    args += [zc, zc, att_tabs['gk'], att_tabs['ones']]
    return pl.pallas_call(
        functools.partial(_att_body, local),
        grid=(B, nb),
        in_specs=in_specs,
        out_specs=pl.BlockSpec((1, CHUNK, qw), lambda b, n: (b, n, 0)),
        out_shape=jax.ShapeDtypeStruct((B, T, qw), bf16),
        compiler_params=_params("parallel", "parallel"),
        name="attention" if local else "ctx_attention",
    )(*args)


def _merge_body(ret_ref, sgu_ref, att_ref, g0_ref, g1_ref, g2_ref, x_ref, wb_ref, wo_ref, gate_ref,
                scale_ref, shift_ref, wr_ref, xo_ref, h2_ref, aff_ref):
    merged = None
    for br_ref, g_ref, i in ((ret_ref, g0_ref, 0), (sgu_ref, g1_ref, 1), (att_ref, g2_ref, 2)):
        p = jnp.dot(br_ref[0], wb_ref[i], preferred_element_type=f32)
        t = jax.nn.sigmoid(g_ref[0].astype(f32)) * p
        merged = t if merged is None else merged + t
    o = jnp.dot(merged.astype(bf16), wo_ref[...], preferred_element_type=f32)
    x = x_ref[0] + gate_ref[0] * o
    xo_ref[0] = x
    y = x * lax.rsqrt(jnp.mean(x * x, axis=-1, keepdims=True) + EPS)
    h2 = (y * scale_ref[0] + shift_ref[0]).astype(bf16)
    h2_ref[0] = h2
    logits = jnp.dot(h2, wr_ref[...], preferred_element_type=f32)
    logits = jnp.where(_lane(logits.shape) < N_EXPERTS, logits, NEG_INF)
    m = jnp.max(logits, axis=-1, keepdims=True)
    e = jnp.exp(logits - m)
    aff_ref[0] = e / jnp.sum(e, axis=-1, keepdims=True)


def _merge(ret, sgu, att, z, x, wb, wo, gate, scale, shift, wr):
    B, T, D = x.shape
    bw = ret.shape[-1]
    tm = _pick_tile(T, 512)
    tok = lambda b, i: (b, i, 0)
    vec = lambda b, i: (b, 0, 0)
    return pl.pallas_call(
        _merge_body,
        grid=(B, T // tm),
        in_specs=[
            pl.BlockSpec((1, tm, bw), tok), pl.BlockSpec((1, tm, bw), tok), pl.BlockSpec((1, tm, bw), tok),
            pl.BlockSpec((1, tm, D), lambda b, i: (b, i, 0)),
            pl.BlockSpec((1, tm, D), lambda b, i: (b, i, 1)),
            pl.BlockSpec((1, tm, D), lambda b, i: (b, i, 2)),
            pl.BlockSpec((1, tm, D), tok),
            pl.BlockSpec((N_BRANCHES, bw, D), lambda b, i: (0, 0, 0)),
            pl.BlockSpec((D, D), lambda b, i: (0, 0)),
            pl.BlockSpec((1, 1, D), vec), pl.BlockSpec((1, 1, D), vec), pl.BlockSpec((1, 1, D), vec),
            pl.BlockSpec((D, LANES), lambda b, i: (0, 0)),
        ],
        out_specs=[
            pl.BlockSpec((1, tm, D), tok), pl.BlockSpec((1, tm, D), tok), pl.BlockSpec((1, tm, LANES), tok),
        ],
        out_shape=[
            jax.ShapeDtypeStruct((B, T, D), f32),
            jax.ShapeDtypeStruct((B, T, D), bf16),
            jax.ShapeDtypeStruct((B, T, LANES), f32),
        ],
        compiler_params=_params("parallel", "parallel"),
        name="merge",
    )(ret, sgu, att, z, z, z, x, wb, wo, gate, scale, shift, wr)


def _expert_body(xs_ref, wg_ref, wu_ref, wd_ref, tw_ref, y_ref):
    xs = xs_ref[0, 0]
    a = jnp.dot(xs, wg_ref[0], preferred_element_type=f32)
    u = jnp.dot(xs, wu_ref[0], preferred_element_type=f32)
    hmid = (jax.nn.silu(a) * u).astype(bf16)
    y_ref[0, 0] = jnp.dot(hmid, wd_ref[0], preferred_element_type=f32) * tw_ref[0, 0]


def _expert_ffn(xs, wg, wu, wd, tw):
    B, E, C, D = xs.shape
    F = wg.shape[-1]
    blk = lambda e, b: (b, e, 0, 0)
    wsp = lambda e, b: (e, 0, 0)
    return pl.pallas_call(
        _expert_body,
        grid=(E, B),
        in_specs=[
            pl.BlockSpec((1, 1, C, D), blk),
            pl.BlockSpec((1, D, F), wsp), pl.BlockSpec((1, D, F), wsp), pl.BlockSpec((1, F, D), wsp),
            pl.BlockSpec((1, 1, C, 1), blk),
        ],
        out_specs=pl.BlockSpec((1, 1, C, D), blk),
        out_shape=jax.ShapeDtypeStruct((B, E, C, D), f32),
        compiler_params=_params("parallel", "parallel"),
        name="expert_ffn",
    )(xs, wg, wu, wd, tw)


def _expert_choice(h2, aff, x, gate2, p):
    B, T, D = x.shape
    cap = CAPACITY_FACTOR * T // N_EXPERTS
    aff_t = jnp.swapaxes(aff[..., :N_EXPERTS], 1, 2)
    top_w, top_idx = lax.top_k(aff_t, cap)
    xs = jax.vmap(lambda hb, ib: hb[ib])(h2, top_idx)
    y = _expert_ffn(xs, p['w_exp_gate'], p['w_exp_up'], p['w_exp_down'], top_w[..., None])
    moe = jax.vmap(lambda ib, yb: jnp.zeros((T, D), f32).at[ib.reshape(-1)].add(yb.reshape(-1, D)))(top_idx, y)
    return x + gate2 * moe


def _rope_tables(T):
    rows = T // GRID_W
    row = jnp.repeat(jnp.arange(rows), GRID_W).astype(f32)
    col = jnp.tile(jnp.arange(GRID_W), rows).astype(f32)
    n_freq = HEAD_DIM // 4
    inv = jnp.power(ROPE_BASE, -jnp.arange(n_freq, dtype=f32) / n_freq)
    ang = jnp.concatenate([row[:, None] * inv, col[:, None] * inv], axis=-1)
    cos, sin = jnp.cos(ang), jnp.sin(ang)
    reps = LANES // HEAD_DIM
    return jnp.tile(jnp.concatenate([cos, cos], axis=-1), (1, reps)), jnp.tile(
        jnp.concatenate([-sin, sin], axis=-1), (1, reps))


def _identity_rope(T):
    return jnp.ones((T, LANES), f32), jnp.zeros((T, LANES), f32)


def _retention_tables(decay_fwd, decay_bwd):
    lgf = jax.nn.log_sigmoid(decay_fwd.astype(f32))
    lgb = jax.nn.log_sigmoid(decay_bwd.astype(f32))
    C = CHUNK
    k_scale = HEAD_DIM ** -0.5
    idx = jnp.arange(C, dtype=f32)
    npair = RET_HEADS // 2

    def lanes(t):
        return jnp.repeat(t.reshape(npair, 2, C).transpose(0, 2, 1), HEAD_DIM, axis=-1)

    diff = idx[:, None] - idx[None, :]
    dm = jnp.where(diff >= 0, jnp.exp(jnp.maximum(diff, 0.0)[None] * lgf[:, None, None]),
                   jnp.exp(jnp.maximum(-diff, 0.0)[None] * lgb[:, None, None])) * k_scale

    def chunk_decay(lg):
        rows = jnp.repeat(jnp.exp(C * lg).reshape(npair, 2), HEAD_DIM, axis=-1)
        return jnp.broadcast_to(rows[:, :, None], (npair, LANES, 2 * LANES))

    r = jnp.arange(LANES)[:, None] // HEAD_DIM
    c = jnp.arange(2 * LANES)[None, :] // LANES
    return {
        'wkf': lanes(jnp.exp((C - 1 - idx)[None, :] * lgf[:, None])) * k_scale,
        'wkb': lanes(jnp.exp(idx[None, :] * lgb[:, None])) * k_scale,
        'qdf': lanes(jnp.exp((idx + 1.0)[None, :] * lgf[:, None])),
        'qdb': lanes(jnp.exp((C - idx)[None, :] * lgb[:, None])),
        'dm': dm, 'cdf': chunk_decay(lgf), 'cdb': chunk_decay(lgb), 'bm': (r == c).astype(f32),
    }


def _attention_tables(cos, sin, q_gain, k_gain):
    reps = LANES // HEAD_DIM
    half = HEAD_DIM // 2
    swap = lambda g: jnp.concatenate([g[half:], g[:half]])
    gq, gk = jnp.tile(q_gain, reps)[None], jnp.tile(k_gain, reps)[None]
    gqs, gks = jnp.tile(swap(q_gain), reps)[None], jnp.tile(swap(k_gain), reps)[None]
    r = jnp.arange(LANES) // HEAD_DIM
    return {'qa': cos * gq, 'qb': sin * gqs, 'ka': cos * gk, 'kb': sin * gks, 'gq': gq, 'gk': gk,
            'ones': (r[:, None] == r[None, :]).astype(bf16)}


def _layer(x, xc, mod_lat, mod_ctx, rope, p, last):
    B, S, D = x.shape
    L = xc.shape[1]
    lay = _layout(D, True)
    sh1, sc1, g1, sh2, sc2, g2 = jnp.split(mod_lat, 6, axis=-1)
    csh1, csc1, cg1, csh2, csc2, cg2 = jnp.split(mod_ctx, 6, axis=-1)
    n1 = p['norm1_g'][None, None, :]
    n2 = p['norm2_g'][None, None, :]

    def bcast(t):
        return jnp.broadcast_to(t, (B, 1, D))

    z = _proj_in(x, n1 * (1 + sc1), sh1, p['w_in'])
    zc = _proj_in(xc, bcast(n1 * (1 + csc1)), bcast(csh1), p['w_in'])

    cos, sin = rope
    cos_c, sin_c = _identity_rope(L)
    rtabs = _retention_tables(p['ret_decay_fwd'], p['ret_decay_bwd'])
    atabs = _attention_tables(cos, sin, p['q_norm_g'], p['k_norm_g'])
    sgu_w = p['sgu_w'].astype(bf16)
    gw = D // 2 // SGU_GROUPS
    sgu_bias = jnp.repeat(p['sgu_b'].T, gw, axis=-1)

    zero = jnp.zeros((B, RET_HEADS // 2, LANES, 2 * LANES), f32)
    sfc, sbc, s_f, s_b = _ret_states(zc, lay, cos_c, sin_c, rtabs, zero, zero)
    sf, sb, _, _ = _ret_states(z, lay, cos, sin, rtabs, s_f, s_b)
    ret, sgu = _ret_sgu(z, lay, cos, sin, sf, sb, rtabs, sgu_w, sgu_bias)
    att = _attention(z, zc, lay, atabs, p['attn_sink'], True)

    x, h2, aff = _merge(ret, sgu, att, z, x, p['w_branch'], p['w_out'], g1, n2 * (1 + sc2), sh2, p['w_router'])
    x = _expert_choice(h2, aff, x, g2, p)
    if last:
        return x, None

    ret_c, sgu_c = _ret_sgu(zc, lay, cos_c, sin_c, sfc, sbc, rtabs, sgu_w, sgu_bias)
    att_c = _attention(zc, zc, lay, atabs, p['attn_sink'], False)
    xc, h2c, affc = _merge(ret_c, sgu_c, att_c, zc, xc, p['w_branch'], p['w_out'], bcast(cg1),
                           bcast(n2 * (1 + csc2)), bcast(csh2), p['w_router'])
    xc = _expert_choice(h2c, affc, xc, cg2, p)
    return x, xc


def kernel(x, c, ctx, c_ctx, w_mod, b_mod, norm1_g, norm2_g, w_in, ret_decay_fwd, ret_decay_bwd, sgu_w, sgu_b,
           q_norm_g, k_norm_g, attn_sink, w_branch, w_out, w_router, w_exp_gate, w_exp_up, w_exp_down):
    B, S, D = x.shape
    depth = w_in.shape[0]
    rope = _rope_tables(S)
    c_act = jax.nn.silu(c)
    c_ctx_act = jax.nn.silu(c_ctx)
    ref_lay = _layout(D, False)
    lay = _layout(D, True)
    xc = ctx
    hp = lax.Precision.HIGHEST
    for l in range(depth):
        w_in_perm = jnp.concatenate(
            [w_in[l][:, ref_lay[n][0]:ref_lay[n][0] + ref_lay[n][1]] for n in lay], axis=1).astype(bf16)
        wr = jnp.zeros((D, LANES), f32).at[:, :N_EXPERTS].set(w_router[l]).astype(bf16)
        p = {
            'norm1_g': norm1_g[l], 'norm2_g': norm2_g[l], 'w_in': w_in_perm,
            'ret_decay_fwd': ret_decay_fwd[l], 'ret_decay_bwd': ret_decay_bwd[l],
            'sgu_w': sgu_w[l], 'sgu_b': sgu_b[l],
            'q_norm_g': q_norm_g[l], 'k_norm_g': k_norm_g[l], 'attn_sink': attn_sink[l],
            'w_branch': w_branch[l].astype(bf16), 'w_out': w_out[l].astype(bf16), 'w_router': wr,
            'w_exp_gate': w_exp_gate[l].astype(bf16), 'w_exp_up': w_exp_up[l].astype(bf16),
            'w_exp_down': w_exp_down[l].astype(bf16),
        }
        mod_lat = (jnp.dot(c_act, w_mod[l], precision=hp) + b_mod[l])[:, None, :]
        mod_ctx = (jnp.dot(c_ctx_act, w_mod[l], precision=hp) + b_mod[l])[None, None, :]
        x, xc = _layer(x, xc, mod_lat, mod_ctx, rope, p, l == depth - 1)
    return x
```

```python
import functools

import jax
import jax.numpy as jnp
from jax import lax
from jax.experimental import pallas as pl
from jax.experimental.pallas import tpu as pltpu

GRID_W = 64
N_BRANCHES = 3
RET_HEADS = 4
SGU_GROUPS = 4
CHUNK = 128
HEAD_DIM = 64
ATT_KV_HEADS = 2
ROPE_BASE = 10000.0
N_EXPERTS = 16
CAPACITY_FACTOR = 2
EPS = 1e-6
NEG_INF = -1e30

LANES = 128
VMEM_LIMIT_BYTES = 48 * 1024 * 1024

f32 = jnp.float32
bf16 = jnp.bfloat16


def _layout(d_model, permuted):
    bw = d_model // 2
    qk = RET_HEADS * HEAD_DIM
    kv = ATT_KV_HEADS * HEAD_DIM
    gates = ('gates', N_BRANCHES * d_model)
    rest = (('ret_q', qk), ('ret_k', qk), ('ret_v', bw), ('ret_g', bw), ('sgu_u', bw), ('sgu_v', bw),
            ('att_q', bw), ('att_k', kv), ('att_v', kv))
    widths = ((gates,) + rest) if permuted else (rest + (gates,))
    out, start = {}, 0
    for name, w in widths:
        out[name] = (start, w)
        start += w
    return out


def _pick_tile(n, pref):
    t = min(n, pref)
    while n % t:
        t //= 2
    return t


def _params(*sem):
    return pltpu.CompilerParams(dimension_semantics=sem, vmem_limit_bytes=VMEM_LIMIT_BYTES)


def _proj_in_body(x_ref, scale_ref, shift_ref, w_ref, o_ref, h_ref):
    @pl.when(pl.program_id(2) == 0)
    def _():
        x = x_ref[0]
        y = x * lax.rsqrt(jnp.mean(x * x, axis=-1, keepdims=True) + EPS)
        h_ref[...] = (y * scale_ref[0] + shift_ref[0]).astype(bf16)

    o_ref[0] = jnp.dot(h_ref[...], w_ref[...], preferred_element_type=f32).astype(o_ref.dtype)


def _proj_in(x, scale, shift, w):
    B, T, D = x.shape
    N = w.shape[1]
    tm = _pick_tile(T, 1024)
    tn = 1280 if N % 1280 == 0 else _pick_tile(N, 1024)
    return pl.pallas_call(
        _proj_in_body,
        grid=(B, T // tm, N // tn),
        in_specs=[
            pl.BlockSpec((1, tm, D), lambda b, i, j: (b, i, 0)),
            pl.BlockSpec((1, 1, D), lambda b, i, j: (b, 0, 0)),
            pl.BlockSpec((1, 1, D), lambda b, i, j: (b, 0, 0)),
            pl.BlockSpec((D, tn), lambda b, i, j: (0, j)),
        ],
        out_specs=pl.BlockSpec((1, tm, tn), lambda b, i, j: (b, i, j)),
        out_shape=jax.ShapeDtypeStruct((B, T, N), bf16),
        scratch_shapes=[pltpu.VMEM((tm, D), bf16)],
        compiler_params=_params("parallel", "parallel", "arbitrary"),
        name="proj_in",
    )(x, scale, shift, w)


def _lane(shape):
    return lax.broadcasted_iota(jnp.int32, shape, 1)


def _swap_halves(x):
    return jnp.where((_lane(x.shape) & (HEAD_DIM // 2)) == 0,
                     pltpu.roll(x, LANES - HEAD_DIM // 2, 1), pltpu.roll(x, HEAD_DIM // 2, 1))


def _rope(x, cos, sin_signed):
    return x * cos + _swap_halves(x) * sin_signed


def _head_sumsq(x, ones_blockdiag):
    x2 = x * x
    hi = x2.astype(bf16)
    lo = (x2 - hi.astype(f32)).astype(bf16)
    return (jnp.dot(hi, ones_blockdiag, preferred_element_type=f32)
            + jnp.dot(lo, ones_blockdiag, preferred_element_type=f32))


def _dot_nt(a, b):
    return lax.dot_general(a, b, (((1,), (1,)), ((), ())), preferred_element_type=f32)


def _dot_tn(a, b):
    return lax.dot_general(a, b, (((0,), (0,)), ((), ())), preferred_element_type=f32)


def _ret_state_body(kf_ref, vf_ref, kb_ref, vb_ref, cosf_ref, sinf_ref, cosb_ref, sinb_ref,
                    wkf_ref, wkb_ref, cdf_ref, cdb_ref, bm_ref, s0f_ref, s0b_ref,
                    sf_ref, sb_ref, ff_ref, fb_ref, accf, accb):
    n = pl.program_id(1)

    @pl.when(n == 0)
    def _():
        accf[...] = s0f_ref[0]
        accb[...] = s0b_ref[0]

    def one_dir(k_ref, v_ref, cos_ref, sin_ref, wk_ref, cd_ref, out_ref, acc):
        k = k_ref[0].astype(f32)
        v = v_ref[0]
        for p in range(RET_HEADS // 2):
            kp = _rope(k[:, LANES * p:LANES * (p + 1)], cos_ref[...], sin_ref[...])
            kw = (kp * wk_ref[p]).astype(bf16)
            upd = _dot_tn(kw, v[:, 2 * LANES * p:2 * LANES * (p + 1)])
            s = acc[p]
            out_ref[0, 0, p] = s.astype(out_ref.dtype)
            acc[p] = cd_ref[p] * s + bm_ref[...] * upd

    one_dir(kf_ref, vf_ref, cosf_ref, sinf_ref, wkf_ref, cdf_ref, sf_ref, accf)
    one_dir(kb_ref, vb_ref, cosb_ref, sinb_ref, wkb_ref, cdb_ref, sb_ref, accb)

    @pl.when(n == pl.num_programs(1) - 1)
    def _():
        ff_ref[0] = accf[...]
        fb_ref[0] = accb[...]


def _ret_states(z, lay, cos, sin, tabs, s0f, s0b):
    B, T, _ = z.shape
    nc = T // CHUNK
    kq = lay['ret_k'][0] // (RET_HEADS * HEAD_DIM)
    vq = lay['ret_v'][0] // lay['ret_v'][1]
    vw = lay['ret_v'][1]
    kw = RET_HEADS * HEAD_DIM
    npair = RET_HEADS // 2
    fwd = lambda b, n: (b, n, 0)
    st_shape = (npair, LANES, 2 * LANES)
    const3 = lambda b, n: (0, 0, 0)
    return pl.pallas_call(
        _ret_state_body,
        grid=(B, nc),
        in_specs=[
            pl.BlockSpec((1, CHUNK, kw), lambda b, n: (b, n, kq)),
            pl.BlockSpec((1, CHUNK, vw), lambda b, n: (b, n, vq)),
            pl.BlockSpec((1, CHUNK, kw), lambda b, n: (b, nc - 1 - n, kq)),
            pl.BlockSpec((1, CHUNK, vw), lambda b, n: (b, nc - 1 - n, vq)),
            pl.BlockSpec((CHUNK, LANES), lambda b, n: (n, 0)),
            pl.BlockSpec((CHUNK, LANES), lambda b, n: (n, 0)),
            pl.BlockSpec((CHUNK, LANES), lambda b, n: (nc - 1 - n, 0)),
            pl.BlockSpec((CHUNK, LANES), lambda b, n: (nc - 1 - n, 0)),
            pl.BlockSpec((npair, CHUNK, LANES), const3),
            pl.BlockSpec((npair, CHUNK, LANES), const3),
            pl.BlockSpec(st_shape, const3),
            pl.BlockSpec(st_shape, const3),
            pl.BlockSpec((LANES, 2 * LANES), lambda b, n: (0, 0)),
            pl.BlockSpec((1,) + st_shape, lambda b, n: (b, 0, 0, 0)),
            pl.BlockSpec((1,) + st_shape, lambda b, n: (b, 0, 0, 0)),
        ],
        out_specs=[
            pl.BlockSpec((1, 1) + st_shape, lambda b, n: (b, n, 0, 0, 0)),
            pl.BlockSpec((1, 1) + st_shape, lambda b, n: (b, nc - 1 - n, 0, 0, 0)),
            pl.BlockSpec((1,) + st_shape, lambda b, n: (b, 0, 0, 0)),
            pl.BlockSpec((1,) + st_shape, lambda b, n: (b, 0, 0, 0)),
        ],
        out_shape=[
            jax.ShapeDtypeStruct((B, nc) + st_shape, bf16),
            jax.ShapeDtypeStruct((B, nc) + st_shape, bf16),
            jax.ShapeDtypeStruct((B,) + st_shape, f32),
            jax.ShapeDtypeStruct((B,) + st_shape, f32),
        ],
        scratch_shapes=[pltpu.VMEM(st_shape, f32), pltpu.VMEM(st_shape, f32)],
        compiler_params=_params("parallel", "arbitrary"),
        name="ret_states",
    )(z, z, z, z, cos, sin, cos, sin, tabs['wkf'], tabs['wkb'], tabs['cdf'], tabs['cdb'], tabs['bm'], s0f, s0b)


def _ret_sgu_body(q_ref, k_ref, v_ref, g_ref, u_ref, w_ref, cos_ref, sin_ref, sf_ref, sb_ref,
                  dm_ref, qdf_ref, qdb_ref, sw_ref, sbias_ref, ret_ref, sgu_ref):
    q = q_ref[0].astype(f32)
    k = k_ref[0].astype(f32)
    v = v_ref[0]
    g = g_ref[0].astype(f32)
    cos = cos_ref[...]
    sin = sin_ref[...]
    first_half = _lane((CHUNK, LANES)) < HEAD_DIM
    for p in range(RET_HEADS // 2):
        qp = _rope(q[:, LANES * p:LANES * (p + 1)], cos, sin)
        kp = _rope(k[:, LANES * p:LANES * (p + 1)], cos, sin).astype(bf16)
        cross = (jnp.dot((qp * qdf_ref[p]).astype(bf16), sf_ref[0, 0, p], preferred_element_type=f32)
                 + jnp.dot((qp * qdb_ref[p]).astype(bf16), sb_ref[0, 0, p], preferred_element_type=f32))
        for a in range(2):
            h = 2 * p + a
            qm = jnp.where(first_half if a == 0 else ~first_half, qp, 0.0).astype(bf16)
            sc = (_dot_nt(qm, kp) * dm_ref[h]).astype(bf16)
            o = jnp.dot(sc, v[:, LANES * h:LANES * (h + 1)], preferred_element_type=f32)
            o = o + cross[:, LANES * a:LANES * (a + 1)]
            oc = o - jnp.mean(o, axis=-1, keepdims=True)
            y = oc * lax.rsqrt(jnp.mean(oc * oc, axis=-1, keepdims=True) + EPS)
            ret_ref[0, :, LANES * h:LANES * (h + 1)] = (
                y * jax.nn.silu(g[:, LANES * h:LANES * (h + 1)])).astype(ret_ref.dtype)

    u = jax.nn.gelu(u_ref[0].astype(f32))
    w = jax.nn.gelu(w_ref[0].astype(f32))
    wc = w - jnp.mean(w, axis=-1, keepdims=True)
    wn = (wc * lax.rsqrt(jnp.mean(wc * wc, axis=-1, keepdims=True) + EPS)).astype(bf16)
    gw = wn.shape[-1] // SGU_GROUPS
    for i in range(SGU_GROUPS):
        sl = slice(gw * i, gw * (i + 1))
        mixed = jnp.dot(sw_ref[i], wn[:, sl], preferred_element_type=f32) + sbias_ref[:, sl]
        sgu_ref[0, :, sl] = (u[:, sl] * mixed).astype(sgu_ref.dtype)


def _ret_sgu(z, lay, cos, sin, sf, sb, tabs, sgu_w, sgu_bias):
    B, T, _ = z.shape
    nc = T // CHUNK
    bw = lay['ret_v'][1]
    qkw = RET_HEADS * HEAD_DIM

    def zs(name):
        s, w = lay[name]
        return pl.BlockSpec((1, CHUNK, w), lambda b, n, j=s // w: (b, n, j))

    st_shape = (RET_HEADS // 2, LANES, 2 * LANES)
    const3 = lambda b, n: (0, 0, 0)
    out_spec = pl.BlockSpec((1, CHUNK, bw), lambda b, n: (b, n, 0))
    return pl.pallas_call(
        _ret_sgu_body,
        grid=(B, nc),
        in_specs=[
            zs('ret_q'), zs('ret_k'), zs('ret_v'), zs('ret_g'), zs('sgu_u'), zs('sgu_v'),
            pl.BlockSpec((CHUNK, LANES), lambda b, n: (n, 0)),
            pl.BlockSpec((CHUNK, LANES), lambda b, n: (n, 0)),
            pl.BlockSpec((1, 1) + st_shape, lambda b, n: (b, n, 0, 0, 0)),
            pl.BlockSpec((1, 1) + st_shape, lambda b, n: (b, n, 0, 0, 0)),
            pl.BlockSpec((RET_HEADS, CHUNK, CHUNK), const3),
            pl.BlockSpec((RET_HEADS // 2, CHUNK, LANES), const3),
            pl.BlockSpec((RET_HEADS // 2, CHUNK, LANES), const3),
            pl.BlockSpec((SGU_GROUPS, CHUNK, CHUNK), const3),
            pl.BlockSpec((CHUNK, bw), lambda b, n: (0, 0)),
        ],
        out_specs=[out_spec, out_spec],
        out_shape=[jax.ShapeDtypeStruct((B, T, bw), bf16), jax.ShapeDtypeStruct((B, T, bw), bf16)],
        compiler_params=_params("parallel", "parallel"),
        name="ret_sgu",
    )(z, z, z, z, z, z, cos, sin, sf, sb, tabs['dm'], tabs['qdf'], tabs['qdb'], sgu_w, sgu_bias)


def _att_body(local, sink_ref, q_ref, *refs):
    if local:
        (kp_ref, kc_ref, kn_ref, vp_ref, vc_ref, vn_ref, qa_ref, qb_ref,
         kap_ref, kbp_ref, kac_ref, kbc_ref, kan_ref, kbn_ref, ck_ref, cv_ref, gk_ref, ones_ref, o_ref) = refs
    else:
        qa_ref, ck_ref, cv_ref, gk_ref, ones_ref, o_ref = refs
    n = pl.program_id(1)
    nb = pl.num_programs(1)
    ones = ones_ref[...]
    inv_hd = 1.0 / HEAD_DIM

    def norm_rope(x_ref_val, ta, tb):
        x = x_ref_val.astype(f32)
        r = lax.rsqrt(_head_sumsq(x, ones) * inv_hd + EPS)
        y = x * ta if tb is None else x * ta + _swap_halves(x) * tb
        return y * r

    def dup(x):
        lo = _lane(x.shape) < HEAD_DIM
        xr = pltpu.roll(x, HEAD_DIM, 1)
        return jnp.where(lo, x, xr).astype(bf16), jnp.where(lo, xr, x).astype(bf16)

    ks = [norm_rope(ck_ref[0], gk_ref[...], None)]
    vs = [cv_ref[0]]
    n_loc = 0
    if local:
        ks = [norm_rope(kp_ref[0], kap_ref[...], kbp_ref[...]), norm_rope(kc_ref[0], kac_ref[...], kbc_ref[...]),
              norm_rope(kn_ref[0], kan_ref[...], kbn_ref[...])] + ks
        vs = [vp_ref[0], vc_ref[0], vn_ref[0]] + vs
        n_loc = 3 * CHUNK
    k_dup = dup(jnp.concatenate(ks, axis=0))
    v_dup = dup(jnp.concatenate(vs, axis=0).astype(f32))

    n_groups = q_ref.shape[-1] // LANES
    groups_per_kv = n_groups // ATT_KV_HEADS
    rows = 2 * groups_per_kv * CHUNK
    if local:
        i = lax.broadcasted_iota(jnp.int32, (rows, n_loc), 0) & (CHUNK - 1)
        j = lax.broadcasted_iota(jnp.int32, (rows, n_loc), 1)
        has_prev = (n > 0).astype(jnp.int32)
        has_next = (n < nb - 1).astype(jnp.int32)
        valid = (j >= i * has_prev + CHUNK * (1 - has_prev)) & (j <= 2 * CHUNK - 1 + has_next * (i + 1))

    log2e = 1.4426950408889634
    first_half = _lane((CHUNK, LANES)) < HEAD_DIM
    for h in range(ATT_KV_HEADS):
        q_rows, sink_rows = [], []
        for c in range(groups_per_kv * h, groups_per_kv * (h + 1)):
            xq = q_ref[0, :, LANES * c:LANES * (c + 1)]
            qn = norm_rope(xq, qa_ref[...], qb_ref[...] if local else None) * (HEAD_DIM ** -0.5 * log2e)
            for a in range(2):
                q_rows.append(jnp.where(first_half if a == 0 else ~first_half, qn, 0.0).astype(bf16))
                sink_rows.append(jnp.full((CHUNK, 1), sink_ref[2 * c + a] * log2e, f32))
        s = _dot_nt(jnp.concatenate(q_rows, axis=0), k_dup[h])
        sk = jnp.concatenate(sink_rows, axis=0)
        s_ctx = s[:, n_loc:]
        m = jnp.maximum(jnp.max(s_ctx, axis=-1, keepdims=True), sk)
        if local:
            s_loc = jnp.where(valid, s[:, :n_loc], NEG_INF)
            m = jnp.maximum(m, jnp.max(s_loc, axis=-1, keepdims=True))
        p_ctx = jnp.exp2(s_ctx - m)
        d = jnp.sum(p_ctx, axis=-1, keepdims=True) + jnp.exp2(sk - m)
        o = jnp.dot(p_ctx.astype(bf16), v_dup[h][n_loc:], preferred_element_type=f32)
        if local:
            p_loc = jnp.exp2(s_loc - m)
            d = d + jnp.sum(p_loc, axis=-1, keepdims=True)
            o = o + jnp.dot(p_loc.astype(bf16), v_dup[h][:n_loc], preferred_element_type=f32)
        o = o * (1.0 / d)
        for ci in range(groups_per_kv):
            c = groups_per_kv * h + ci
            o0 = o[2 * CHUNK * ci:2 * CHUNK * ci + CHUNK]
            o1 = o[2 * CHUNK * ci + CHUNK:2 * CHUNK * (ci + 1)]
            o_ref[0, :, LANES * c:LANES * (c + 1)] = jnp.where(first_half, o0, o1).astype(o_ref.dtype)


def _attention(z, zc, lay, att_tabs, sink, local):
    B, T, _ = z.shape
    L = zc.shape[1]
    nb = T // CHUNK
    qs, qw = lay['att_q']
    ks, kw = lay['att_k']
    vs, _ = lay['att_v']
    kj, vj = ks // kw, vs // kw
    prev = lambda n: jnp.maximum(n - 1, 0)
    nxt = lambda n: jnp.minimum(n + 1, nb - 1)
    tab = lambda f: pl.BlockSpec((CHUNK, LANES), lambda b, n: (f(n), 0))
    same = lambda n: n
    in_specs = [pl.BlockSpec(memory_space=pltpu.SMEM),
                pl.BlockSpec((1, CHUNK, qw), lambda b, n: (b, n, qs // qw))]
    args = [sink, z]
    if local:
        in_specs += [pl.BlockSpec((1, CHUNK, kw), lambda b, n, f=f, j=j: (b, f(n), j))
                     for j in (kj, vj) for f in (prev, same, nxt)]
        args += [z] * 6
        in_specs += [tab(same), tab(same), tab(prev), tab(prev), tab(same), tab(same), tab(nxt), tab(nxt)]
        args += [att_tabs['qa'], att_tabs['qb'], att_tabs['ka'], att_tabs['kb'], att_tabs['ka'], att_tabs['kb'],
                 att_tabs['ka'], att_tabs['kb']]
    else:
        in_specs += [pl.BlockSpec((1, LANES), lambda b, n: (0, 0))]
        args += [att_tabs['gq']]
    in_specs += [pl.BlockSpec((1, L, kw), lambda b, n: (b, 0, kj)),
                 pl.BlockSpec((1, L, kw), lambda b, n: (b, 0, vj)),
                 pl.BlockSpec((1, LANES), lambda b, n: (0, 0)),
                 pl.BlockSpec((LANES, LANES), lambda b, n: (0, 0))]
    args += [zc, zc, att_tabs['gk'], att_tabs['ones']]
    return pl.pallas_call(
        functools.partial(_att_body, local),
        grid=(B, nb),
        in_specs=in_specs,
        out_specs=pl.BlockSpec((1, CHUNK, qw), lambda b, n: (b, n, 0)),
        out_shape=jax.ShapeDtypeStruct((B, T, qw), bf16),
        compiler_params=_params("parallel", "parallel"),
        name="attention" if local else "ctx_attention",
    )(*args)


def _merge_body(ret_ref, sgu_ref, att_ref, g0_ref, g1_ref, g2_ref, x_ref, wb_ref, wo_ref, gate_ref,
                scale_ref, shift_ref, wr_ref, xo_ref, h2_ref, aff_ref):
    merged = None
    for br_ref, g_ref, i in ((ret_ref, g0_ref, 0), (sgu_ref, g1_ref, 1), (att_ref, g2_ref, 2)):
        p = jnp.dot(br_ref[0], wb_ref[i], preferred_element_type=f32)
        t = jax.nn.sigmoid(g_ref[0].astype(f32)) * p
        merged = t if merged is None else merged + t
    o = jnp.dot(merged.astype(bf16), wo_ref[...], preferred_element_type=f32)
    x = x_ref[0] + gate_ref[0] * o
    xo_ref[0] = x
    y = x * lax.rsqrt(jnp.mean(x * x, axis=-1, keepdims=True) + EPS)
    h2 = (y * scale_ref[0] + shift_ref[0]).astype(bf16)
    h2_ref[0] = h2
    logits = jnp.dot(h2, wr_ref[...], preferred_element_type=f32)
    logits = jnp.where(_lane(logits.shape) < N_EXPERTS, logits, NEG_INF)
    m = jnp.max(logits, axis=-1, keepdims=True)
    e = jnp.exp(logits - m)
    aff_ref[0] = e / jnp.sum(e, axis=-1, keepdims=True)


def _merge(ret, sgu, att, z, x, wb, wo, gate, scale, shift, wr):
    B, T, D = x.shape
    bw = ret.shape[-1]
    tm = _pick_tile(T, 512)
    tok = lambda b, i: (b, i, 0)
    vec = lambda b, i: (b, 0, 0)
    return pl.pallas_call(
        _merge_body,
        grid=(B, T // tm),
        in_specs=[
            pl.BlockSpec((1, tm, bw), tok), pl.BlockSpec((1, tm, bw), tok), pl.BlockSpec((1, tm, bw), tok),
            pl.BlockSpec((1, tm, D), lambda b, i: (b, i, 0)),
            pl.BlockSpec((1, tm, D), lambda b, i: (b, i, 1)),
            pl.BlockSpec((1, tm, D), lambda b, i: (b, i, 2)),
            pl.BlockSpec((1, tm, D), tok),
            pl.BlockSpec((N_BRANCHES, bw, D), lambda b, i: (0, 0, 0)),
            pl.BlockSpec((D, D), lambda b, i: (0, 0)),
            pl.BlockSpec((1, 1, D), vec), pl.BlockSpec((1, 1, D), vec), pl.BlockSpec((1, 1, D), vec),
            pl.BlockSpec((D, LANES), lambda b, i: (0, 0)),
        ],
        out_specs=[
            pl.BlockSpec((1, tm, D), tok), pl.BlockSpec((1, tm, D), tok), pl.BlockSpec((1, tm, LANES), tok),
        ],
        out_shape=[
            jax.ShapeDtypeStruct((B, T, D), f32),
            jax.ShapeDtypeStruct((B, T, D), bf16),
            jax.ShapeDtypeStruct((B, T, LANES), f32),
        ],
        compiler_params=_params("parallel", "parallel"),
        name="merge",
    )(ret, sgu, att, z, z, z, x, wb, wo, gate, scale, shift, wr)


ROUTE_TILE = 256
AFF_BITS = 31


def _route_body(cap, aff_ref, tri_ref, pos_ref, post_ref, c0_ref):
    T = aff_ref.shape[1]
    tt = tri_ref.shape[0]

    def refine(i, thr):
        cand = thr | jnp.left_shift(jnp.int32(1), AFF_BITS - 1 - i)
        above = jnp.where(pltpu.bitcast(aff_ref[0], jnp.int32) >= cand, 1.0, 0.0)
        return jnp.where(jnp.sum(above, axis=0, keepdims=True) >= cap, cand, thr)

    thr = lax.fori_loop(0, AFF_BITS, refine, jnp.zeros((1, LANES), jnp.int32))
    bits = pltpu.bitcast(aff_ref[0], jnp.int32)
    need = cap - jnp.sum(jnp.where(bits > thr, 1.0, 0.0), axis=0, keepdims=True)
    tri = tri_ref[...]
    carry_eq = jnp.zeros((1, LANES), f32)
    carry_sel = jnp.zeros((1, LANES), f32)
    for j in range(T // tt):
        b = pltpu.bitcast(aff_ref[0, tt * j:tt * (j + 1), :], jnp.int32)
        eq = b == thr
        eqf = jnp.where(eq, 1.0, 0.0)
        eq_rank = carry_eq + jnp.dot(tri, eqf.astype(bf16), preferred_element_type=f32) - eqf
        sel = (b > thr) | (eq & (eq_rank < need))
        self_ = jnp.where(sel, 1.0, 0.0)
        pos = carry_sel + jnp.dot(tri, self_.astype(bf16), preferred_element_type=f32) - self_
        pos = jnp.where(sel, pos, -1.0).astype(jnp.int32)
        pos_ref[0, tt * j:tt * (j + 1), :] = pos
        post_ref[0, j] = pos.T[:N_EXPERTS]
        c0_ref[0, j:j + 1, :] = carry_sel.astype(jnp.int32)
        carry_eq = carry_eq + jnp.sum(eqf, axis=0, keepdims=True)
        carry_sel = carry_sel + jnp.sum(self_, axis=0, keepdims=True)


def _route(aff, cap):
    B, T, _ = aff.shape
    tt = min(ROUTE_TILE, T)
    nt = T // tt
    tri = jnp.tril(jnp.ones((tt, tt), bf16))
    return pl.pallas_call(
        functools.partial(_route_body, cap),
        grid=(B,),
        in_specs=[pl.BlockSpec((1, T, LANES), lambda b: (b, 0, 0)), pl.BlockSpec((tt, tt), lambda b: (0, 0))],
        out_specs=[
            pl.BlockSpec((1, T, LANES), lambda b: (b, 0, 0)),
            pl.BlockSpec((1, nt, N_EXPERTS, tt), lambda b: (b, 0, 0, 0)),
            pl.BlockSpec((1, nt, LANES), lambda b: (b, 0, 0)),
        ],
        out_shape=[
            jax.ShapeDtypeStruct((B, T, LANES), jnp.int32),
            jax.ShapeDtypeStruct((B, nt, N_EXPERTS, tt), jnp.int32),
            jax.ShapeDtypeStruct((B, nt, LANES), jnp.int32),
        ],
        compiler_params=_params("parallel"),
        name="route",
    )(aff, tri)


def _expert_body(nt, cap, c0_ref, post_ref, h_ref, wg_ref, wu_ref, wd_ref, y_ref, xs_ref):
    e = pl.program_id(0)
    b = pl.program_id(1)
    tt = post_ref.shape[-1]
    D = h_ref.shape[-1]
    sb = min(cap, LANES)

    def tiles_starting_at_or_before(slot):
        return sum((c0_ref[(b * nt + j) * N_EXPERTS + e] <= slot).astype(jnp.int32) for j in range(nt))

    for k in range(cap // sb):
        slot = sb * k + lax.broadcasted_iota(jnp.int32, (sb, 1), 0)

        def gather(j, acc):
            onehot = jnp.where(post_ref[0, j, pl.ds(e, 1), :] == slot, 1.0, 0.0).astype(bf16)
            rows = h_ref[0, pl.ds(pl.multiple_of(j * tt, tt), tt), :]
            return acc + jnp.dot(onehot, rows, preferred_element_type=f32)

        j_lo = tiles_starting_at_or_before(sb * k) - 1
        j_hi = tiles_starting_at_or_before(sb * (k + 1) - 1)
        xs_ref[sb * k:sb * (k + 1), :] = lax.fori_loop(j_lo, j_hi, gather, jnp.zeros((sb, D), f32)).astype(bf16)

    xs = xs_ref[...]
    a = jnp.dot(xs, wg_ref[0], preferred_element_type=f32)
    u = jnp.dot(xs, wu_ref[0], preferred_element_type=f32)
    hmid = (jax.nn.silu(a) * u).astype(bf16)
    y_ref[0, 0] = jnp.dot(hmid, wd_ref[0], preferred_element_type=f32).astype(y_ref.dtype)


def _expert_ffn(c0s, post, h2, wg, wu, wd, cap):
    B, T, D = h2.shape
    E, _, F = wg.shape
    nt, tt = post.shape[1], post.shape[3]
    wsp = lambda e, b, c0: (e, 0, 0)
    return pl.pallas_call(
        functools.partial(_expert_body, nt, cap),
        grid_spec=pltpu.PrefetchScalarGridSpec(
            num_scalar_prefetch=1,
            grid=(E, B),
            in_specs=[
                pl.BlockSpec((1, nt, E, tt), lambda e, b, c0: (b, 0, 0, 0)),
                pl.BlockSpec((1, T, D), lambda e, b, c0: (b, 0, 0)),
                pl.BlockSpec((1, D, F), wsp), pl.BlockSpec((1, D, F), wsp), pl.BlockSpec((1, F, D), wsp),
            ],
            out_specs=pl.BlockSpec((1, 1, cap, D), lambda e, b, c0: (b, e, 0, 0)),
            scratch_shapes=[pltpu.VMEM((cap, D), bf16)],
        ),
        out_shape=jax.ShapeDtypeStruct((B, E, cap, D), bf16),
        compiler_params=_params("arbitrary", "arbitrary"),
        name="expert_ffn",
    )(c0s, post, h2, wg, wu, wd)


Y_ROW_ALIGN = 16


def _combine_body(nt, cap, win, c0_ref, pos_ref, aff_ref, x_ref, gate_ref, y_ref, o_ref, buf, extra, sem, sem_x):
    b = pl.program_id(0)
    j = pl.program_id(1)
    step = b * nt + j
    tt = pos_ref.shape[1]
    stride = win if win == cap else win - Y_ROW_ALIGN

    def c0(bb, jj, e):
        return c0_ref[(bb * nt + jj) * N_EXPERTS + e]

    def window_start(lo):
        return pl.multiple_of(jnp.minimum(lo & (-Y_ROW_ALIGN), cap - win), Y_ROW_ALIGN)

    def copies(bb, jj, p, dst, s):
        return [pltpu.make_async_copy(
            y_ref.at[bb, e, pl.ds(window_start(c0(bb, jj, e) + p * stride), win), :],
            dst.at[pl.ds(e * win, win), :], s) for e in range(N_EXPERTS)]

    @pl.when(step == 0)
    def _():
        for cp in copies(b, j, 0, buf.at[0], sem.at[0]):
            cp.start()

    nxt = step + 1

    @pl.when(nxt < pl.num_programs(0) * nt)
    def _():
        for cp in copies(nxt // nt, nxt % nt, 0, buf.at[nxt % 2], sem.at[nxt % 2]):
            cp.start()

    lane = _lane((1, LANES))
    pos = pos_ref[0]
    aff = aff_ref[0]
    per_group = LANES // win

    def lane_vector(fn):
        v = jnp.zeros((1, LANES), jnp.int32)
        for e in range(N_EXPERTS):
            v = jnp.where(lane == e, fn(e), v)
        return v

    def weights(p):
        lo = lane_vector(lambda e: c0(b, j, e) + p * stride)
        start = lane_vector(lambda e: window_start(c0(b, j, e) + p * stride))
        rel = jnp.where((pos >= lo) & (pos < lo + stride), pos - start, -LANES)
        lanes_full = _lane((tt, LANES))
        groups = []
        for g in range(N_EXPERTS // per_group):
            q = jnp.zeros((tt, LANES), f32)
            for i in range(per_group):
                e = g * per_group + i
                q = jnp.where(rel[:, e:e + 1] + i * win == lanes_full, aff[:, e:e + 1], q)
            groups.append(q.astype(bf16))
        return jnp.concatenate(groups, axis=1)

    for cp in copies(b, j, 0, buf.at[step % 2], sem.at[step % 2]):
        cp.wait()
    moe = jnp.dot(weights(0), buf[step % 2], preferred_element_type=f32)

    def count(e):
        end = jnp.where(j + 1 < nt, c0(b, jnp.minimum(j + 1, nt - 1), e), cap)
        return end - c0(b, j, e)

    max_count = functools.reduce(jnp.maximum, [count(e) for e in range(N_EXPERTS)])

    def more(p, acc):
        cps = copies(b, j, p, extra, sem_x.at[0])
        for cp in cps:
            cp.start()
        for cp in cps:
            cp.wait()
        return acc + jnp.dot(weights(p), extra[...], preferred_element_type=f32)

    moe = lax.fori_loop(1, (max_count + stride - 1) // stride, more, moe)
    o_ref[0] = x_ref[0] + gate_ref[0] * moe


def _combine(c0s, pos, aff, x, gate, y):
    B, T, D = x.shape
    E, cap = y.shape[1], y.shape[2]
    tt = min(ROUTE_TILE, T)
    nt = T // tt
    win = min(cap, LANES // 2)
    assert (cap - win) % Y_ROW_ALIGN == 0 and LANES % win == 0
    tok = lambda b, j, c0: (b, j, 0)
    return pl.pallas_call(
        functools.partial(_combine_body, nt, cap, win),
        grid_spec=pltpu.PrefetchScalarGridSpec(
            num_scalar_prefetch=1,
            grid=(B, nt),
            in_specs=[
                pl.BlockSpec((1, tt, LANES), tok), pl.BlockSpec((1, tt, LANES), tok), pl.BlockSpec((1, tt, D), tok),
                pl.BlockSpec((1, 1, D), lambda b, j, c0: (b, 0, 0)),
                pl.BlockSpec(memory_space=pl.ANY),
            ],
            out_specs=pl.BlockSpec((1, tt, D), tok),
            scratch_shapes=[pltpu.VMEM((2, E * win, D), bf16), pltpu.VMEM((E * win, D), bf16),
                            pltpu.SemaphoreType.DMA((2,)), pltpu.SemaphoreType.DMA((1,))],
        ),
        out_shape=jax.ShapeDtypeStruct((B, T, D), f32),
        compiler_params=_params("arbitrary", "arbitrary"),
        name="combine",
    )(c0s, pos, aff, x, gate, y)


def _expert_choice(h2, aff, x, gate2, p):
    T = x.shape[1]
    cap = CAPACITY_FACTOR * T // N_EXPERTS
    pos, post, c0 = _route(aff, cap)
    c0s = c0[:, :, :N_EXPERTS].reshape(-1)
    y = _expert_ffn(c0s, post, h2, p['w_exp_gate'], p['w_exp_up'], p['w_exp_down'], cap)
    return _combine(c0s, pos, aff, x, gate2, y)


def _rope_tables(T):
    rows = T // GRID_W
    row = jnp.repeat(jnp.arange(rows), GRID_W).astype(f32)
    col = jnp.tile(jnp.arange(GRID_W), rows).astype(f32)
    n_freq = HEAD_DIM // 4
    inv = jnp.power(ROPE_BASE, -jnp.arange(n_freq, dtype=f32) / n_freq)
    ang = jnp.concatenate([row[:, None] * inv, col[:, None] * inv], axis=-1)
    cos, sin = jnp.cos(ang), jnp.sin(ang)
    reps = LANES // HEAD_DIM
    return jnp.tile(jnp.concatenate([cos, cos], axis=-1), (1, reps)), jnp.tile(
        jnp.concatenate([-sin, sin], axis=-1), (1, reps))


def _identity_rope(T):
    return jnp.ones((T, LANES), f32), jnp.zeros((T, LANES), f32)


def _retention_tables(decay_fwd, decay_bwd):
    lgf = jax.nn.log_sigmoid(decay_fwd.astype(f32))
    lgb = jax.nn.log_sigmoid(decay_bwd.astype(f32))
    C = CHUNK
    k_scale = HEAD_DIM ** -0.5
    idx = jnp.arange(C, dtype=f32)
    npair = RET_HEADS // 2

    def lanes(t):
        return jnp.repeat(t.reshape(npair, 2, C).transpose(0, 2, 1), HEAD_DIM, axis=-1)

    diff = idx[:, None] - idx[None, :]
    dm = jnp.where(diff >= 0, jnp.exp(jnp.maximum(diff, 0.0)[None] * lgf[:, None, None]),
                   jnp.exp(jnp.maximum(-diff, 0.0)[None] * lgb[:, None, None])) * k_scale

    def chunk_decay(lg):
        rows = jnp.repeat(jnp.exp(C * lg).reshape(npair, 2), HEAD_DIM, axis=-1)
        return jnp.broadcast_to(rows[:, :, None], (npair, LANES, 2 * LANES))

    r = jnp.arange(LANES)[:, None] // HEAD_DIM
    c = jnp.arange(2 * LANES)[None, :] // LANES
    return {
        'wkf': lanes(jnp.exp((C - 1 - idx)[None, :] * lgf[:, None])) * k_scale,
        'wkb': lanes(jnp.exp(idx[None, :] * lgb[:, None])) * k_scale,
        'qdf': lanes(jnp.exp((idx + 1.0)[None, :] * lgf[:, None])),
        'qdb': lanes(jnp.exp((C - idx)[None, :] * lgb[:, None])),
        'dm': dm, 'cdf': chunk_decay(lgf), 'cdb': chunk_decay(lgb), 'bm': (r == c).astype(f32),
    }


def _attention_tables(cos, sin, q_gain, k_gain):
    reps = LANES // HEAD_DIM
    half = HEAD_DIM // 2
    swap = lambda g: jnp.concatenate([g[half:], g[:half]])
    gq, gk = jnp.tile(q_gain, reps)[None], jnp.tile(k_gain, reps)[None]
    gqs, gks = jnp.tile(swap(q_gain), reps)[None], jnp.tile(swap(k_gain), reps)[None]
    r = jnp.arange(LANES) // HEAD_DIM
    return {'qa': cos * gq, 'qb': sin * gqs, 'ka': cos * gk, 'kb': sin * gks, 'gq': gq, 'gk': gk,
            'ones': (r[:, None] == r[None, :]).astype(bf16)}


def _layer(x, xc, mod_lat, mod_ctx, rope, p, last):
    B, S, D = x.shape
    L = xc.shape[1]
    lay = _layout(D, True)
    sh1, sc1, g1, sh2, sc2, g2 = jnp.split(mod_lat, 6, axis=-1)
    csh1, csc1, cg1, csh2, csc2, cg2 = jnp.split(mod_ctx, 6, axis=-1)
    n1 = p['norm1_g'][None, None, :]
    n2 = p['norm2_g'][None, None, :]

    def bcast(t):
        return jnp.broadcast_to(t, (B, 1, D))

    z = _proj_in(x, n1 * (1 + sc1), sh1, p['w_in'])
    zc = _proj_in(xc, bcast(n1 * (1 + csc1)), bcast(csh1), p['w_in'])

    cos, sin = rope
    cos_c, sin_c = _identity_rope(L)
    rtabs = _retention_tables(p['ret_decay_fwd'], p['ret_decay_bwd'])
    atabs = _attention_tables(cos, sin, p['q_norm_g'], p['k_norm_g'])
    sgu_w = p['sgu_w'].astype(bf16)
    gw = D // 2 // SGU_GROUPS
    sgu_bias = jnp.repeat(p['sgu_b'].T, gw, axis=-1)

    zero = jnp.zeros((B, RET_HEADS // 2, LANES, 2 * LANES), f32)
    sfc, sbc, s_f, s_b = _ret_states(zc, lay, cos_c, sin_c, rtabs, zero, zero)
    sf, sb, _, _ = _ret_states(z, lay, cos, sin, rtabs, s_f, s_b)
    ret, sgu = _ret_sgu(z, lay, cos, sin, sf, sb, rtabs, sgu_w, sgu_bias)
    att = _attention(z, zc, lay, atabs, p['attn_sink'], True)

    x, h2, aff = _merge(ret, sgu, att, z, x, p['w_branch'], p['w_out'], g1, n2 * (1 + sc2), sh2, p['w_router'])
    x = _expert_choice(h2, aff, x, g2, p)
    if last:
        return x, None

    ret_c, sgu_c = _ret_sgu(zc, lay, cos_c, sin_c, sfc, sbc, rtabs, sgu_w, sgu_bias)
    att_c = _attention(zc, zc, lay, atabs, p['attn_sink'], False)
    xc, h2c, affc = _merge(ret_c, sgu_c, att_c, zc, xc, p['w_branch'], p['w_out'], bcast(cg1),
                           bcast(n2 * (1 + csc2)), bcast(csh2), p['w_router'])
    xc = _expert_choice(h2c, affc, xc, bcast(cg2), p)
    return x, xc


def kernel(x, c, ctx, c_ctx, w_mod, b_mod, norm1_g, norm2_g, w_in, ret_decay_fwd, ret_decay_bwd, sgu_w, sgu_b,
           q_norm_g, k_norm_g, attn_sink, w_branch, w_out, w_router, w_exp_gate, w_exp_up, w_exp_down):
    B, S, D = x.shape
    depth = w_in.shape[0]
    rope = _rope_tables(S)
    c_act = jax.nn.silu(c)
    c_ctx_act = jax.nn.silu(c_ctx)
    ref_lay = _layout(D, False)
    lay = _layout(D, True)
    xc = ctx
    hp = lax.Precision.HIGHEST
    for l in range(depth):
        w_in_perm = jnp.concatenate(
            [w_in[l][:, ref_lay[n][0]:ref_lay[n][0] + ref_lay[n][1]] for n in lay], axis=1).astype(bf16)
        wr = jnp.zeros((D, LANES), f32).at[:, :N_EXPERTS].set(w_router[l]).astype(bf16)
        p = {
            'norm1_g': norm1_g[l], 'norm2_g': norm2_g[l], 'w_in': w_in_perm,
            'ret_decay_fwd': ret_decay_fwd[l], 'ret_decay_bwd': ret_decay_bwd[l],
            'sgu_w': sgu_w[l], 'sgu_b': sgu_b[l],
            'q_norm_g': q_norm_g[l], 'k_norm_g': k_norm_g[l], 'attn_sink': attn_sink[l],
            'w_branch': w_branch[l].astype(bf16), 'w_out': w_out[l].astype(bf16), 'w_router': wr,
            'w_exp_gate': w_exp_gate[l].astype(bf16), 'w_exp_up': w_exp_up[l].astype(bf16),
            'w_exp_down': w_exp_down[l].astype(bf16),
        }
        mod_lat = (jnp.dot(c_act, w_mod[l], precision=hp) + b_mod[l])[:, None, :]
        mod_ctx = (jnp.dot(c_ctx_act, w_mod[l], precision=hp) + b_mod[l])[None, None, :]
        x, xc = _layer(x, xc, mod_lat, mod_ctx, rope, p, l == depth - 1)
    return x
```

```python
import functools

import jax
import jax.numpy as jnp
from jax import lax
from jax.experimental import pallas as pl
from jax.experimental.pallas import tpu as pltpu

GRID_W = 64
N_BRANCHES = 3
RET_HEADS = 4
SGU_GROUPS = 4
CHUNK = 128
HEAD_DIM = 64
ATT_KV_HEADS = 2
ROPE_BASE = 10000.0
N_EXPERTS = 16
CAPACITY_FACTOR = 2
EPS = 1e-6
NEG_INF = -1e30

LANES = 128
VMEM_LIMIT_BYTES = 48 * 1024 * 1024
EXPERT_VMEM_LIMIT_BYTES = 56 * 1024 * 1024

f32 = jnp.float32
bf16 = jnp.bfloat16


def _layout(d_model):
    bw = d_model // 2
    qk = RET_HEADS * HEAD_DIM
    kv = ATT_KV_HEADS * HEAD_DIM
    widths = (('ret_q', qk), ('ret_k', qk), ('ret_v', bw), ('ret_g', bw), ('sgu_u', bw), ('sgu_v', bw),
              ('att_q', bw), ('att_k', kv), ('att_v', kv), ('gates', N_BRANCHES * d_model))
    out, start = {}, 0
    for name, w in widths:
        out[name] = (start, w)
        start += w
    return out


def _chunks_per_step(T):
    return 2 if (T // CHUNK) % 2 == 0 else 1


def _pick_tile(n, pref):
    t = min(n, pref)
    while n % t:
        t //= 2
    return t


def _params(*sem):
    return pltpu.CompilerParams(dimension_semantics=sem, vmem_limit_bytes=VMEM_LIMIT_BYTES)


def _proj_in_body(x_ref, scale_ref, shift_ref, w_ref, o_ref, h_ref):
    @pl.when(pl.program_id(2) == 0)
    def _():
        x = x_ref[0]
        y = x * lax.rsqrt(jnp.mean(x * x, axis=-1, keepdims=True) + EPS)
        h_ref[...] = (y * scale_ref[0] + shift_ref[0]).astype(bf16)

    o_ref[0] = jnp.dot(h_ref[...], w_ref[...], preferred_element_type=f32).astype(o_ref.dtype)


def _proj_in(x, scale, shift, w):
    B, T, D = x.shape
    N = w.shape[1]
    tm = _pick_tile(T, 1024)
    tn = 1280 if N % 1280 == 0 else _pick_tile(N, 1024)
    return pl.pallas_call(
        _proj_in_body,
        grid=(B, T // tm, N // tn),
        in_specs=[
            pl.BlockSpec((1, tm, D), lambda b, i, j: (b, i, 0)),
            pl.BlockSpec((1, 1, D), lambda b, i, j: (b, 0, 0)),
            pl.BlockSpec((1, 1, D), lambda b, i, j: (b, 0, 0)),
            pl.BlockSpec((D, tn), lambda b, i, j: (0, j)),
        ],
        out_specs=pl.BlockSpec((1, tm, tn), lambda b, i, j: (b, i, j)),
        out_shape=jax.ShapeDtypeStruct((B, T, N), bf16),
        scratch_shapes=[pltpu.VMEM((tm, D), bf16)],
        compiler_params=_params("parallel", "parallel", "arbitrary"),
        name="proj_in",
    )(x, scale, shift, w)


def _lane(shape):
    return lax.broadcasted_iota(jnp.int32, shape, 1)


def _swap_halves(x):
    return jnp.where((_lane(x.shape) & (HEAD_DIM // 2)) == 0,
                     pltpu.roll(x, LANES - HEAD_DIM // 2, 1), pltpu.roll(x, HEAD_DIM // 2, 1))


def _rope(x, cos, sin_signed):
    return x * cos + _swap_halves(x) * sin_signed


def _head_sumsq(x, ones_blockdiag):
    x2 = x * x
    hi = x2.astype(bf16)
    lo = (x2 - hi.astype(f32)).astype(bf16)
    return (jnp.dot(hi, ones_blockdiag, preferred_element_type=f32)
            + jnp.dot(lo, ones_blockdiag, preferred_element_type=f32))


def _dot_nt(a, b):
    return lax.dot_general(a, b, (((1,), (1,)), ((), ())), preferred_element_type=f32)


def _dot_tn(a, b):
    return lax.dot_general(a, b, (((0,), (0,)), ((), ())), preferred_element_type=f32)


def _ret_state_body(cpb, kf_ref, vf_ref, kb_ref, vb_ref, cosf_ref, sinf_ref, cosb_ref, sinb_ref,
                    wkf_ref, wkb_ref, cdf_ref, cdb_ref, bm_ref, s0f_ref, s0b_ref,
                    sf_ref, sb_ref, ff_ref, fb_ref, accf, accb):
    n = pl.program_id(1)

    @pl.when(n == 0)
    def _():
        accf[...] = s0f_ref[0]
        accb[...] = s0b_ref[0]

    def one_chunk(c, k_ref, v_ref, cos_ref, sin_ref, wk_ref, cd_ref, out_ref, acc):
        rows = slice(CHUNK * c, CHUNK * (c + 1))
        k = k_ref[0, rows, :].astype(f32)
        v = v_ref[0, rows, :]
        for p in range(RET_HEADS // 2):
            kp = _rope(k[:, LANES * p:LANES * (p + 1)], cos_ref[rows, :], sin_ref[rows, :])
            kw = (kp * wk_ref[p]).astype(bf16)
            upd = _dot_tn(kw, v[:, 2 * LANES * p:2 * LANES * (p + 1)])
            s = acc[p]
            out_ref[0, c, p] = s.astype(out_ref.dtype)
            acc[p] = cd_ref[p] * s + bm_ref[...] * upd

    for c in range(cpb):
        one_chunk(c, kf_ref, vf_ref, cosf_ref, sinf_ref, wkf_ref, cdf_ref, sf_ref, accf)
        one_chunk(cpb - 1 - c, kb_ref, vb_ref, cosb_ref, sinb_ref, wkb_ref, cdb_ref, sb_ref, accb)

    @pl.when(n == pl.num_programs(1) - 1)
    def _():
        ff_ref[0] = accf[...]
        fb_ref[0] = accb[...]


def _ret_states(z, lay, cos, sin, tabs, s0f, s0b):
    B, T, _ = z.shape
    cpb = _chunks_per_step(T)
    rows = cpb * CHUNK
    nc = T // rows
    kq = lay['ret_k'][0] // (RET_HEADS * HEAD_DIM)
    vq = lay['ret_v'][0] // lay['ret_v'][1]
    vw = lay['ret_v'][1]
    kw = RET_HEADS * HEAD_DIM
    npair = RET_HEADS // 2
    st_shape = (npair, LANES, 2 * LANES)
    const3 = lambda b, n: (0, 0, 0)
    return pl.pallas_call(
        functools.partial(_ret_state_body, cpb),
        grid=(B, nc),
        in_specs=[
            pl.BlockSpec((1, rows, kw), lambda b, n: (b, n, kq)),
            pl.BlockSpec((1, rows, vw), lambda b, n: (b, n, vq)),
            pl.BlockSpec((1, rows, kw), lambda b, n: (b, nc - 1 - n, kq)),
            pl.BlockSpec((1, rows, vw), lambda b, n: (b, nc - 1 - n, vq)),
            pl.BlockSpec((rows, LANES), lambda b, n: (n, 0)),
            pl.BlockSpec((rows, LANES), lambda b, n: (n, 0)),
            pl.BlockSpec((rows, LANES), lambda b, n: (nc - 1 - n, 0)),
            pl.BlockSpec((rows, LANES), lambda b, n: (nc - 1 - n, 0)),
            pl.BlockSpec((npair, CHUNK, LANES), const3),
            pl.BlockSpec((npair, CHUNK, LANES), const3),
            pl.BlockSpec(st_shape, const3),
            pl.BlockSpec(st_shape, const3),
            pl.BlockSpec((LANES, 2 * LANES), lambda b, n: (0, 0)),
            pl.BlockSpec((1,) + st_shape, lambda b, n: (b, 0, 0, 0)),
            pl.BlockSpec((1,) + st_shape, lambda b, n: (b, 0, 0, 0)),
        ],
        out_specs=[
            pl.BlockSpec((1, cpb) + st_shape, lambda b, n: (b, n, 0, 0, 0)),
            pl.BlockSpec((1, cpb) + st_shape, lambda b, n: (b, nc - 1 - n, 0, 0, 0)),
            pl.BlockSpec((1,) + st_shape, lambda b, n: (b, 0, 0, 0)),
            pl.BlockSpec((1,) + st_shape, lambda b, n: (b, 0, 0, 0)),
        ],
        out_shape=[
            jax.ShapeDtypeStruct((B, nc * cpb) + st_shape, bf16),
            jax.ShapeDtypeStruct((B, nc * cpb) + st_shape, bf16),
            jax.ShapeDtypeStruct((B,) + st_shape, f32),
            jax.ShapeDtypeStruct((B,) + st_shape, f32),
        ],
        scratch_shapes=[pltpu.VMEM(st_shape, f32), pltpu.VMEM(st_shape, f32)],
        compiler_params=_params("parallel", "arbitrary"),
        name="ret_states",
    )(z, z, z, z, cos, sin, cos, sin, tabs['wkf'], tabs['wkb'], tabs['cdf'], tabs['cdb'], tabs['bm'], s0f, s0b)


def _ret_sgu_body(cpb, *refs):
    for c in range(cpb):
        _ret_sgu_chunk(c, *refs)


def _ret_sgu_chunk(c, q_ref, k_ref, v_ref, g_ref, u_ref, w_ref, cos_ref, sin_ref, sf_ref, sb_ref,
                   dm_ref, qdf_ref, qdb_ref, sw_ref, sbias_ref, ret_ref, sgu_ref):
    rows = slice(CHUNK * c, CHUNK * (c + 1))
    q = q_ref[0, rows, :].astype(f32)
    k = k_ref[0, rows, :].astype(f32)
    v = v_ref[0, rows, :]
    g = g_ref[0, rows, :].astype(f32)
    cos = cos_ref[rows, :]
    sin = sin_ref[rows, :]
    first_half = _lane((CHUNK, LANES)) < HEAD_DIM
    for p in range(RET_HEADS // 2):
        qp = _rope(q[:, LANES * p:LANES * (p + 1)], cos, sin)
        kp = _rope(k[:, LANES * p:LANES * (p + 1)], cos, sin).astype(bf16)
        cross = (jnp.dot((qp * qdf_ref[p]).astype(bf16), sf_ref[0, c, p], preferred_element_type=f32)
                 + jnp.dot((qp * qdb_ref[p]).astype(bf16), sb_ref[0, c, p], preferred_element_type=f32))
        for a in range(2):
            h = 2 * p + a
            qm = jnp.where(first_half if a == 0 else ~first_half, qp, 0.0).astype(bf16)
            sc = (_dot_nt(qm, kp) * dm_ref[h]).astype(bf16)
            o = jnp.dot(sc, v[:, LANES * h:LANES * (h + 1)], preferred_element_type=f32)
            o = o + cross[:, LANES * a:LANES * (a + 1)]
            oc = o - jnp.mean(o, axis=-1, keepdims=True)
            y = oc * lax.rsqrt(jnp.mean(oc * oc, axis=-1, keepdims=True) + EPS)
            ret_ref[0, rows, LANES * h:LANES * (h + 1)] = (
                y * jax.nn.silu(g[:, LANES * h:LANES * (h + 1)])).astype(ret_ref.dtype)

    u = jax.nn.gelu(u_ref[0, rows, :].astype(f32))
    w = jax.nn.gelu(w_ref[0, rows, :].astype(f32))
    wc = w - jnp.mean(w, axis=-1, keepdims=True)
    wn = (wc * lax.rsqrt(jnp.mean(wc * wc, axis=-1, keepdims=True) + EPS)).astype(bf16)
    gw = wn.shape[-1] // SGU_GROUPS
    for i in range(SGU_GROUPS):
        sl = slice(gw * i, gw * (i + 1))
        mixed = jnp.dot(sw_ref[i], wn[:, sl], preferred_element_type=f32) + sbias_ref[:, sl]
        sgu_ref[0, rows, sl] = (u[:, sl] * mixed).astype(sgu_ref.dtype)


def _ret_sgu(z, lay, cos, sin, sf, sb, tabs, sgu_w, sgu_bias):
    B, T, _ = z.shape
    cpb = _chunks_per_step(T)
    rows = cpb * CHUNK
    nc = T // rows
    bw = lay['ret_v'][1]

    def zs(name):
        s, w = lay[name]
        return pl.BlockSpec((1, rows, w), lambda b, n, j=s // w: (b, n, j))

    st_shape = (RET_HEADS // 2, LANES, 2 * LANES)
    const3 = lambda b, n: (0, 0, 0)
    out_spec = pl.BlockSpec((1, rows, bw), lambda b, n: (b, n, 0))
    return pl.pallas_call(
        functools.partial(_ret_sgu_body, cpb),
        grid=(B, nc),
        in_specs=[
            zs('ret_q'), zs('ret_k'), zs('ret_v'), zs('ret_g'), zs('sgu_u'), zs('sgu_v'),
            pl.BlockSpec((rows, LANES), lambda b, n: (n, 0)),
            pl.BlockSpec((rows, LANES), lambda b, n: (n, 0)),
            pl.BlockSpec((1, cpb) + st_shape, lambda b, n: (b, n, 0, 0, 0)),
            pl.BlockSpec((1, cpb) + st_shape, lambda b, n: (b, n, 0, 0, 0)),
            pl.BlockSpec((RET_HEADS, CHUNK, CHUNK), const3),
            pl.BlockSpec((RET_HEADS // 2, CHUNK, LANES), const3),
            pl.BlockSpec((RET_HEADS // 2, CHUNK, LANES), const3),
            pl.BlockSpec((SGU_GROUPS, CHUNK, CHUNK), const3),
            pl.BlockSpec((CHUNK, bw), lambda b, n: (0, 0)),
        ],
        out_specs=[out_spec, out_spec],
        out_shape=[jax.ShapeDtypeStruct((B, T, bw), bf16), jax.ShapeDtypeStruct((B, T, bw), bf16)],
        compiler_params=_params("parallel", "parallel"),
        name="ret_sgu",
    )(z, z, z, z, z, z, cos, sin, sf, sb, tabs['dm'], tabs['qdf'], tabs['qdb'], sgu_w, sgu_bias)


def _att_body(local, sink_ref, q_ref, *refs):
    if local:
        (kp_ref, kc_ref, kn_ref, vp_ref, vc_ref, vn_ref, qa_ref, qb_ref,
         kap_ref, kbp_ref, kac_ref, kbc_ref, kan_ref, kbn_ref, ck_ref, cv_ref, gk_ref, ones_ref, o_ref) = refs
    else:
        qa_ref, ck_ref, cv_ref, gk_ref, ones_ref, o_ref = refs
    n = pl.program_id(1)
    nb = pl.num_programs(1)
    ones = ones_ref[...]
    inv_hd = 1.0 / HEAD_DIM

    def norm_rope(x_ref_val, ta, tb):
        x = x_ref_val.astype(f32)
        r = lax.rsqrt(_head_sumsq(x, ones) * inv_hd + EPS)
        y = x * ta if tb is None else x * ta + _swap_halves(x) * tb
        return y * r

    def dup(x):
        lo = _lane(x.shape) < HEAD_DIM
        xr = pltpu.roll(x, HEAD_DIM, 1)
        return jnp.where(lo, x, xr).astype(bf16), jnp.where(lo, xr, x).astype(bf16)

    ks = [norm_rope(ck_ref[0], gk_ref[...], None)]
    vs = [cv_ref[0]]
    n_loc = 0
    if local:
        ks = [norm_rope(kp_ref[0], kap_ref[...], kbp_ref[...]), norm_rope(kc_ref[0], kac_ref[...], kbc_ref[...]),
              norm_rope(kn_ref[0], kan_ref[...], kbn_ref[...])] + ks
        vs = [vp_ref[0], vc_ref[0], vn_ref[0]] + vs
        n_loc = 3 * CHUNK
    k_dup = dup(jnp.concatenate(ks, axis=0))
    v_dup = dup(jnp.concatenate(vs, axis=0).astype(f32))

    n_groups = q_ref.shape[-1] // LANES
    groups_per_kv = n_groups // ATT_KV_HEADS
    rows = 2 * groups_per_kv * CHUNK
    if local:
        i = lax.broadcasted_iota(jnp.int32, (rows, n_loc), 0) & (CHUNK - 1)
        j = lax.broadcasted_iota(jnp.int32, (rows, n_loc), 1)
        has_prev = (n > 0).astype(jnp.int32)
        has_next = (n < nb - 1).astype(jnp.int32)
        valid = (j >= i * has_prev + CHUNK * (1 - has_prev)) & (j <= 2 * CHUNK - 1 + has_next * (i + 1))

    log2e = 1.4426950408889634
    first_half = _lane((CHUNK, LANES)) < HEAD_DIM
    for h in range(ATT_KV_HEADS):
        q_rows, sink_rows = [], []
        for c in range(groups_per_kv * h, groups_per_kv * (h + 1)):
            xq = q_ref[0, :, LANES * c:LANES * (c + 1)]
            qn = norm_rope(xq, qa_ref[...], qb_ref[...] if local else None) * (HEAD_DIM ** -0.5 * log2e)
            for a in range(2):
                q_rows.append(jnp.where(first_half if a == 0 else ~first_half, qn, 0.0).astype(bf16))
                sink_rows.append(jnp.full((CHUNK, 1), sink_ref[2 * c + a] * log2e, f32))
        s = _dot_nt(jnp.concatenate(q_rows, axis=0), k_dup[h])
        sk = jnp.concatenate(sink_rows, axis=0)
        s_ctx = s[:, n_loc:]
        m = jnp.maximum(jnp.max(s_ctx, axis=-1, keepdims=True), sk)
        if local:
            s_loc = jnp.where(valid, s[:, :n_loc], NEG_INF)
            m = jnp.maximum(m, jnp.max(s_loc, axis=-1, keepdims=True))
        p_ctx = jnp.exp2(s_ctx - m)
        d = jnp.sum(p_ctx, axis=-1, keepdims=True) + jnp.exp2(sk - m)
        o = jnp.dot(p_ctx.astype(bf16), v_dup[h][n_loc:], preferred_element_type=f32)
        if local:
            p_loc = jnp.exp2(s_loc - m)
            d = d + jnp.sum(p_loc, axis=-1, keepdims=True)
            o = o + jnp.dot(p_loc.astype(bf16), v_dup[h][:n_loc], preferred_element_type=f32)
        o = o * (1.0 / d)
        for ci in range(groups_per_kv):
            c = groups_per_kv * h + ci
            o0 = o[2 * CHUNK * ci:2 * CHUNK * ci + CHUNK]
            o1 = o[2 * CHUNK * ci + CHUNK:2 * CHUNK * (ci + 1)]
            o_ref[0, :, LANES * c:LANES * (c + 1)] = jnp.where(first_half, o0, o1).astype(o_ref.dtype)


def _attention(z, zc, lay, att_tabs, sink, local):
    B, T, _ = z.shape
    L = zc.shape[1]
    nb = T // CHUNK
    qs, qw = lay['att_q']
    ks, kw = lay['att_k']
    vs, _ = lay['att_v']
    kj, vj = ks // kw, vs // kw
    prev = lambda n: jnp.maximum(n - 1, 0)
    nxt = lambda n: jnp.minimum(n + 1, nb - 1)
    tab = lambda f: pl.BlockSpec((CHUNK, LANES), lambda b, n: (f(n), 0))
    same = lambda n: n
    in_specs = [pl.BlockSpec(memory_space=pltpu.SMEM),
                pl.BlockSpec((1, CHUNK, qw), lambda b, n: (b, n, qs // qw))]
    args = [sink, z]
    if local:
        in_specs += [pl.BlockSpec((1, CHUNK, kw), lambda b, n, f=f, j=j: (b, f(n), j))
                     for j in (kj, vj) for f in (prev, same, nxt)]
        args += [z] * 6
        in_specs += [tab(same), tab(same), tab(prev), tab(prev), tab(same), tab(same), tab(nxt), tab(nxt)]
        args += [att_tabs['qa'], att_tabs['qb'], att_tabs['ka'], att_tabs['kb'], att_tabs['ka'], att_tabs['kb'],
                 att_tabs['ka'], att_tabs['kb']]
    else:
        in_specs += [pl.BlockSpec((1, LANES), lambda b, n: (0, 0))]
        args += [att_tabs['gq']]
    in_specs += [pl.BlockSpec((1, L, kw), lambda b, n: (b, 0, kj)),
                 pl.BlockSpec((1, L, kw), lambda b, n: (b, 0, vj)),
                 pl.BlockSpec((1, LANES), lambda b, n: (0, 0)),
                 pl.BlockSpec((LANES, LANES), lambda b, n: (0, 0))]
    args += [zc, zc, att_tabs['gk'], att_tabs['ones']]
    return pl.pallas_call(
        functools.partial(_att_body, local),
        grid=(B, nb),
        in_specs=in_specs,
        out_specs=pl.BlockSpec((1, CHUNK, qw), lambda b, n: (b, n, 0)),
        out_shape=jax.ShapeDtypeStruct((B, T, qw), bf16),
        compiler_params=_params("parallel", "parallel"),
        name="attention" if local else "ctx_attention",
    )(*args)


def _merge_body(n_gate_blocks, ret_ref, sgu_ref, att_ref, *refs):
    g_refs = refs[:N_BRANCHES * n_gate_blocks]
    x_ref, wb_ref, wo_ref, gate_ref, scale_ref, shift_ref, wr_ref, xo_ref, h2_ref, aff_ref = refs[len(g_refs):]
    merged = None
    for i, br_ref in enumerate((ret_ref, sgu_ref, att_ref)):
        p = jnp.dot(br_ref[0], wb_ref[i], preferred_element_type=f32)
        g = jnp.concatenate([r[0] for r in g_refs[n_gate_blocks * i:n_gate_blocks * (i + 1)]], axis=1)
        t = jax.nn.sigmoid(g.astype(f32)) * p
        merged = t if merged is None else merged + t
    o = jnp.dot(merged.astype(bf16), wo_ref[...], preferred_element_type=f32)
    x = x_ref[0] + gate_ref[0] * o
    xo_ref[0] = x
    y = x * lax.rsqrt(jnp.mean(x * x, axis=-1, keepdims=True) + EPS)
    h2 = y * scale_ref[0] + shift_ref[0]
    h2_ref[0, :, 0, :] = h2
    logits = jnp.dot(h2.astype(bf16), wr_ref[...], preferred_element_type=f32)
    logits = jnp.where(_lane(logits.shape) < N_EXPERTS, logits, NEG_INF)
    m = jnp.max(logits, axis=-1, keepdims=True)
    e = jnp.exp(logits - m)
    aff_ref[0] = e / jnp.sum(e, axis=-1, keepdims=True)


def _merge(ret, sgu, att, z, gates_at, x, wb, wo, gate, scale, shift, wr):
    B, T, D = x.shape
    bw = ret.shape[-1]
    tm = _pick_tile(T, 512)
    tok = lambda b, i: (b, i, 0)
    vec = lambda b, i: (b, 0, 0)
    gw = 2 * LANES
    assert gates_at % gw == 0 and D % gw == 0
    n_gate_blocks = D // gw
    gate_specs = [pl.BlockSpec((1, tm, gw), lambda b, i, j=gates_at // gw + k: (b, i, j))
                  for k in range(N_BRANCHES * n_gate_blocks)]
    return pl.pallas_call(
        functools.partial(_merge_body, n_gate_blocks),
        grid=(B, T // tm),
        in_specs=[
            pl.BlockSpec((1, tm, bw), tok), pl.BlockSpec((1, tm, bw), tok), pl.BlockSpec((1, tm, bw), tok),
            *gate_specs,
            pl.BlockSpec((1, tm, D), tok),
            pl.BlockSpec((N_BRANCHES, bw, D), lambda b, i: (0, 0, 0)),
            pl.BlockSpec((D, D), lambda b, i: (0, 0)),
            pl.BlockSpec((1, 1, D), vec), pl.BlockSpec((1, 1, D), vec), pl.BlockSpec((1, 1, D), vec),
            pl.BlockSpec((D, LANES), lambda b, i: (0, 0)),
        ],
        out_specs=[
            pl.BlockSpec((1, tm, D), tok), pl.BlockSpec((1, tm, 1, D), lambda b, i: (b, i, 0, 0)),
            pl.BlockSpec((1, tm, LANES), tok),
        ],
        out_shape=[
            jax.ShapeDtypeStruct((B, T, D), f32),
            jax.ShapeDtypeStruct((B, T, 1, D), f32),
            jax.ShapeDtypeStruct((B, T, LANES), f32),
        ],
        compiler_params=_params("parallel", "parallel"),
        name="merge",
    )(ret, sgu, att, *([z] * len(gate_specs)), x, wb, wo, gate, scale, shift, wr)


ROUTE_TILE = 256
AFF_BITS = 31
MIN_NORMAL_BITS = 0x00800000


def _route_body(cap, aff_ref, tri_ref, pos_ref, post_ref, c0_ref):
    T = aff_ref.shape[1]
    tt = tri_ref.shape[0]

    def as_float(bits):
        return pltpu.bitcast(jnp.broadcast_to(bits, (8, LANES)), f32)[:1]

    def refine(i, thr):
        cand = thr | jnp.left_shift(jnp.int32(1), AFF_BITS - 1 - i)
        above = jnp.where(aff_ref[0] >= as_float(cand), 1.0, 0.0)
        return jnp.where(jnp.sum(above, axis=0, keepdims=True) >= cap, cand, thr)

    thr = lax.fori_loop(0, AFF_BITS, refine, jnp.zeros((1, LANES), jnp.int32))
    thr_f = as_float(thr)
    hi_f = as_float(jnp.maximum(thr + 1, MIN_NORMAL_BITS))
    need = cap - jnp.sum(jnp.where(aff_ref[0] >= hi_f, 1.0, 0.0), axis=0, keepdims=True)
    tri = tri_ref[...]
    carry_eq = jnp.zeros((1, LANES), f32)
    carry_sel = jnp.zeros((1, LANES), f32)
    for j in range(T // tt):
        a = aff_ref[0, tt * j:tt * (j + 1), :]
        eq = (a >= thr_f) & (a < hi_f)
        eqf = jnp.where(eq, 1.0, 0.0)
        eq_rank = carry_eq + jnp.dot(tri, eqf.astype(bf16), preferred_element_type=f32) - eqf
        sel = (a >= hi_f) | (eq & (eq_rank < need))
        self_ = jnp.where(sel, 1.0, 0.0)
        pos = carry_sel + jnp.dot(tri, self_.astype(bf16), preferred_element_type=f32) - self_
        pos = jnp.where(sel, pos, -1.0).astype(jnp.int32)
        pos_ref[0, tt * j:tt * (j + 1), :] = pos
        post_ref[0, j] = pos.T[:N_EXPERTS]
        c0_ref[0, j:j + 1, :] = carry_sel.astype(jnp.int32)
        carry_eq = carry_eq + jnp.sum(eqf, axis=0, keepdims=True)
        carry_sel = carry_sel + jnp.sum(self_, axis=0, keepdims=True)


def _route(aff, cap):
    B, T, _ = aff.shape
    tt = min(ROUTE_TILE, T)
    nt = T // tt
    tri = jnp.tril(jnp.ones((tt, tt), bf16))
    return pl.pallas_call(
        functools.partial(_route_body, cap),
        grid=(B,),
        in_specs=[pl.BlockSpec((1, T, LANES), lambda b: (b, 0, 0)), pl.BlockSpec((tt, tt), lambda b: (0, 0))],
        out_specs=[
            pl.BlockSpec((1, T, LANES), lambda b: (b, 0, 0)),
            pl.BlockSpec((1, nt, N_EXPERTS, tt), lambda b: (b, 0, 0, 0)),
            pl.BlockSpec((1, nt, LANES), lambda b: (b, 0, 0)),
        ],
        out_shape=[
            jax.ShapeDtypeStruct((B, T, LANES), jnp.int32),
            jax.ShapeDtypeStruct((B, nt, N_EXPERTS, tt), jnp.int32),
            jax.ShapeDtypeStruct((B, nt, LANES), jnp.int32),
        ],
        compiler_params=_params("parallel"),
        name="route",
    )(aff, tri)


def _slots_body(nt, cap, c0_ref, post_ref, idx_ref):
    b = pl.program_id(0)
    tt = post_ref.shape[-1]
    tok = lax.broadcasted_iota(jnp.int32, (LANES, tt), 1)
    diag = lax.broadcasted_iota(jnp.int32, (LANES, LANES), 0) == lax.broadcasted_iota(jnp.int32, (LANES, LANES), 1)

    def per_expert(e, carry):
        def tiles_starting_at_or_before(slot):
            return sum((c0_ref[(b * nt + j) * N_EXPERTS + e] <= slot).astype(jnp.int32) for j in range(nt))

        for k in range(idx_ref.shape[2]):
            slot = LANES * k + lax.broadcasted_iota(jnp.int32, (LANES, 1), 0)

            def match(j, acc):
                return acc + jnp.where(post_ref[0, j, pl.ds(e, 1), :] == slot, tok + j * tt, 0)

            j_lo = tiles_starting_at_or_before(LANES * k) - 1
            j_hi = tiles_starting_at_or_before(min(LANES * (k + 1), cap) - 1)
            acc = lax.fori_loop(j_lo, j_hi, match, jnp.zeros((LANES, tt), jnp.int32))
            col = jnp.sum(acc.astype(f32), axis=1, keepdims=True)
            row = jnp.sum(jnp.where(diag, col, 0.0), axis=0, keepdims=True)
            idx_ref[0, e, k:k + 1, :] = row.astype(jnp.int32)
        return carry

    lax.fori_loop(0, N_EXPERTS, per_expert, 0)


def _slots(c0s, post, cap):
    B, nt, E, tt = post.shape
    nblk = -(-cap // LANES)
    idx = pl.pallas_call(
        functools.partial(_slots_body, nt, cap),
        grid_spec=pltpu.PrefetchScalarGridSpec(
            num_scalar_prefetch=1,
            grid=(B,),
            in_specs=[pl.BlockSpec((1, nt, E, tt), lambda b, c0: (b, 0, 0, 0))],
            out_specs=pl.BlockSpec((1, E, nblk, LANES), lambda b, c0: (b, 0, 0, 0)),
        ),
        out_shape=jax.ShapeDtypeStruct((B, E, nblk, LANES), jnp.int32),
        compiler_params=_params("arbitrary"),
        name="slots",
    )(c0s, post)
    return idx.reshape(B, E, nblk * LANES)[:, :, :cap]


def _expert_body(cap, idx_ref, h_ref, wg_ref, wu_ref, wd_ref, y_ref, rows, wg_s, wu_s, wd_s, sem):
    n_b = pl.num_programs(1)
    step = pl.program_id(0) * n_b + pl.program_id(1)
    last = pl.num_programs(0) * n_b - 1
    T = h_ref.shape[0] // n_b

    def start_rows(s, buf):
        e, b = s // n_b, s % n_b
        for r in range(cap):
            t = idx_ref[(b * N_EXPERTS + e) * cap + r]
            pltpu.make_async_copy(h_ref.at[b * T + t], rows.at[buf, pl.ds(r, 1), :], sem.at[buf]).start()

    def wait_rows(buf):
        pltpu.make_async_copy(h_ref.at[pl.ds(0, cap), 0, :], rows.at[buf], sem.at[buf]).wait()

    @pl.when(step == 0)
    def _():
        start_rows(step, 0)

    @pl.when(pl.program_id(1) == 0)
    def _():
        wg_s[...] = wg_ref[0, 0].astype(bf16)
        wu_s[...] = wu_ref[0, 0].astype(bf16)
        wd_s[...] = wd_ref[0, 0].astype(bf16)

    for cur in range(2):
        @pl.when(step % 2 == cur)
        def _():
            wait_rows(cur)
            xs = rows[cur].astype(bf16)

            @pl.when(step < last)
            def _():
                start_rows(step + 1, 1 - cur)

            a = jnp.dot(xs, wg_s[...], preferred_element_type=f32)
            u = jnp.dot(xs, wu_s[...], preferred_element_type=f32)
            hmid = (jax.nn.silu(a) * u).astype(bf16)
            y_ref[0, 0] = jnp.dot(hmid, wd_s[...], preferred_element_type=f32).astype(y_ref.dtype)


def _expert_ffn(idx, h2, layer, wg, wu, wd):
    B, T, _, D = h2.shape
    _, E, _, F = wg.shape
    cap = idx.shape[-1]
    wsp = lambda e, b, idx: (layer, e, 0, 0)
    return pl.pallas_call(
        functools.partial(_expert_body, cap),
        grid_spec=pltpu.PrefetchScalarGridSpec(
            num_scalar_prefetch=1,
            grid=(E, B),
            in_specs=[
                pl.BlockSpec(memory_space=pl.ANY),
                pl.BlockSpec((1, 1, D, F), wsp), pl.BlockSpec((1, 1, D, F), wsp), pl.BlockSpec((1, 1, F, D), wsp),
            ],
            out_specs=pl.BlockSpec((1, 1, cap, D), lambda e, b, idx: (b, e, 0, 0)),
            scratch_shapes=[pltpu.VMEM((2, cap, D), f32), pltpu.VMEM((D, F), bf16), pltpu.VMEM((D, F), bf16),
                            pltpu.VMEM((F, D), bf16), pltpu.SemaphoreType.DMA((2,))],
        ),
        out_shape=jax.ShapeDtypeStruct((B, E, cap, D), bf16),
        compiler_params=pltpu.CompilerParams(dimension_semantics=("arbitrary", "arbitrary"),
                                             vmem_limit_bytes=EXPERT_VMEM_LIMIT_BYTES),
        name="expert_ffn",
    )(idx.reshape(-1), h2.reshape(B * T, 1, D), wg, wu, wd)


Y_ROW_ALIGN = 16


def _combine_body(nt, cap, win, c0_ref, pos_ref, aff_ref, x_ref, gate_ref, y_ref, o_ref, buf, extra, sem, sem_x):
    b = pl.program_id(0)
    j = pl.program_id(1)
    step = b * nt + j
    tt = pos_ref.shape[1]
    stride = win if win == cap else win - Y_ROW_ALIGN

    def c0(bb, jj, e):
        return c0_ref[(bb * nt + jj) * N_EXPERTS + e]

    def window_start(lo):
        return pl.multiple_of(jnp.minimum(lo & (-Y_ROW_ALIGN), cap - win), Y_ROW_ALIGN)

    def copies(bb, jj, p, dst, s):
        return [pltpu.make_async_copy(
            y_ref.at[bb, e, pl.ds(window_start(c0(bb, jj, e) + p * stride), win), :],
            dst.at[pl.ds(e * win, win), :], s) for e in range(N_EXPERTS)]

    @pl.when(step == 0)
    def _():
        for cp in copies(b, j, 0, buf.at[0], sem.at[0]):
            cp.start()

    nxt = step + 1

    @pl.when(nxt < pl.num_programs(0) * nt)
    def _():
        for cp in copies(nxt // nt, nxt % nt, 0, buf.at[nxt % 2], sem.at[nxt % 2]):
            cp.start()

    lane = _lane((1, LANES))
    pos = pos_ref[0]
    aff = aff_ref[0]
    per_group = LANES // win

    def lane_vector(fn):
        v = jnp.zeros((1, LANES), jnp.int32)
        for e in range(N_EXPERTS):
            v = jnp.where(lane == e, fn(e), v)
        return v

    def weights(p):
        lo = lane_vector(lambda e: c0(b, j, e) + p * stride)
        start = lane_vector(lambda e: window_start(c0(b, j, e) + p * stride))
        rel = jnp.where((pos >= lo) & (pos < lo + stride), pos - start, -LANES)
        lanes_full = _lane((tt, LANES))
        groups = []
        for g in range(N_EXPERTS // per_group):
            q = jnp.zeros((tt, LANES), f32)
            for i in range(per_group):
                e = g * per_group + i
                q = jnp.where(rel[:, e:e + 1] + i * win == lanes_full, aff[:, e:e + 1], q)
            groups.append(q.astype(bf16))
        return jnp.concatenate(groups, axis=1)

    for cp in copies(b, j, 0, buf.at[step % 2], sem.at[step % 2]):
        cp.wait()
    moe = jnp.dot(weights(0), buf[step % 2], preferred_element_type=f32)

    def count(e):
        end = jnp.where(j + 1 < nt, c0(b, jnp.minimum(j + 1, nt - 1), e), cap)
        return end - c0(b, j, e)

    max_count = functools.reduce(jnp.maximum, [count(e) for e in range(N_EXPERTS)])

    def more(p, acc):
        cps = copies(b, j, p, extra, sem_x.at[0])
        for cp in cps:
            cp.start()
        for cp in cps:
            cp.wait()
        return acc + jnp.dot(weights(p), extra[...], preferred_element_type=f32)

    moe = lax.fori_loop(1, (max_count + stride - 1) // stride, more, moe)
    o_ref[0] = x_ref[0] + gate_ref[0] * moe


def _combine(c0s, pos, aff, x, gate, y):
    B, T, D = x.shape
    E, cap = y.shape[1], y.shape[2]
    tt = min(ROUTE_TILE, T)
    nt = T // tt
    win = min(cap, LANES // 2)
    assert (cap - win) % Y_ROW_ALIGN == 0 and LANES % win == 0
    tok = lambda b, j, c0: (b, j, 0)
    return pl.pallas_call(
        functools.partial(_combine_body, nt, cap, win),
        grid_spec=pltpu.PrefetchScalarGridSpec(
            num_scalar_prefetch=1,
            grid=(B, nt),
            in_specs=[
                pl.BlockSpec((1, tt, LANES), tok), pl.BlockSpec((1, tt, LANES), tok), pl.BlockSpec((1, tt, D), tok),
                pl.BlockSpec((1, 1, D), lambda b, j, c0: (b, 0, 0)),
                pl.BlockSpec(memory_space=pl.ANY),
            ],
            out_specs=pl.BlockSpec((1, tt, D), tok),
            scratch_shapes=[pltpu.VMEM((2, E * win, D), bf16), pltpu.VMEM((E * win, D), bf16),
                            pltpu.SemaphoreType.DMA((2,)), pltpu.SemaphoreType.DMA((1,))],
        ),
        out_shape=jax.ShapeDtypeStruct((B, T, D), f32),
        compiler_params=_params("arbitrary", "arbitrary"),
        name="combine",
    )(c0s, pos, aff, x, gate, y)


def _expert_choice(h2, aff, x, gate2, p):
    T = x.shape[1]
    cap = CAPACITY_FACTOR * T // N_EXPERTS
    pos, post, c0 = _route(aff, cap)
    c0s = c0[:, :, :N_EXPERTS].reshape(-1)
    y = _expert_ffn(_slots(c0s, post, cap), h2, p['layer'], p['w_exp_gate'], p['w_exp_up'], p['w_exp_down'])
    return _combine(c0s, pos, aff, x, gate2, y)


def _rope_tables(T):
    rows = T // GRID_W
    row = jnp.repeat(jnp.arange(rows), GRID_W).astype(f32)
    col = jnp.tile(jnp.arange(GRID_W), rows).astype(f32)
    n_freq = HEAD_DIM // 4
    inv = jnp.power(ROPE_BASE, -jnp.arange(n_freq, dtype=f32) / n_freq)
    ang = jnp.concatenate([row[:, None] * inv, col[:, None] * inv], axis=-1)
    cos, sin = jnp.cos(ang), jnp.sin(ang)
    reps = LANES // HEAD_DIM
    return jnp.tile(jnp.concatenate([cos, cos], axis=-1), (1, reps)), jnp.tile(
        jnp.concatenate([-sin, sin], axis=-1), (1, reps))


def _identity_rope(T):
    return jnp.ones((T, LANES), f32), jnp.zeros((T, LANES), f32)


def _retention_tables(decay_fwd, decay_bwd):
    lgf = jax.nn.log_sigmoid(decay_fwd.astype(f32))
    lgb = jax.nn.log_sigmoid(decay_bwd.astype(f32))
    C = CHUNK
    k_scale = HEAD_DIM ** -0.5
    idx = jnp.arange(C, dtype=f32)
    npair = RET_HEADS // 2

    def lanes(t):
        return jnp.repeat(t.reshape(npair, 2, C).transpose(0, 2, 1), HEAD_DIM, axis=-1)

    diff = idx[:, None] - idx[None, :]
    dm = jnp.where(diff >= 0, jnp.exp(jnp.maximum(diff, 0.0)[None] * lgf[:, None, None]),
                   jnp.exp(jnp.maximum(-diff, 0.0)[None] * lgb[:, None, None])) * k_scale

    def chunk_decay(lg):
        rows = jnp.repeat(jnp.exp(C * lg).reshape(npair, 2), HEAD_DIM, axis=-1)
        return jnp.broadcast_to(rows[:, :, None], (npair, LANES, 2 * LANES))

    r = jnp.arange(LANES)[:, None] // HEAD_DIM
    c = jnp.arange(2 * LANES)[None, :] // LANES
    return {
        'wkf': lanes(jnp.exp((C - 1 - idx)[None, :] * lgf[:, None])) * k_scale,
        'wkb': lanes(jnp.exp(idx[None, :] * lgb[:, None])) * k_scale,
        'qdf': lanes(jnp.exp((idx + 1.0)[None, :] * lgf[:, None])),
        'qdb': lanes(jnp.exp((C - idx)[None, :] * lgb[:, None])),
        'dm': dm, 'cdf': chunk_decay(lgf), 'cdb': chunk_decay(lgb), 'bm': (r == c).astype(f32),
    }


def _attention_tables(cos, sin, q_gain, k_gain):
    reps = LANES // HEAD_DIM
    half = HEAD_DIM // 2
    swap = lambda g: jnp.concatenate([g[half:], g[:half]])
    gq, gk = jnp.tile(q_gain, reps)[None], jnp.tile(k_gain, reps)[None]
    gqs, gks = jnp.tile(swap(q_gain), reps)[None], jnp.tile(swap(k_gain), reps)[None]
    r = jnp.arange(LANES) // HEAD_DIM
    return {'qa': cos * gq, 'qb': sin * gqs, 'ka': cos * gk, 'kb': sin * gks, 'gq': gq, 'gk': gk,
            'ones': (r[:, None] == r[None, :]).astype(bf16)}


def _layer(x, xc, mod_lat, mod_ctx, rope, p, last):
    B, S, D = x.shape
    L = xc.shape[1]
    lay = _layout(D)
    sh1, sc1, g1, sh2, sc2, g2 = jnp.split(mod_lat, 6, axis=-1)
    csh1, csc1, cg1, csh2, csc2, cg2 = jnp.split(mod_ctx, 6, axis=-1)
    n1 = p['norm1_g'][None, None, :]
    n2 = p['norm2_g'][None, None, :]

    def bcast(t):
        return jnp.broadcast_to(t, (B, 1, D))

    z = _proj_in(x, n1 * (1 + sc1), sh1, p['w_in'])
    zc = _proj_in(xc, bcast(n1 * (1 + csc1)), bcast(csh1), p['w_in'])

    cos, sin = rope
    cos_c, sin_c = _identity_rope(L)
    rtabs = _retention_tables(p['ret_decay_fwd'], p['ret_decay_bwd'])
    atabs = _attention_tables(cos, sin, p['q_norm_g'], p['k_norm_g'])
    sgu_w = p['sgu_w'].astype(bf16)
    gw = D // 2 // SGU_GROUPS
    sgu_bias = jnp.repeat(p['sgu_b'].T, gw, axis=-1)

    zero = jnp.zeros((B, RET_HEADS // 2, LANES, 2 * LANES), f32)
    sfc, sbc, s_f, s_b = _ret_states(zc, lay, cos_c, sin_c, rtabs, zero, zero)
    sf, sb, _, _ = _ret_states(z, lay, cos, sin, rtabs, s_f, s_b)
    ret, sgu = _ret_sgu(z, lay, cos, sin, sf, sb, rtabs, sgu_w, sgu_bias)
    att = _attention(z, zc, lay, atabs, p['attn_sink'], True)

    x, h2, aff = _merge(ret, sgu, att, z, lay['gates'][0], x, p['w_branch'], p['w_out'], g1, n2 * (1 + sc2), sh2,
                        p['w_router'])
    x = _expert_choice(h2, aff, x, g2, p)
    if last:
        return x, None

    ret_c, sgu_c = _ret_sgu(zc, lay, cos_c, sin_c, sfc, sbc, rtabs, sgu_w, sgu_bias)
    att_c = _attention(zc, zc, lay, atabs, p['attn_sink'], False)
    xc, h2c, affc = _merge(ret_c, sgu_c, att_c, zc, lay['gates'][0], xc, p['w_branch'], p['w_out'], bcast(cg1),
                           bcast(n2 * (1 + csc2)), bcast(csh2), p['w_router'])
    xc = _expert_choice(h2c, affc, xc, bcast(cg2), p)
    return x, xc


def kernel(x, c, ctx, c_ctx, w_mod, b_mod, norm1_g, norm2_g, w_in, ret_decay_fwd, ret_decay_bwd, sgu_w, sgu_b,
           q_norm_g, k_norm_g, attn_sink, w_branch, w_out, w_router, w_exp_gate, w_exp_up, w_exp_down):
    B, S, D = x.shape
    depth = w_in.shape[0]
    rope = _rope_tables(S)
    c_act = jax.nn.silu(c)
    c_ctx_act = jax.nn.silu(c_ctx)
    xc = ctx
    hp = lax.Precision.HIGHEST
    for l in range(depth):
        wr = jnp.zeros((D, LANES), f32).at[:, :N_EXPERTS].set(w_router[l]).astype(bf16)
        p = {
            'norm1_g': norm1_g[l], 'norm2_g': norm2_g[l], 'w_in': w_in[l].astype(bf16),
            'ret_decay_fwd': ret_decay_fwd[l], 'ret_decay_bwd': ret_decay_bwd[l],
            'sgu_w': sgu_w[l], 'sgu_b': sgu_b[l],
            'q_norm_g': q_norm_g[l], 'k_norm_g': k_norm_g[l], 'attn_sink': attn_sink[l],
            'w_branch': w_branch[l].astype(bf16), 'w_out': w_out[l].astype(bf16), 'w_router': wr,
            'layer': l, 'w_exp_gate': w_exp_gate, 'w_exp_up': w_exp_up, 'w_exp_down': w_exp_down,
        }
        mod_lat = (jnp.dot(c_act, w_mod[l], precision=hp) + b_mod[l])[:, None, :]
        mod_ctx = (jnp.dot(c_ctx_act, w_mod[l], precision=hp) + b_mod[l])[None, None, :]
        x, xc = _layer(x, xc, mod_lat, mod_ctx, rope, p, l == depth - 1)
    return x
```

```python
import functools

import jax
import jax.numpy as jnp
from jax import lax
from jax.experimental import pallas as pl
from jax.experimental.pallas import tpu as pltpu

GRID_W = 64
N_BRANCHES = 3
RET_HEADS = 4
SGU_GROUPS = 4
CHUNK = 128
HEAD_DIM = 64
ATT_KV_HEADS = 2
ROPE_BASE = 10000.0
N_EXPERTS = 16
CAPACITY_FACTOR = 2
EPS = 1e-6
NEG_INF = -1e30

LANES = 128
VMEM_LIMIT_BYTES = 48 * 1024 * 1024
EXPERT_VMEM_LIMIT_BYTES = 56 * 1024 * 1024

f32 = jnp.float32
bf16 = jnp.bfloat16


def _layout(d_model):
    bw = d_model // 2
    qk = RET_HEADS * HEAD_DIM
    kv = ATT_KV_HEADS * HEAD_DIM
    widths = (('ret_q', qk), ('ret_k', qk), ('ret_v', bw), ('ret_g', bw), ('sgu_u', bw), ('sgu_v', bw),
              ('att_q', bw), ('att_k', kv), ('att_v', kv), ('gates', N_BRANCHES * d_model))
    out, start = {}, 0
    for name, w in widths:
        out[name] = (start, w)
        start += w
    return out


def _chunks_per_step(T):
    return 2 if (T // CHUNK) % 2 == 0 else 1


def _pick_tile(n, pref):
    t = min(n, pref)
    while n % t:
        t //= 2
    return t


def _params(*sem):
    return pltpu.CompilerParams(dimension_semantics=sem, vmem_limit_bytes=VMEM_LIMIT_BYTES)


def _proj_in_body(x_ref, scale_ref, shift_ref, w_ref, o_ref, h_ref):
    @pl.when(pl.program_id(2) == 0)
    def _():
        x = x_ref[0]
        y = x * lax.rsqrt(jnp.mean(x * x, axis=-1, keepdims=True) + EPS)
        h_ref[...] = (y * scale_ref[0] + shift_ref[0]).astype(bf16)

    o_ref[0] = jnp.dot(h_ref[...], w_ref[...], preferred_element_type=f32).astype(o_ref.dtype)


def _proj_in(x, scale, shift, w):
    B, T, D = x.shape
    N = w.shape[1]
    tm = _pick_tile(T, 2048)
    tn = 1280 if N % 1280 == 0 else _pick_tile(N, 1024)
    return pl.pallas_call(
        _proj_in_body,
        grid=(B, T // tm, N // tn),
        in_specs=[
            pl.BlockSpec((1, tm, D), lambda b, i, j: (b, i, 0)),
            pl.BlockSpec((1, 1, D), lambda b, i, j: (b, 0, 0)),
            pl.BlockSpec((1, 1, D), lambda b, i, j: (b, 0, 0)),
            pl.BlockSpec((D, tn), lambda b, i, j: (0, j)),
        ],
        out_specs=pl.BlockSpec((1, tm, tn), lambda b, i, j: (b, i, j)),
        out_shape=jax.ShapeDtypeStruct((B, T, N), bf16),
        scratch_shapes=[pltpu.VMEM((tm, D), bf16)],
        compiler_params=_params("parallel", "parallel", "arbitrary"),
        name="proj_in",
    )(x, scale, shift, w)


def _lane(shape):
    return lax.broadcasted_iota(jnp.int32, shape, 1)


def _swap_halves(x):
    return jnp.where((_lane(x.shape) & (HEAD_DIM // 2)) == 0,
                     pltpu.roll(x, LANES - HEAD_DIM // 2, 1), pltpu.roll(x, HEAD_DIM // 2, 1))


def _rope(x, cos, sin_signed):
    return x * cos + _swap_halves(x) * sin_signed


def _head_sumsq(x, ones_blockdiag):
    x2 = x * x
    hi = x2.astype(bf16)
    lo = (x2 - hi.astype(f32)).astype(bf16)
    return (jnp.dot(hi, ones_blockdiag, preferred_element_type=f32)
            + jnp.dot(lo, ones_blockdiag, preferred_element_type=f32))


def _dot_nt(a, b):
    return lax.dot_general(a, b, (((1,), (1,)), ((), ())), preferred_element_type=f32)


def _dot_tn(a, b):
    return lax.dot_general(a, b, (((0,), (0,)), ((), ())), preferred_element_type=f32)


def _ret_state_body(cpb, kf_ref, vf_ref, kb_ref, vb_ref, cosf_ref, sinf_ref, cosb_ref, sinb_ref,
                    wkf_ref, wkb_ref, cdf_ref, cdb_ref, bm_ref, s0f_ref, s0b_ref,
                    sf_ref, sb_ref, ff_ref, fb_ref, accf, accb):
    n = pl.program_id(1)

    @pl.when(n == 0)
    def _():
        accf[...] = s0f_ref[0]
        accb[...] = s0b_ref[0]

    def one_chunk(c, k_ref, v_ref, cos_ref, sin_ref, wk_ref, cd_ref, out_ref, acc):
        rows = slice(CHUNK * c, CHUNK * (c + 1))
        k = k_ref[0, rows, :].astype(f32)
        v = v_ref[0, rows, :]
        for p in range(RET_HEADS // 2):
            kp = _rope(k[:, LANES * p:LANES * (p + 1)], cos_ref[rows, :], sin_ref[rows, :])
            kw = (kp * wk_ref[p]).astype(bf16)
            upd = _dot_tn(kw, v[:, 2 * LANES * p:2 * LANES * (p + 1)])
            s = acc[p]
            out_ref[0, c, p] = s.astype(out_ref.dtype)
            acc[p] = cd_ref[p] * s + bm_ref[...] * upd

    for c in range(cpb):
        one_chunk(c, kf_ref, vf_ref, cosf_ref, sinf_ref, wkf_ref, cdf_ref, sf_ref, accf)
        one_chunk(cpb - 1 - c, kb_ref, vb_ref, cosb_ref, sinb_ref, wkb_ref, cdb_ref, sb_ref, accb)

    @pl.when(n == pl.num_programs(1) - 1)
    def _():
        ff_ref[0] = accf[...]
        fb_ref[0] = accb[...]


def _ret_states(z, lay, cos, sin, tabs, s0f, s0b):
    B, T, _ = z.shape
    cpb = _chunks_per_step(T)
    rows = cpb * CHUNK
    nc = T // rows
    kq = lay['ret_k'][0] // (RET_HEADS * HEAD_DIM)
    vq = lay['ret_v'][0] // lay['ret_v'][1]
    vw = lay['ret_v'][1]
    kw = RET_HEADS * HEAD_DIM
    npair = RET_HEADS // 2
    st_shape = (npair, LANES, 2 * LANES)
    const3 = lambda b, n: (0, 0, 0)
    return pl.pallas_call(
        functools.partial(_ret_state_body, cpb),
        grid=(B, nc),
        in_specs=[
            pl.BlockSpec((1, rows, kw), lambda b, n: (b, n, kq)),
            pl.BlockSpec((1, rows, vw), lambda b, n: (b, n, vq)),
            pl.BlockSpec((1, rows, kw), lambda b, n: (b, nc - 1 - n, kq)),
            pl.BlockSpec((1, rows, vw), lambda b, n: (b, nc - 1 - n, vq)),
            pl.BlockSpec((rows, LANES), lambda b, n: (n, 0)),
            pl.BlockSpec((rows, LANES), lambda b, n: (n, 0)),
            pl.BlockSpec((rows, LANES), lambda b, n: (nc - 1 - n, 0)),
            pl.BlockSpec((rows, LANES), lambda b, n: (nc - 1 - n, 0)),
            pl.BlockSpec((npair, CHUNK, LANES), const3),
            pl.BlockSpec((npair, CHUNK, LANES), const3),
            pl.BlockSpec(st_shape, const3),
            pl.BlockSpec(st_shape, const3),
            pl.BlockSpec((LANES, 2 * LANES), lambda b, n: (0, 0)),
            pl.BlockSpec((1,) + st_shape, lambda b, n: (b, 0, 0, 0)),
            pl.BlockSpec((1,) + st_shape, lambda b, n: (b, 0, 0, 0)),
        ],
        out_specs=[
            pl.BlockSpec((1, cpb) + st_shape, lambda b, n: (b, n, 0, 0, 0)),
            pl.BlockSpec((1, cpb) + st_shape, lambda b, n: (b, nc - 1 - n, 0, 0, 0)),
            pl.BlockSpec((1,) + st_shape, lambda b, n: (b, 0, 0, 0)),
            pl.BlockSpec((1,) + st_shape, lambda b, n: (b, 0, 0, 0)),
        ],
        out_shape=[
            jax.ShapeDtypeStruct((B, nc * cpb) + st_shape, bf16),
            jax.ShapeDtypeStruct((B, nc * cpb) + st_shape, bf16),
            jax.ShapeDtypeStruct((B,) + st_shape, f32),
            jax.ShapeDtypeStruct((B,) + st_shape, f32),
        ],
        scratch_shapes=[pltpu.VMEM(st_shape, f32), pltpu.VMEM(st_shape, f32)],
        compiler_params=_params("parallel", "arbitrary"),
        name="ret_states",
    )(z, z, z, z, cos, sin, cos, sin, tabs['wkf'], tabs['wkb'], tabs['cdf'], tabs['cdb'], tabs['bm'], s0f, s0b)


def _ret_sgu_body(cpb, *refs):
    for c in range(cpb):
        _ret_sgu_chunk(c, *refs)


def _ret_sgu_chunk(c, q_ref, k_ref, v_ref, g_ref, u_ref, w_ref, cos_ref, sin_ref, sf_ref, sb_ref,
                   dm_ref, qdf_ref, qdb_ref, sw_ref, sbias_ref, ret_ref, sgu_ref):
    rows = slice(CHUNK * c, CHUNK * (c + 1))
    q = q_ref[0, rows, :].astype(f32)
    k = k_ref[0, rows, :].astype(f32)
    v = v_ref[0, rows, :]
    g = g_ref[0, rows, :].astype(f32)
    cos = cos_ref[rows, :]
    sin = sin_ref[rows, :]
    first_half = _lane((CHUNK, LANES)) < HEAD_DIM
    for p in range(RET_HEADS // 2):
        qp = _rope(q[:, LANES * p:LANES * (p + 1)], cos, sin)
        kp = _rope(k[:, LANES * p:LANES * (p + 1)], cos, sin).astype(bf16)
        cross = (jnp.dot((qp * qdf_ref[p]).astype(bf16), sf_ref[0, c, p], preferred_element_type=f32)
                 + jnp.dot((qp * qdb_ref[p]).astype(bf16), sb_ref[0, c, p], preferred_element_type=f32))
        for a in range(2):
            h = 2 * p + a
            qm = jnp.where(first_half if a == 0 else ~first_half, qp, 0.0).astype(bf16)
            sc = (_dot_nt(qm, kp) * dm_ref[h]).astype(bf16)
            o = jnp.dot(sc, v[:, LANES * h:LANES * (h + 1)], preferred_element_type=f32)
            o = o + cross[:, LANES * a:LANES * (a + 1)]
            oc = o - jnp.mean(o, axis=-1, keepdims=True)
            y = oc * lax.rsqrt(jnp.mean(oc * oc, axis=-1, keepdims=True) + EPS)
            ret_ref[0, rows, LANES * h:LANES * (h + 1)] = (
                y * jax.nn.silu(g[:, LANES * h:LANES * (h + 1)])).astype(ret_ref.dtype)

    u = jax.nn.gelu(u_ref[0, rows, :].astype(f32))
    w = jax.nn.gelu(w_ref[0, rows, :].astype(f32))
    wc = w - jnp.mean(w, axis=-1, keepdims=True)
    wn = (wc * lax.rsqrt(jnp.mean(wc * wc, axis=-1, keepdims=True) + EPS)).astype(bf16)
    gw = wn.shape[-1] // SGU_GROUPS
    for i in range(SGU_GROUPS):
        sl = slice(gw * i, gw * (i + 1))
        mixed = jnp.dot(sw_ref[i], wn[:, sl], preferred_element_type=f32) + sbias_ref[:, sl]
        sgu_ref[0, rows, sl] = (u[:, sl] * mixed).astype(sgu_ref.dtype)


def _ret_sgu(z, lay, cos, sin, sf, sb, tabs, sgu_w, sgu_bias):
    B, T, _ = z.shape
    cpb = _chunks_per_step(T)
    rows = cpb * CHUNK
    nc = T // rows
    bw = lay['ret_v'][1]

    def zs(name):
        s, w = lay[name]
        return pl.BlockSpec((1, rows, w), lambda b, n, j=s // w: (b, n, j))

    st_shape = (RET_HEADS // 2, LANES, 2 * LANES)
    const3 = lambda b, n: (0, 0, 0)
    out_spec = pl.BlockSpec((1, rows, bw), lambda b, n: (b, n, 0))
    return pl.pallas_call(
        functools.partial(_ret_sgu_body, cpb),
        grid=(B, nc),
        in_specs=[
            zs('ret_q'), zs('ret_k'), zs('ret_v'), zs('ret_g'), zs('sgu_u'), zs('sgu_v'),
            pl.BlockSpec((rows, LANES), lambda b, n: (n, 0)),
            pl.BlockSpec((rows, LANES), lambda b, n: (n, 0)),
            pl.BlockSpec((1, cpb) + st_shape, lambda b, n: (b, n, 0, 0, 0)),
            pl.BlockSpec((1, cpb) + st_shape, lambda b, n: (b, n, 0, 0, 0)),
            pl.BlockSpec((RET_HEADS, CHUNK, CHUNK), const3),
            pl.BlockSpec((RET_HEADS // 2, CHUNK, LANES), const3),
            pl.BlockSpec((RET_HEADS // 2, CHUNK, LANES), const3),
            pl.BlockSpec((SGU_GROUPS, CHUNK, CHUNK), const3),
            pl.BlockSpec((CHUNK, bw), lambda b, n: (0, 0)),
        ],
        out_specs=[out_spec, out_spec],
        out_shape=[jax.ShapeDtypeStruct((B, T, bw), bf16), jax.ShapeDtypeStruct((B, T, bw), bf16)],
        compiler_params=_params("parallel", "parallel"),
        name="ret_sgu",
    )(z, z, z, z, z, z, cos, sin, sf, sb, tabs['dm'], tabs['qdf'], tabs['qdb'], sgu_w, sgu_bias)


def _att_body(local, qb, sink_ref, q_ref, *refs):
    if local:
        (kp_ref, kc_ref, kn_ref, vp_ref, vc_ref, vn_ref, qa_ref, qb_ref,
         kap_ref, kbp_ref, kac_ref, kbc_ref, kan_ref, kbn_ref, ck_ref, cv_ref, gk_ref, ones_ref, o_ref) = refs
    else:
        qa_ref, ck_ref, cv_ref, gk_ref, ones_ref, o_ref = refs
    n = pl.program_id(1)
    n_blocks = pl.num_programs(1) * qb
    ones = ones_ref[...]
    inv_hd = 1.0 / HEAD_DIM

    def norm_rope(x_ref_val, ta, tb):
        x = x_ref_val.astype(f32)
        r = lax.rsqrt(_head_sumsq(x, ones) * inv_hd + EPS)
        y = x * ta if tb is None else x * ta + _swap_halves(x) * tb
        return y * r

    def dup(x):
        lo = _lane(x.shape) < HEAD_DIM
        xr = pltpu.roll(x, HEAD_DIM, 1)
        return jnp.where(lo, x, xr).astype(bf16), jnp.where(lo, xr, x).astype(bf16)

    def blk(i):
        return slice(CHUNK * i, CHUNK * (i + 1))

    k_ctx = dup(norm_rope(ck_ref[0], gk_ref[...], None))
    v_ctx = dup(cv_ref[0].astype(f32))
    if local:
        k_loc = ([dup(norm_rope(kp_ref[0], kap_ref[...], kbp_ref[...]))]
                 + [dup(norm_rope(kc_ref[0, blk(i), :], kac_ref[blk(i), :], kbc_ref[blk(i), :])) for i in range(qb)]
                 + [dup(norm_rope(kn_ref[0], kan_ref[...], kbn_ref[...]))])
        v_loc = ([dup(vp_ref[0].astype(f32))] + [dup(vc_ref[0, blk(i), :].astype(f32)) for i in range(qb)]
                 + [dup(vn_ref[0].astype(f32))])
    n_loc = 3 * CHUNK if local else 0

    n_groups = q_ref.shape[-1] // LANES
    groups_per_kv = n_groups // ATT_KV_HEADS
    rows = 2 * groups_per_kv * CHUNK
    log2e = 1.4426950408889634
    first_half = _lane((CHUNK, LANES)) < HEAD_DIM
    for qi in range(qb):
        if local:
            i = lax.broadcasted_iota(jnp.int32, (rows, n_loc), 0) & (CHUNK - 1)
            j = lax.broadcasted_iota(jnp.int32, (rows, n_loc), 1)
            has_prev = (n * qb + qi > 0).astype(jnp.int32)
            has_next = (n * qb + qi < n_blocks - 1).astype(jnp.int32)
            valid = (j >= i * has_prev + CHUNK * (1 - has_prev)) & (j <= 2 * CHUNK - 1 + has_next * (i + 1))
        for h in range(ATT_KV_HEADS):
            keys = ([k_loc[qi + d][h] for d in range(3)] if local else []) + [k_ctx[h]]
            vals = ([v_loc[qi + d][h] for d in range(3)] if local else []) + [v_ctx[h]]
            keys = jnp.concatenate(keys, axis=0) if local else keys[0]
            vals = jnp.concatenate(vals, axis=0) if local else vals[0]
            q_rows, sink_rows = [], []
            for c in range(groups_per_kv * h, groups_per_kv * (h + 1)):
                xq = q_ref[0, blk(qi), LANES * c:LANES * (c + 1)]
                qn = norm_rope(xq, qa_ref[blk(qi), :] if local else qa_ref[...],
                               qb_ref[blk(qi), :] if local else None) * (HEAD_DIM ** -0.5 * log2e)
                for a in range(2):
                    q_rows.append(jnp.where(first_half if a == 0 else ~first_half, qn, 0.0).astype(bf16))
                    sink_rows.append(jnp.full((CHUNK, 1), sink_ref[2 * c + a] * log2e, f32))
            s = _dot_nt(jnp.concatenate(q_rows, axis=0), keys)
            sk = jnp.concatenate(sink_rows, axis=0)
            s_ctx = s[:, n_loc:]
            m = jnp.maximum(jnp.max(s_ctx, axis=-1, keepdims=True), sk)
            if local:
                s_loc = jnp.where(valid, s[:, :n_loc], NEG_INF)
                m = jnp.maximum(m, jnp.max(s_loc, axis=-1, keepdims=True))
            p_ctx = jnp.exp2(s_ctx - m)
            d = jnp.sum(p_ctx, axis=-1, keepdims=True) + jnp.exp2(sk - m)
            o = jnp.dot(p_ctx.astype(bf16), vals[n_loc:], preferred_element_type=f32)
            if local:
                p_loc = jnp.exp2(s_loc - m)
                d = d + jnp.sum(p_loc, axis=-1, keepdims=True)
                o = o + jnp.dot(p_loc.astype(bf16), vals[:n_loc], preferred_element_type=f32)
            o = o * (1.0 / d)
            for ci in range(groups_per_kv):
                c = groups_per_kv * h + ci
                o0 = o[2 * CHUNK * ci:2 * CHUNK * ci + CHUNK]
                o1 = o[2 * CHUNK * ci + CHUNK:2 * CHUNK * (ci + 1)]
                o_ref[0, blk(qi), LANES * c:LANES * (c + 1)] = jnp.where(first_half, o0, o1).astype(o_ref.dtype)


def _attention(z, zc, lay, att_tabs, sink, local):
    B, T, _ = z.shape
    L = zc.shape[1]
    nb = T // CHUNK
    qb = _chunks_per_step(T)
    rows = qb * CHUNK
    qs, qw = lay['att_q']
    ks, kw = lay['att_k']
    vs, _ = lay['att_v']
    kj, vj = ks // kw, vs // kw
    prev = lambda n: jnp.maximum(n * qb - 1, 0)
    nxt = lambda n: jnp.minimum(n * qb + qb, nb - 1)
    one = lambda f: pl.BlockSpec((CHUNK, LANES), lambda b, n: (f(n), 0))
    cur = pl.BlockSpec((rows, LANES), lambda b, n: (n, 0))
    in_specs = [pl.BlockSpec(memory_space=pltpu.SMEM),
                pl.BlockSpec((1, rows, qw), lambda b, n: (b, n, qs // qw))]
    args = [sink, z]
    if local:
        for j in (kj, vj):
            in_specs += [pl.BlockSpec((1, CHUNK, kw), lambda b, n, j=j: (b, prev(n), j)),
                         pl.BlockSpec((1, rows, kw), lambda b, n, j=j: (b, n, j)),
                         pl.BlockSpec((1, CHUNK, kw), lambda b, n, j=j: (b, nxt(n), j))]
        args += [z] * 6
        in_specs += [cur, cur, one(prev), one(prev), cur, cur, one(nxt), one(nxt)]
        args += [att_tabs['qa'], att_tabs['qb'], att_tabs['ka'], att_tabs['kb'], att_tabs['ka'], att_tabs['kb'],
                 att_tabs['ka'], att_tabs['kb']]
    else:
        in_specs += [pl.BlockSpec((1, LANES), lambda b, n: (0, 0))]
        args += [att_tabs['gq']]
    in_specs += [pl.BlockSpec((1, L, kw), lambda b, n: (b, 0, kj)),
                 pl.BlockSpec((1, L, kw), lambda b, n: (b, 0, vj)),
                 pl.BlockSpec((1, LANES), lambda b, n: (0, 0)),
                 pl.BlockSpec((LANES, LANES), lambda b, n: (0, 0))]
    args += [zc, zc, att_tabs['gk'], att_tabs['ones']]
    return pl.pallas_call(
        functools.partial(_att_body, local, qb),
        grid=(B, nb // qb),
        in_specs=in_specs,
        out_specs=pl.BlockSpec((1, rows, qw), lambda b, n: (b, n, 0)),
        out_shape=jax.ShapeDtypeStruct((B, T, qw), bf16),
        compiler_params=_params("parallel", "parallel"),
        name="attention" if local else "ctx_attention",
    )(*args)


def _merge_body(n_gate_blocks, ret_ref, sgu_ref, att_ref, *refs):
    g_refs = refs[:N_BRANCHES * n_gate_blocks]
    x_ref, wb_ref, wo_ref, gate_ref, scale_ref, shift_ref, wr_ref, xo_ref, h2_ref, aff_ref = refs[len(g_refs):]
    merged = None
    for i, br_ref in enumerate((ret_ref, sgu_ref, att_ref)):
        p = jnp.dot(br_ref[0], wb_ref[i], preferred_element_type=f32)
        g = jnp.concatenate([r[0] for r in g_refs[n_gate_blocks * i:n_gate_blocks * (i + 1)]], axis=1)
        t = jax.nn.sigmoid(g.astype(f32)) * p
        merged = t if merged is None else merged + t
    o = jnp.dot(merged.astype(bf16), wo_ref[...], preferred_element_type=f32)
    x = x_ref[0] + gate_ref[0] * o
    xo_ref[0] = x
    y = x * lax.rsqrt(jnp.mean(x * x, axis=-1, keepdims=True) + EPS)
    h2 = y * scale_ref[0] + shift_ref[0]
    h2_ref[0, :, 0, :] = h2
    logits = jnp.dot(h2.astype(bf16), wr_ref[...], preferred_element_type=f32)
    logits = jnp.where(_lane(logits.shape) < N_EXPERTS, logits, NEG_INF)
    m = jnp.max(logits, axis=-1, keepdims=True)
    e = jnp.exp(logits - m)
    aff_ref[0] = e / jnp.sum(e, axis=-1, keepdims=True)


def _merge(ret, sgu, att, z, gates_at, x, wb, wo, gate, scale, shift, wr):
    B, T, D = x.shape
    bw = ret.shape[-1]
    tm = _pick_tile(T, 512)
    tok = lambda b, i: (b, i, 0)
    vec = lambda b, i: (b, 0, 0)
    gw = 2 * LANES
    assert gates_at % gw == 0 and D % gw == 0
    n_gate_blocks = D // gw
    gate_specs = [pl.BlockSpec((1, tm, gw), lambda b, i, j=gates_at // gw + k: (b, i, j))
                  for k in range(N_BRANCHES * n_gate_blocks)]
    return pl.pallas_call(
        functools.partial(_merge_body, n_gate_blocks),
        grid=(B, T // tm),
        in_specs=[
            pl.BlockSpec((1, tm, bw), tok), pl.BlockSpec((1, tm, bw), tok), pl.BlockSpec((1, tm, bw), tok),
            *gate_specs,
            pl.BlockSpec((1, tm, D), tok),
            pl.BlockSpec((N_BRANCHES, bw, D), lambda b, i: (0, 0, 0)),
            pl.BlockSpec((D, D), lambda b, i: (0, 0)),
            pl.BlockSpec((1, 1, D), vec), pl.BlockSpec((1, 1, D), vec), pl.BlockSpec((1, 1, D), vec),
            pl.BlockSpec((D, LANES), lambda b, i: (0, 0)),
        ],
        out_specs=[
            pl.BlockSpec((1, tm, D), tok), pl.BlockSpec((1, tm, 1, D), lambda b, i: (b, i, 0, 0)),
            pl.BlockSpec((1, tm, LANES), tok),
        ],
        out_shape=[
            jax.ShapeDtypeStruct((B, T, D), f32),
            jax.ShapeDtypeStruct((B, T, 1, D), f32),
            jax.ShapeDtypeStruct((B, T, LANES), f32),
        ],
        compiler_params=_params("parallel", "parallel"),
        name="merge",
    )(ret, sgu, att, *([z] * len(gate_specs)), x, wb, wo, gate, scale, shift, wr)


ROUTE_TILE = 256
AFF_BITS = 31
MIN_NORMAL_BITS = 0x00800000


def _route_body(cap, aff_ref, tri_ref, pos_ref, post_ref, c0_ref, afft):
    T = aff_ref.shape[1]
    tt = tri_ref.shape[0]
    for j in range(T // tt):
        afft[:, tt * j:tt * (j + 1)] = aff_ref[0, tt * j:tt * (j + 1), :].T[:N_EXPERTS]

    def as_float(bits):
        return pltpu.bitcast(jnp.broadcast_to(bits, (N_EXPERTS, LANES)), f32)[:, :1]

    def count_at_least(v):
        return jnp.sum(jnp.where(afft[...] >= v, 1.0, 0.0), axis=1, keepdims=True)

    def refine(i, thr):
        cand = thr | jnp.left_shift(jnp.int32(1), AFF_BITS - 1 - i)
        return jnp.where(count_at_least(as_float(cand)) >= cap, cand, thr)

    thr = lax.fori_loop(0, AFF_BITS, refine, jnp.zeros((N_EXPERTS, 1), jnp.int32))
    hi_col = as_float(jnp.maximum(thr + 1, MIN_NORMAL_BITS))
    diag = (lax.broadcasted_iota(jnp.int32, (N_EXPERTS, LANES), 0)
            == lax.broadcasted_iota(jnp.int32, (N_EXPERTS, LANES), 1))

    def along_lanes(col):
        return jnp.sum(jnp.where(diag, col, 0.0), axis=0, keepdims=True)

    thr_f = along_lanes(as_float(thr))
    hi_f = along_lanes(hi_col)
    need = along_lanes(cap - count_at_least(hi_col))
    tri = tri_ref[...]
    carry_eq = jnp.zeros((1, LANES), f32)
    carry_sel = jnp.zeros((1, LANES), f32)
    for j in range(T // tt):
        a = aff_ref[0, tt * j:tt * (j + 1), :]
        eq = (a >= thr_f) & (a < hi_f)
        eqf = jnp.where(eq, 1.0, 0.0)
        eq_rank = carry_eq + jnp.dot(tri, eqf.astype(bf16), preferred_element_type=f32) - eqf
        sel = (a >= hi_f) | (eq & (eq_rank < need))
        self_ = jnp.where(sel, 1.0, 0.0)
        pos = carry_sel + jnp.dot(tri, self_.astype(bf16), preferred_element_type=f32) - self_
        pos = jnp.where(sel, pos, -1.0).astype(jnp.int32)
        pos_ref[0, tt * j:tt * (j + 1), :] = pos
        post_ref[0, j] = pos.T[:N_EXPERTS]
        c0_ref[0, j:j + 1, :] = carry_sel.astype(jnp.int32)
        carry_eq = carry_eq + jnp.sum(eqf, axis=0, keepdims=True)
        carry_sel = carry_sel + jnp.sum(self_, axis=0, keepdims=True)


def _route(aff, cap):
    B, T, _ = aff.shape
    tt = min(ROUTE_TILE, T)
    nt = T // tt
    tri = jnp.tril(jnp.ones((tt, tt), bf16))
    return pl.pallas_call(
        functools.partial(_route_body, cap),
        grid=(B,),
        in_specs=[pl.BlockSpec((1, T, LANES), lambda b: (b, 0, 0)), pl.BlockSpec((tt, tt), lambda b: (0, 0))],
        out_specs=[
            pl.BlockSpec((1, T, LANES), lambda b: (b, 0, 0)),
            pl.BlockSpec((1, nt, N_EXPERTS, tt), lambda b: (b, 0, 0, 0)),
            pl.BlockSpec((1, nt, LANES), lambda b: (b, 0, 0)),
        ],
        out_shape=[
            jax.ShapeDtypeStruct((B, T, LANES), jnp.int32),
            jax.ShapeDtypeStruct((B, nt, N_EXPERTS, tt), jnp.int32),
            jax.ShapeDtypeStruct((B, nt, LANES), jnp.int32),
        ],
        scratch_shapes=[pltpu.VMEM((N_EXPERTS, T), f32)],
        compiler_params=_params("parallel"),
        name="route",
    )(aff, tri)


def _slots_body(nt, cap, c0_ref, post_ref, idx_ref):
    b = pl.program_id(0)
    tt = post_ref.shape[-1]
    tok = lax.broadcasted_iota(jnp.int32, (LANES, tt), 1)
    diag = lax.broadcasted_iota(jnp.int32, (LANES, LANES), 0) == lax.broadcasted_iota(jnp.int32, (LANES, LANES), 1)

    def per_expert(e, carry):
        def tiles_starting_at_or_before(slot):
            return sum((c0_ref[(b * nt + j) * N_EXPERTS + e] <= slot).astype(jnp.int32) for j in range(nt))

        for k in range(idx_ref.shape[2]):
            slot = LANES * k + lax.broadcasted_iota(jnp.int32, (LANES, 1), 0)

            def match(j, acc):
                return acc + jnp.where(post_ref[0, j, pl.ds(e, 1), :] == slot, tok + j * tt, 0)

            j_lo = tiles_starting_at_or_before(LANES * k) - 1
            j_hi = tiles_starting_at_or_before(min(LANES * (k + 1), cap) - 1)
            acc = lax.fori_loop(j_lo, j_hi, match, jnp.zeros((LANES, tt), jnp.int32))
            col = jnp.sum(acc.astype(f32), axis=1, keepdims=True)
            row = jnp.sum(jnp.where(diag, col, 0.0), axis=0, keepdims=True)
            idx_ref[0, e, k:k + 1, :] = row.astype(jnp.int32)
        return carry

    lax.fori_loop(0, N_EXPERTS, per_expert, 0)


def _slots(c0s, post, cap):
    B, nt, E, tt = post.shape
    nblk = -(-cap // LANES)
    idx = pl.pallas_call(
        functools.partial(_slots_body, nt, cap),
        grid_spec=pltpu.PrefetchScalarGridSpec(
            num_scalar_prefetch=1,
            grid=(B,),
            in_specs=[pl.BlockSpec((1, nt, E, tt), lambda b, c0: (b, 0, 0, 0))],
            out_specs=pl.BlockSpec((1, E, nblk, LANES), lambda b, c0: (b, 0, 0, 0)),
        ),
        out_shape=jax.ShapeDtypeStruct((B, E, nblk, LANES), jnp.int32),
        compiler_params=_params("arbitrary"),
        name="slots",
    )(c0s, post)
    return idx.reshape(B, E, nblk * LANES)[:, :, :cap]


def _expert_body(cap, idx_ref, h_ref, wg_ref, wu_ref, wd_ref, y_ref, rows, wg_s, wu_s, wd_s, sem):
    n_b = pl.num_programs(1)
    step = pl.program_id(0) * n_b + pl.program_id(1)
    last = pl.num_programs(0) * n_b - 1
    T = h_ref.shape[0] // n_b

    def start_rows(s, buf):
        e, b = s // n_b, s % n_b
        for r in range(cap):
            t = idx_ref[(b * N_EXPERTS + e) * cap + r]
            pltpu.make_async_copy(h_ref.at[b * T + t], rows.at[buf, pl.ds(r, 1), :], sem.at[buf]).start()

    def wait_rows(buf):
        pltpu.make_async_copy(h_ref.at[pl.ds(0, cap), 0, :], rows.at[buf], sem.at[buf]).wait()

    @pl.when(step == 0)
    def _():
        start_rows(step, 0)

    @pl.when(pl.program_id(1) == 0)
    def _():
        wg_s[...] = wg_ref[0, 0].astype(bf16)
        wu_s[...] = wu_ref[0, 0].astype(bf16)
        wd_s[...] = wd_ref[0, 0].astype(bf16)

    for cur in range(2):
        @pl.when(step % 2 == cur)
        def _():
            wait_rows(cur)
            xs = rows[cur].astype(bf16)

            @pl.when(step < last)
            def _():
                start_rows(step + 1, 1 - cur)

            a = jnp.dot(xs, wg_s[...], preferred_element_type=f32)
            u = jnp.dot(xs, wu_s[...], preferred_element_type=f32)
            hmid = (jax.nn.silu(a) * u).astype(bf16)
            y_ref[0, 0] = jnp.dot(hmid, wd_s[...], preferred_element_type=f32).astype(y_ref.dtype)


def _expert_ffn(idx, h2, layer, wg, wu, wd):
    B, T, _, D = h2.shape
    _, E, _, F = wg.shape
    cap = idx.shape[-1]
    wsp = lambda e, b, idx: (layer, e, 0, 0)
    return pl.pallas_call(
        functools.partial(_expert_body, cap),
        grid_spec=pltpu.PrefetchScalarGridSpec(
            num_scalar_prefetch=1,
            grid=(E, B),
            in_specs=[
                pl.BlockSpec(memory_space=pl.ANY),
                pl.BlockSpec((1, 1, D, F), wsp), pl.BlockSpec((1, 1, D, F), wsp), pl.BlockSpec((1, 1, F, D), wsp),
            ],
            out_specs=pl.BlockSpec((1, 1, cap, D), lambda e, b, idx: (b, e, 0, 0)),
            scratch_shapes=[pltpu.VMEM((2, cap, D), f32), pltpu.VMEM((D, F), bf16), pltpu.VMEM((D, F), bf16),
                            pltpu.VMEM((F, D), bf16), pltpu.SemaphoreType.DMA((2,))],
        ),
        out_shape=jax.ShapeDtypeStruct((B, E, cap, D), bf16),
        compiler_params=pltpu.CompilerParams(dimension_semantics=("arbitrary", "arbitrary"),
                                             vmem_limit_bytes=EXPERT_VMEM_LIMIT_BYTES),
        name="expert_ffn",
    )(idx.reshape(-1), h2.reshape(B * T, 1, D), wg, wu, wd)


Y_ROW_ALIGN = 16


def _combine_body(nt, cap, win, c0_ref, pos_ref, aff_ref, x_ref, gate_ref, y_ref, o_ref, buf, extra, sem, sem_x):
    b = pl.program_id(0)
    j = pl.program_id(1)
    step = b * nt + j
    tt = pos_ref.shape[1]
    stride = win if win == cap else win - Y_ROW_ALIGN

    def c0(bb, jj, e):
        return c0_ref[(bb * nt + jj) * N_EXPERTS + e]

    def window_start(lo):
        return pl.multiple_of(jnp.minimum(lo & (-Y_ROW_ALIGN), cap - win), Y_ROW_ALIGN)

    def copies(bb, jj, p, dst, s):
        return [pltpu.make_async_copy(
            y_ref.at[bb, e, pl.ds(window_start(c0(bb, jj, e) + p * stride), win), :],
            dst.at[pl.ds(e * win, win), :], s) for e in range(N_EXPERTS)]

    @pl.when(step == 0)
    def _():
        for cp in copies(b, j, 0, buf.at[0], sem.at[0]):
            cp.start()

    nxt = step + 1

    @pl.when(nxt < pl.num_programs(0) * nt)
    def _():
        for cp in copies(nxt // nt, nxt % nt, 0, buf.at[nxt % 2], sem.at[nxt % 2]):
            cp.start()

    lane = _lane((1, LANES))
    pos = pos_ref[0]
    aff = aff_ref[0]
    per_group = LANES // win

    def lane_vector(fn):
        v = jnp.zeros((1, LANES), jnp.int32)
        for e in range(N_EXPERTS):
            v = jnp.where(lane == e, fn(e), v)
        return v

    def weights(p):
        lo = lane_vector(lambda e: c0(b, j, e) + p * stride)
        start = lane_vector(lambda e: window_start(c0(b, j, e) + p * stride))
        rel = jnp.where((pos >= lo) & (pos < lo + stride), pos - start, -LANES)
        lanes_full = _lane((tt, LANES))
        groups = []
        for g in range(N_EXPERTS // per_group):
            q = jnp.zeros((tt, LANES), f32)
            for i in range(per_group):
                e = g * per_group + i
                q = jnp.where(rel[:, e:e + 1] + i * win == lanes_full, aff[:, e:e + 1], q)
            groups.append(q.astype(bf16))
        return jnp.concatenate(groups, axis=1)

    for cp in copies(b, j, 0, buf.at[step % 2], sem.at[step % 2]):
        cp.wait()
    moe = jnp.dot(weights(0), buf[step % 2], preferred_element_type=f32)

    def count(e):
        end = jnp.where(j + 1 < nt, c0(b, jnp.minimum(j + 1, nt - 1), e), cap)
        return end - c0(b, j, e)

    max_count = functools.reduce(jnp.maximum, [count(e) for e in range(N_EXPERTS)])

    def more(p, acc):
        cps = copies(b, j, p, extra, sem_x.at[0])
        for cp in cps:
            cp.start()
        for cp in cps:
            cp.wait()
        return acc + jnp.dot(weights(p), extra[...], preferred_element_type=f32)

    moe = lax.fori_loop(1, (max_count + stride - 1) // stride, more, moe)
    o_ref[0] = x_ref[0] + gate_ref[0] * moe


def _combine(c0s, pos, aff, x, gate, y):
    B, T, D = x.shape
    E, cap = y.shape[1], y.shape[2]
    tt = min(ROUTE_TILE, T)
    nt = T // tt
    win = min(cap, LANES // 2)
    assert (cap - win) % Y_ROW_ALIGN == 0 and LANES % win == 0
    tok = lambda b, j, c0: (b, j, 0)
    return pl.pallas_call(
        functools.partial(_combine_body, nt, cap, win),
        grid_spec=pltpu.PrefetchScalarGridSpec(
            num_scalar_prefetch=1,
            grid=(B, nt),
            in_specs=[
                pl.BlockSpec((1, tt, LANES), tok), pl.BlockSpec((1, tt, LANES), tok), pl.BlockSpec((1, tt, D), tok),
                pl.BlockSpec((1, 1, D), lambda b, j, c0: (b, 0, 0)),
                pl.BlockSpec(memory_space=pl.ANY),
            ],
            out_specs=pl.BlockSpec((1, tt, D), tok),
            scratch_shapes=[pltpu.VMEM((2, E * win, D), bf16), pltpu.VMEM((E * win, D), bf16),
                            pltpu.SemaphoreType.DMA((2,)), pltpu.SemaphoreType.DMA((1,))],
        ),
        out_shape=jax.ShapeDtypeStruct((B, T, D), f32),
        compiler_params=_params("arbitrary", "arbitrary"),
        name="combine",
    )(c0s, pos, aff, x, gate, y)


def _expert_choice(h2, aff, x, gate2, p):
    T = x.shape[1]
    cap = CAPACITY_FACTOR * T // N_EXPERTS
    pos, post, c0 = _route(aff, cap)
    c0s = c0[:, :, :N_EXPERTS].reshape(-1)
    y = _expert_ffn(_slots(c0s, post, cap), h2, p['layer'], p['w_exp_gate'], p['w_exp_up'], p['w_exp_down'])
    return _combine(c0s, pos, aff, x, gate2, y)


def _rope_tables(T):
    rows = T // GRID_W
    row = jnp.repeat(jnp.arange(rows), GRID_W).astype(f32)
    col = jnp.tile(jnp.arange(GRID_W), rows).astype(f32)
    n_freq = HEAD_DIM // 4
    inv = jnp.power(ROPE_BASE, -jnp.arange(n_freq, dtype=f32) / n_freq)
    ang = jnp.concatenate([row[:, None] * inv, col[:, None] * inv], axis=-1)
    cos, sin = jnp.cos(ang), jnp.sin(ang)
    reps = LANES // HEAD_DIM
    return jnp.tile(jnp.concatenate([cos, cos], axis=-1), (1, reps)), jnp.tile(
        jnp.concatenate([-sin, sin], axis=-1), (1, reps))


def _identity_rope(T):
    return jnp.ones((T, LANES), f32), jnp.zeros((T, LANES), f32)


def _retention_tables(decay_fwd, decay_bwd):
    lgf = jax.nn.log_sigmoid(decay_fwd.astype(f32))
    lgb = jax.nn.log_sigmoid(decay_bwd.astype(f32))
    C = CHUNK
    k_scale = HEAD_DIM ** -0.5
    idx = jnp.arange(C, dtype=f32)
    npair = RET_HEADS // 2

    def lanes(t):
        return jnp.repeat(t.reshape(npair, 2, C).transpose(0, 2, 1), HEAD_DIM, axis=-1)

    diff = idx[:, None] - idx[None, :]
    dm = jnp.where(diff >= 0, jnp.exp(jnp.maximum(diff, 0.0)[None] * lgf[:, None, None]),
                   jnp.exp(jnp.maximum(-diff, 0.0)[None] * lgb[:, None, None])) * k_scale

    def chunk_decay(lg):
        rows = jnp.repeat(jnp.exp(C * lg).reshape(npair, 2), HEAD_DIM, axis=-1)
        return jnp.broadcast_to(rows[:, :, None], (npair, LANES, 2 * LANES))

    r = jnp.arange(LANES)[:, None] // HEAD_DIM
    c = jnp.arange(2 * LANES)[None, :] // LANES
    return {
        'wkf': lanes(jnp.exp((C - 1 - idx)[None, :] * lgf[:, None])) * k_scale,
        'wkb': lanes(jnp.exp(idx[None, :] * lgb[:, None])) * k_scale,
        'qdf': lanes(jnp.exp((idx + 1.0)[None, :] * lgf[:, None])),
        'qdb': lanes(jnp.exp((C - idx)[None, :] * lgb[:, None])),
        'dm': dm, 'cdf': chunk_decay(lgf), 'cdb': chunk_decay(lgb), 'bm': (r == c).astype(f32),
    }


def _attention_tables(cos, sin, q_gain, k_gain):
    reps = LANES // HEAD_DIM
    half = HEAD_DIM // 2
    swap = lambda g: jnp.concatenate([g[half:], g[:half]])
    gq, gk = jnp.tile(q_gain, reps)[None], jnp.tile(k_gain, reps)[None]
    gqs, gks = jnp.tile(swap(q_gain), reps)[None], jnp.tile(swap(k_gain), reps)[None]
    r = jnp.arange(LANES) // HEAD_DIM
    return {'qa': cos * gq, 'qb': sin * gqs, 'ka': cos * gk, 'kb': sin * gks, 'gq': gq, 'gk': gk,
            'ones': (r[:, None] == r[None, :]).astype(bf16)}


def _layer(x, xc, mod_lat, mod_ctx, rope, p, last):
    B, S, D = x.shape
    L = xc.shape[1]
    lay = _layout(D)
    sh1, sc1, g1, sh2, sc2, g2 = jnp.split(mod_lat, 6, axis=-1)
    csh1, csc1, cg1, csh2, csc2, cg2 = jnp.split(mod_ctx, 6, axis=-1)
    n1 = p['norm1_g'][None, None, :]
    n2 = p['norm2_g'][None, None, :]

    def bcast(t):
        return jnp.broadcast_to(t, (B, 1, D))

    z = _proj_in(x, n1 * (1 + sc1), sh1, p['w_in'])
    zc = _proj_in(xc, bcast(n1 * (1 + csc1)), bcast(csh1), p['w_in'])

    cos, sin = rope
    cos_c, sin_c = _identity_rope(L)
    rtabs = _retention_tables(p['ret_decay_fwd'], p['ret_decay_bwd'])
    atabs = _attention_tables(cos, sin, p['q_norm_g'], p['k_norm_g'])
    sgu_w = p['sgu_w'].astype(bf16)
    gw = D // 2 // SGU_GROUPS
    sgu_bias = jnp.repeat(p['sgu_b'].T, gw, axis=-1)

    zero = jnp.zeros((B, RET_HEADS // 2, LANES, 2 * LANES), f32)
    sfc, sbc, s_f, s_b = _ret_states(zc, lay, cos_c, sin_c, rtabs, zero, zero)
    sf, sb, _, _ = _ret_states(z, lay, cos, sin, rtabs, s_f, s_b)
    ret, sgu = _ret_sgu(z, lay, cos, sin, sf, sb, rtabs, sgu_w, sgu_bias)
    att = _attention(z, zc, lay, atabs, p['attn_sink'], True)

    x, h2, aff = _merge(ret, sgu, att, z, lay['gates'][0], x, p['w_branch'], p['w_out'], g1, n2 * (1 + sc2), sh2,
                        p['w_router'])
    x = _expert_choice(h2, aff, x, g2, p)
    if last:
        return x, None

    ret_c, sgu_c = _ret_sgu(zc, lay, cos_c, sin_c, sfc, sbc, rtabs, sgu_w, sgu_bias)
    att_c = _attention(zc, zc, lay, atabs, p['attn_sink'], False)
    xc, h2c, affc = _merge(ret_c, sgu_c, att_c, zc, lay['gates'][0], xc, p['w_branch'], p['w_out'], bcast(cg1),
                           bcast(n2 * (1 + csc2)), bcast(csh2), p['w_router'])
    xc = _expert_choice(h2c, affc, xc, bcast(cg2), p)
    return x, xc


def kernel(x, c, ctx, c_ctx, w_mod, b_mod, norm1_g, norm2_g, w_in, ret_decay_fwd, ret_decay_bwd, sgu_w, sgu_b,
           q_norm_g, k_norm_g, attn_sink, w_branch, w_out, w_router, w_exp_gate, w_exp_up, w_exp_down):
    B, S, D = x.shape
    depth = w_in.shape[0]
    rope = _rope_tables(S)
    c_act = jax.nn.silu(c)
    c_ctx_act = jax.nn.silu(c_ctx)
    xc = ctx
    hp = lax.Precision.HIGHEST
    for l in range(depth):
        wr = jnp.zeros((D, LANES), f32).at[:, :N_EXPERTS].set(w_router[l]).astype(bf16)
        p = {
            'norm1_g': norm1_g[l], 'norm2_g': norm2_g[l], 'w_in': w_in[l].astype(bf16),
            'ret_decay_fwd': ret_decay_fwd[l], 'ret_decay_bwd': ret_decay_bwd[l],
            'sgu_w': sgu_w[l], 'sgu_b': sgu_b[l],
            'q_norm_g': q_norm_g[l], 'k_norm_g': k_norm_g[l], 'attn_sink': attn_sink[l],
            'w_branch': w_branch[l].astype(bf16), 'w_out': w_out[l].astype(bf16), 'w_router': wr,
            'layer': l, 'w_exp_gate': w_exp_gate, 'w_exp_up': w_exp_up, 'w_exp_down': w_exp_down,
        }
        mod_lat = (jnp.dot(c_act, w_mod[l], precision=hp) + b_mod[l])[:, None, :]
        mod_ctx = (jnp.dot(c_ctx_act, w_mod[l], precision=hp) + b_mod[l])[None, None, :]
        x, xc = _layer(x, xc, mod_lat, mod_ctx, rope, p, l == depth - 1)
    return x
```

```python
import functools

import jax
import jax.numpy as jnp
from jax import lax
from jax.experimental import pallas as pl
from jax.experimental.pallas import tpu as pltpu

GRID_W = 64
N_BRANCHES = 3
RET_HEADS = 4
SGU_GROUPS = 4
CHUNK = 128
HEAD_DIM = 64
ATT_KV_HEADS = 2
ROPE_BASE = 10000.0
N_EXPERTS = 16
CAPACITY_FACTOR = 2
EPS = 1e-6
NEG_INF = -1e30

LANES = 128
VMEM_LIMIT_BYTES = 48 * 1024 * 1024
EXPERT_VMEM_LIMIT_BYTES = 56 * 1024 * 1024

f32 = jnp.float32
bf16 = jnp.bfloat16


def _layout(d_model):
    bw = d_model // 2
    qk = RET_HEADS * HEAD_DIM
    kv = ATT_KV_HEADS * HEAD_DIM
    widths = (('ret_q', qk), ('ret_k', qk), ('ret_v', bw), ('ret_g', bw), ('sgu_u', bw), ('sgu_v', bw),
              ('att_q', bw), ('att_k', kv), ('att_v', kv), ('gates', N_BRANCHES * d_model))
    out, start = {}, 0
    for name, w in widths:
        out[name] = (start, w)
        start += w
    return out


def _chunks_per_step(T):
    return 2 if (T // CHUNK) % 2 == 0 else 1


def _pick_tile(n, pref):
    t = min(n, pref)
    while n % t:
        t //= 2
    return t


def _params(*sem):
    return pltpu.CompilerParams(dimension_semantics=sem, vmem_limit_bytes=VMEM_LIMIT_BYTES)


def _proj_in_body(x_ref, scale_ref, shift_ref, w_ref, o_ref, h_ref):
    @pl.when(pl.program_id(2) == 0)
    def _():
        x = x_ref[0]
        y = x * lax.rsqrt(jnp.mean(x * x, axis=-1, keepdims=True) + EPS)
        h_ref[...] = (y * scale_ref[0] + shift_ref[0]).astype(bf16)

    o_ref[0] = jnp.dot(h_ref[...], w_ref[...], preferred_element_type=f32).astype(o_ref.dtype)


def _proj_in(x, scale, shift, w):
    B, T, D = x.shape
    N = w.shape[1]
    tm = _pick_tile(T, 2048)
    tn = 1280 if N % 1280 == 0 else _pick_tile(N, 1024)
    return pl.pallas_call(
        _proj_in_body,
        grid=(B, T // tm, N // tn),
        in_specs=[
            pl.BlockSpec((1, tm, D), lambda b, i, j: (b, i, 0)),
            pl.BlockSpec((1, 1, D), lambda b, i, j: (b, 0, 0)),
            pl.BlockSpec((1, 1, D), lambda b, i, j: (b, 0, 0)),
            pl.BlockSpec((D, tn), lambda b, i, j: (0, j)),
        ],
        out_specs=pl.BlockSpec((1, tm, tn), lambda b, i, j: (b, i, j)),
        out_shape=jax.ShapeDtypeStruct((B, T, N), bf16),
        scratch_shapes=[pltpu.VMEM((tm, D), bf16)],
        compiler_params=_params("parallel", "parallel", "arbitrary"),
        name="proj_in",
    )(x, scale, shift, w)


def _lane(shape):
    return lax.broadcasted_iota(jnp.int32, shape, 1)


def _swap_halves(x):
    return jnp.where((_lane(x.shape) & (HEAD_DIM // 2)) == 0,
                     pltpu.roll(x, LANES - HEAD_DIM // 2, 1), pltpu.roll(x, HEAD_DIM // 2, 1))


def _rope(x, cos, sin_signed):
    return x * cos + _swap_halves(x) * sin_signed


def _head_sumsq(x, ones_blockdiag):
    x2 = x * x
    hi = x2.astype(bf16)
    lo = (x2 - hi.astype(f32)).astype(bf16)
    return (jnp.dot(hi, ones_blockdiag, preferred_element_type=f32)
            + jnp.dot(lo, ones_blockdiag, preferred_element_type=f32))


def _dot_nt(a, b):
    return lax.dot_general(a, b, (((1,), (1,)), ((), ())), preferred_element_type=f32)


def _dot_tn(a, b):
    return lax.dot_general(a, b, (((0,), (0,)), ((), ())), preferred_element_type=f32)


def _ret_state_body(cpb, kf_ref, vf_ref, kb_ref, vb_ref, cosf_ref, sinf_ref, cosb_ref, sinb_ref,
                    wkf_ref, wkb_ref, cdf_ref, cdb_ref, bm_ref, s0f_ref, s0b_ref,
                    sf_ref, sb_ref, ff_ref, fb_ref, accf, accb):
    n = pl.program_id(1)

    @pl.when(n == 0)
    def _():
        accf[...] = s0f_ref[0]
        accb[...] = s0b_ref[0]

    def one_chunk(c, k_ref, v_ref, cos_ref, sin_ref, wk_ref, cd_ref, out_ref, acc):
        rows = slice(CHUNK * c, CHUNK * (c + 1))
        k = k_ref[0, rows, :].astype(f32)
        v = v_ref[0, rows, :]
        for p in range(RET_HEADS // 2):
            kp = _rope(k[:, LANES * p:LANES * (p + 1)], cos_ref[rows, :], sin_ref[rows, :])
            kw = (kp * wk_ref[p]).astype(bf16)
            upd = _dot_tn(kw, v[:, 2 * LANES * p:2 * LANES * (p + 1)])
            s = acc[p]
            out_ref[0, c, p] = s.astype(out_ref.dtype)
            acc[p] = cd_ref[p] * s + bm_ref[...] * upd

    for c in range(cpb):
        one_chunk(c, kf_ref, vf_ref, cosf_ref, sinf_ref, wkf_ref, cdf_ref, sf_ref, accf)
        one_chunk(cpb - 1 - c, kb_ref, vb_ref, cosb_ref, sinb_ref, wkb_ref, cdb_ref, sb_ref, accb)

    @pl.when(n == pl.num_programs(1) - 1)
    def _():
        ff_ref[0] = accf[...]
        fb_ref[0] = accb[...]


def _ret_states(z, lay, cos, sin, tabs, s0f, s0b):
    B, T, _ = z.shape
    cpb = _chunks_per_step(T)
    rows = cpb * CHUNK
    nc = T // rows
    kq = lay['ret_k'][0] // (RET_HEADS * HEAD_DIM)
    vq = lay['ret_v'][0] // lay['ret_v'][1]
    vw = lay['ret_v'][1]
    kw = RET_HEADS * HEAD_DIM
    npair = RET_HEADS // 2
    st_shape = (npair, LANES, 2 * LANES)
    const3 = lambda b, n: (0, 0, 0)
    return pl.pallas_call(
        functools.partial(_ret_state_body, cpb),
        grid=(B, nc),
        in_specs=[
            pl.BlockSpec((1, rows, kw), lambda b, n: (b, n, kq)),
            pl.BlockSpec((1, rows, vw), lambda b, n: (b, n, vq)),
            pl.BlockSpec((1, rows, kw), lambda b, n: (b, nc - 1 - n, kq)),
            pl.BlockSpec((1, rows, vw), lambda b, n: (b, nc - 1 - n, vq)),
            pl.BlockSpec((rows, LANES), lambda b, n: (n, 0)),
            pl.BlockSpec((rows, LANES), lambda b, n: (n, 0)),
            pl.BlockSpec((rows, LANES), lambda b, n: (nc - 1 - n, 0)),
            pl.BlockSpec((rows, LANES), lambda b, n: (nc - 1 - n, 0)),
            pl.BlockSpec((npair, CHUNK, LANES), const3),
            pl.BlockSpec((npair, CHUNK, LANES), const3),
            pl.BlockSpec(st_shape, const3),
            pl.BlockSpec(st_shape, const3),
            pl.BlockSpec((LANES, 2 * LANES), lambda b, n: (0, 0)),
            pl.BlockSpec((1,) + st_shape, lambda b, n: (b, 0, 0, 0)),
            pl.BlockSpec((1,) + st_shape, lambda b, n: (b, 0, 0, 0)),
        ],
        out_specs=[
            pl.BlockSpec((1, cpb) + st_shape, lambda b, n: (b, n, 0, 0, 0)),
            pl.BlockSpec((1, cpb) + st_shape, lambda b, n: (b, nc - 1 - n, 0, 0, 0)),
            pl.BlockSpec((1,) + st_shape, lambda b, n: (b, 0, 0, 0)),
            pl.BlockSpec((1,) + st_shape, lambda b, n: (b, 0, 0, 0)),
        ],
        out_shape=[
            jax.ShapeDtypeStruct((B, nc * cpb) + st_shape, bf16),
            jax.ShapeDtypeStruct((B, nc * cpb) + st_shape, bf16),
            jax.ShapeDtypeStruct((B,) + st_shape, f32),
            jax.ShapeDtypeStruct((B,) + st_shape, f32),
        ],
        scratch_shapes=[pltpu.VMEM(st_shape, f32), pltpu.VMEM(st_shape, f32)],
        compiler_params=_params("parallel", "arbitrary"),
        name="ret_states",
    )(z, z, z, z, cos, sin, cos, sin, tabs['wkf'], tabs['wkb'], tabs['cdf'], tabs['cdb'], tabs['bm'], s0f, s0b)


def _ret_sgu_body(cpb, *refs):
    for c in range(cpb):
        _ret_sgu_chunk(c, *refs)


def _ret_sgu_chunk(c, q_ref, k_ref, v_ref, g_ref, u_ref, w_ref, cos_ref, sin_ref, sf_ref, sb_ref,
                   dm_ref, qdf_ref, qdb_ref, sw_ref, sbias_ref, ret_ref, sgu_ref):
    rows = slice(CHUNK * c, CHUNK * (c + 1))
    q = q_ref[0, rows, :].astype(f32)
    k = k_ref[0, rows, :].astype(f32)
    v = v_ref[0, rows, :]
    g = g_ref[0, rows, :].astype(f32)
    cos = cos_ref[rows, :]
    sin = sin_ref[rows, :]
    first_half = _lane((CHUNK, LANES)) < HEAD_DIM
    for p in range(RET_HEADS // 2):
        qp = _rope(q[:, LANES * p:LANES * (p + 1)], cos, sin)
        kp = _rope(k[:, LANES * p:LANES * (p + 1)], cos, sin).astype(bf16)
        cross = (jnp.dot((qp * qdf_ref[p]).astype(bf16), sf_ref[0, c, p], preferred_element_type=f32)
                 + jnp.dot((qp * qdb_ref[p]).astype(bf16), sb_ref[0, c, p], preferred_element_type=f32))
        for a in range(2):
            h = 2 * p + a
            qm = jnp.where(first_half if a == 0 else ~first_half, qp, 0.0).astype(bf16)
            sc = (_dot_nt(qm, kp) * dm_ref[h]).astype(bf16)
            o = jnp.dot(sc, v[:, LANES * h:LANES * (h + 1)], preferred_element_type=f32)
            o = o + cross[:, LANES * a:LANES * (a + 1)]
            oc = o - jnp.mean(o, axis=-1, keepdims=True)
            y = oc * lax.rsqrt(jnp.mean(oc * oc, axis=-1, keepdims=True) + EPS)
            ret_ref[0, rows, LANES * h:LANES * (h + 1)] = (
                y * jax.nn.silu(g[:, LANES * h:LANES * (h + 1)])).astype(ret_ref.dtype)

    u = jax.nn.gelu(u_ref[0, rows, :].astype(f32))
    w = jax.nn.gelu(w_ref[0, rows, :].astype(f32))
    wc = w - jnp.mean(w, axis=-1, keepdims=True)
    wn = (wc * lax.rsqrt(jnp.mean(wc * wc, axis=-1, keepdims=True) + EPS)).astype(bf16)
    gw = wn.shape[-1] // SGU_GROUPS
    for i in range(SGU_GROUPS):
        sl = slice(gw * i, gw * (i + 1))
        mixed = jnp.dot(sw_ref[i], wn[:, sl], preferred_element_type=f32) + sbias_ref[:, sl]
        sgu_ref[0, rows, sl] = (u[:, sl] * mixed).astype(sgu_ref.dtype)


def _ret_sgu(z, lay, cos, sin, sf, sb, tabs, sgu_w, sgu_bias):
    B, T, _ = z.shape
    cpb = _chunks_per_step(T)
    rows = cpb * CHUNK
    nc = T // rows
    bw = lay['ret_v'][1]

    def zs(name):
        s, w = lay[name]
        return pl.BlockSpec((1, rows, w), lambda b, n, j=s // w: (b, n, j))

    st_shape = (RET_HEADS // 2, LANES, 2 * LANES)
    const3 = lambda b, n: (0, 0, 0)
    out_spec = pl.BlockSpec((1, rows, bw), lambda b, n: (b, n, 0))
    return pl.pallas_call(
        functools.partial(_ret_sgu_body, cpb),
        grid=(B, nc),
        in_specs=[
            zs('ret_q'), zs('ret_k'), zs('ret_v'), zs('ret_g'), zs('sgu_u'), zs('sgu_v'),
            pl.BlockSpec((rows, LANES), lambda b, n: (n, 0)),
            pl.BlockSpec((rows, LANES), lambda b, n: (n, 0)),
            pl.BlockSpec((1, cpb) + st_shape, lambda b, n: (b, n, 0, 0, 0)),
            pl.BlockSpec((1, cpb) + st_shape, lambda b, n: (b, n, 0, 0, 0)),
            pl.BlockSpec((RET_HEADS, CHUNK, CHUNK), const3),
            pl.BlockSpec((RET_HEADS // 2, CHUNK, LANES), const3),
            pl.BlockSpec((RET_HEADS // 2, CHUNK, LANES), const3),
            pl.BlockSpec((SGU_GROUPS, CHUNK, CHUNK), const3),
            pl.BlockSpec((CHUNK, bw), lambda b, n: (0, 0)),
        ],
        out_specs=[out_spec, out_spec],
        out_shape=[jax.ShapeDtypeStruct((B, T, bw), bf16), jax.ShapeDtypeStruct((B, T, bw), bf16)],
        compiler_params=_params("parallel", "parallel"),
        name="ret_sgu",
    )(z, z, z, z, z, z, cos, sin, sf, sb, tabs['dm'], tabs['qdf'], tabs['qdb'], sgu_w, sgu_bias)


def _att_body(local, qb, sink_ref, q_ref, *refs):
    if local:
        (kp_ref, kc_ref, kn_ref, vp_ref, vc_ref, vn_ref, qa_ref, qb_ref,
         kap_ref, kbp_ref, kac_ref, kbc_ref, kan_ref, kbn_ref, ck_ref, cv_ref, gk_ref, ones_ref, o_ref) = refs
    else:
        qa_ref, ck_ref, cv_ref, gk_ref, ones_ref, o_ref = refs
    n = pl.program_id(1)
    n_blocks = pl.num_programs(1) * qb
    ones = ones_ref[...]
    inv_hd = 1.0 / HEAD_DIM

    def norm_rope(x_ref_val, ta, tb):
        x = x_ref_val.astype(f32)
        r = lax.rsqrt(_head_sumsq(x, ones) * inv_hd + EPS)
        y = x * ta if tb is None else x * ta + _swap_halves(x) * tb
        return y * r

    def dup(x):
        lo = _lane(x.shape) < HEAD_DIM
        xr = pltpu.roll(x, HEAD_DIM, 1)
        return jnp.where(lo, x, xr).astype(bf16), jnp.where(lo, xr, x).astype(bf16)

    def blk(i):
        return slice(CHUNK * i, CHUNK * (i + 1))

    k_ctx = dup(norm_rope(ck_ref[0], gk_ref[...], None))
    v_ctx = dup(cv_ref[0].astype(f32))
    if local:
        k_loc = ([dup(norm_rope(kp_ref[0], kap_ref[...], kbp_ref[...]))]
                 + [dup(norm_rope(kc_ref[0, blk(i), :], kac_ref[blk(i), :], kbc_ref[blk(i), :])) for i in range(qb)]
                 + [dup(norm_rope(kn_ref[0], kan_ref[...], kbn_ref[...]))])
        v_loc = ([dup(vp_ref[0].astype(f32))] + [dup(vc_ref[0, blk(i), :].astype(f32)) for i in range(qb)]
                 + [dup(vn_ref[0].astype(f32))])
    n_loc = 3 * CHUNK if local else 0

    n_groups = q_ref.shape[-1] // LANES
    groups_per_kv = n_groups // ATT_KV_HEADS
    rows = 2 * groups_per_kv * CHUNK
    log2e = 1.4426950408889634
    first_half = _lane((CHUNK, LANES)) < HEAD_DIM
    for qi in range(qb):
        if local:
            i = lax.broadcasted_iota(jnp.int32, (CHUNK, n_loc), 0)
            j = lax.broadcasted_iota(jnp.int32, (CHUNK, n_loc), 1)
            has_prev = (n * qb + qi > 0).astype(jnp.int32)
            has_next = (n * qb + qi < n_blocks - 1).astype(jnp.int32)
            valid = (j >= i * has_prev + CHUNK * (1 - has_prev)) & (j <= 2 * CHUNK - 1 + has_next * (i + 1))
            bias = jnp.where(valid, 0.0, NEG_INF)
            bias = jnp.concatenate([bias] * (rows // CHUNK), axis=0)
        for h in range(ATT_KV_HEADS):
            keys = ([k_loc[qi + d][h] for d in range(3)] if local else []) + [k_ctx[h]]
            vals = ([v_loc[qi + d][h] for d in range(3)] if local else []) + [v_ctx[h]]
            keys = jnp.concatenate(keys, axis=0) if local else keys[0]
            vals = jnp.concatenate(vals, axis=0) if local else vals[0]
            q_rows, sink_rows = [], []
            for c in range(groups_per_kv * h, groups_per_kv * (h + 1)):
                xq = q_ref[0, blk(qi), LANES * c:LANES * (c + 1)]
                qn = norm_rope(xq, qa_ref[blk(qi), :] if local else qa_ref[...],
                               qb_ref[blk(qi), :] if local else None) * (HEAD_DIM ** -0.5 * log2e)
                for a in range(2):
                    q_rows.append(jnp.where(first_half if a == 0 else ~first_half, qn, 0.0).astype(bf16))
                    sink_rows.append(jnp.full((CHUNK, 1), sink_ref[2 * c + a] * log2e, f32))
            s = _dot_nt(jnp.concatenate(q_rows, axis=0), keys)
            sk = jnp.concatenate(sink_rows, axis=0)
            s_ctx = s[:, n_loc:]
            m = jnp.maximum(jnp.max(s_ctx, axis=-1, keepdims=True), sk)
            if local:
                s_loc = s[:, :n_loc] + bias
                m = jnp.maximum(m, jnp.max(s_loc, axis=-1, keepdims=True))
            p_ctx = jnp.exp2(s_ctx - m)
            d = jnp.sum(p_ctx, axis=-1, keepdims=True) + jnp.exp2(sk - m)
            o = jnp.dot(p_ctx.astype(bf16), vals[n_loc:], preferred_element_type=f32)
            if local:
                p_loc = jnp.exp2(s_loc - m)
                d = d + jnp.sum(p_loc, axis=-1, keepdims=True)
                o = o + jnp.dot(p_loc.astype(bf16), vals[:n_loc], preferred_element_type=f32)
            o = o * (1.0 / d)
            for ci in range(groups_per_kv):
                c = groups_per_kv * h + ci
                o0 = o[2 * CHUNK * ci:2 * CHUNK * ci + CHUNK]
                o1 = o[2 * CHUNK * ci + CHUNK:2 * CHUNK * (ci + 1)]
                o_ref[0, blk(qi), LANES * c:LANES * (c + 1)] = jnp.where(first_half, o0, o1).astype(o_ref.dtype)


def _attention(z, zc, lay, att_tabs, sink, local):
    B, T, _ = z.shape
    L = zc.shape[1]
    nb = T // CHUNK
    qb = _chunks_per_step(T)
    rows = qb * CHUNK
    qs, qw = lay['att_q']
    ks, kw = lay['att_k']
    vs, _ = lay['att_v']
    kj, vj = ks // kw, vs // kw
    prev = lambda n: jnp.maximum(n * qb - 1, 0)
    nxt = lambda n: jnp.minimum(n * qb + qb, nb - 1)
    one = lambda f: pl.BlockSpec((CHUNK, LANES), lambda b, n: (f(n), 0))
    cur = pl.BlockSpec((rows, LANES), lambda b, n: (n, 0))
    in_specs = [pl.BlockSpec(memory_space=pltpu.SMEM),
                pl.BlockSpec((1, rows, qw), lambda b, n: (b, n, qs // qw))]
    args = [sink, z]
    if local:
        for j in (kj, vj):
            in_specs += [pl.BlockSpec((1, CHUNK, kw), lambda b, n, j=j: (b, prev(n), j)),
                         pl.BlockSpec((1, rows, kw), lambda b, n, j=j: (b, n, j)),
                         pl.BlockSpec((1, CHUNK, kw), lambda b, n, j=j: (b, nxt(n), j))]
        args += [z] * 6
        in_specs += [cur, cur, one(prev), one(prev), cur, cur, one(nxt), one(nxt)]
        args += [att_tabs['qa'], att_tabs['qb'], att_tabs['ka'], att_tabs['kb'], att_tabs['ka'], att_tabs['kb'],
                 att_tabs['ka'], att_tabs['kb']]
    else:
        in_specs += [pl.BlockSpec((1, LANES), lambda b, n: (0, 0))]
        args += [att_tabs['gq']]
    in_specs += [pl.BlockSpec((1, L, kw), lambda b, n: (b, 0, kj)),
                 pl.BlockSpec((1, L, kw), lambda b, n: (b, 0, vj)),
                 pl.BlockSpec((1, LANES), lambda b, n: (0, 0)),
                 pl.BlockSpec((LANES, LANES), lambda b, n: (0, 0))]
    args += [zc, zc, att_tabs['gk'], att_tabs['ones']]
    return pl.pallas_call(
        functools.partial(_att_body, local, qb),
        grid=(B, nb // qb),
        in_specs=in_specs,
        out_specs=pl.BlockSpec((1, rows, qw), lambda b, n: (b, n, 0)),
        out_shape=jax.ShapeDtypeStruct((B, T, qw), bf16),
        compiler_params=_params("parallel", "parallel"),
        name="attention" if local else "ctx_attention",
    )(*args)


def _merge_body(n_gate_blocks, ret_ref, sgu_ref, att_ref, *refs):
    g_refs = refs[:N_BRANCHES * n_gate_blocks]
    x_ref, wb_ref, wo_ref, gate_ref, scale_ref, shift_ref, wr_ref, xo_ref, h2_ref, aff_ref = refs[len(g_refs):]
    merged = None
    for i, br_ref in enumerate((ret_ref, sgu_ref, att_ref)):
        p = jnp.dot(br_ref[0], wb_ref[i], preferred_element_type=f32)
        g = jnp.concatenate([r[0] for r in g_refs[n_gate_blocks * i:n_gate_blocks * (i + 1)]], axis=1)
        t = jax.nn.sigmoid(g.astype(f32)) * p
        merged = t if merged is None else merged + t
    o = jnp.dot(merged.astype(bf16), wo_ref[...], preferred_element_type=f32)
    x = x_ref[0] + gate_ref[0] * o
    xo_ref[0] = x
    y = x * lax.rsqrt(jnp.mean(x * x, axis=-1, keepdims=True) + EPS)
    h2 = y * scale_ref[0] + shift_ref[0]
    h2_ref[0, :, 0, :] = h2
    logits = jnp.dot(h2.astype(bf16), wr_ref[...], preferred_element_type=f32)
    logits = jnp.where(_lane(logits.shape) < N_EXPERTS, logits, NEG_INF)
    m = jnp.max(logits, axis=-1, keepdims=True)
    e = jnp.exp(logits - m)
    aff_ref[0] = e / jnp.sum(e, axis=-1, keepdims=True)


def _merge(ret, sgu, att, z, gates_at, x, wb, wo, gate, scale, shift, wr):
    B, T, D = x.shape
    bw = ret.shape[-1]
    tm = _pick_tile(T, 512)
    tok = lambda b, i: (b, i, 0)
    vec = lambda b, i: (b, 0, 0)
    gw = 2 * LANES
    assert gates_at % gw == 0 and D % gw == 0
    n_gate_blocks = D // gw
    gate_specs = [pl.BlockSpec((1, tm, gw), lambda b, i, j=gates_at // gw + k: (b, i, j))
                  for k in range(N_BRANCHES * n_gate_blocks)]
    return pl.pallas_call(
        functools.partial(_merge_body, n_gate_blocks),
        grid=(B, T // tm),
        in_specs=[
            pl.BlockSpec((1, tm, bw), tok), pl.BlockSpec((1, tm, bw), tok), pl.BlockSpec((1, tm, bw), tok),
            *gate_specs,
            pl.BlockSpec((1, tm, D), tok),
            pl.BlockSpec((N_BRANCHES, bw, D), lambda b, i: (0, 0, 0)),
            pl.BlockSpec((D, D), lambda b, i: (0, 0)),
            pl.BlockSpec((1, 1, D), vec), pl.BlockSpec((1, 1, D), vec), pl.BlockSpec((1, 1, D), vec),
            pl.BlockSpec((D, LANES), lambda b, i: (0, 0)),
        ],
        out_specs=[
            pl.BlockSpec((1, tm, D), tok), pl.BlockSpec((1, tm, 1, D), lambda b, i: (b, i, 0, 0)),
            pl.BlockSpec((1, tm, LANES), tok),
        ],
        out_shape=[
            jax.ShapeDtypeStruct((B, T, D), f32),
            jax.ShapeDtypeStruct((B, T, 1, D), f32),
            jax.ShapeDtypeStruct((B, T, LANES), f32),
        ],
        compiler_params=_params("parallel", "parallel"),
        name="merge",
    )(ret, sgu, att, *([z] * len(gate_specs)), x, wb, wo, gate, scale, shift, wr)


ROUTE_TILE = 256
AFF_BITS = 31
MIN_NORMAL_BITS = 0x00800000


def _route_body(cap, aff_ref, tri_ref, pos_ref, post_ref, c0_ref, afft):
    T = aff_ref.shape[1]
    tt = tri_ref.shape[0]
    for j in range(T // tt):
        afft[:, tt * j:tt * (j + 1)] = aff_ref[0, tt * j:tt * (j + 1), :].T[:N_EXPERTS]

    def as_float(bits):
        return pltpu.bitcast(jnp.broadcast_to(bits, (N_EXPERTS, LANES)), f32)[:, :1]

    def count_at_least(v):
        return jnp.sum(jnp.where(afft[...] >= v, 1.0, 0.0), axis=1, keepdims=True)

    def refine(i, thr):
        cand = thr | jnp.left_shift(jnp.int32(1), AFF_BITS - 1 - i)
        return jnp.where(count_at_least(as_float(cand)) >= cap, cand, thr)

    thr = lax.fori_loop(0, AFF_BITS, refine, jnp.zeros((N_EXPERTS, 1), jnp.int32))
    hi_col = as_float(jnp.maximum(thr + 1, MIN_NORMAL_BITS))
    diag = (lax.broadcasted_iota(jnp.int32, (N_EXPERTS, LANES), 0)
            == lax.broadcasted_iota(jnp.int32, (N_EXPERTS, LANES), 1))

    def along_lanes(col):
        return jnp.sum(jnp.where(diag, col, 0.0), axis=0, keepdims=True)

    thr_f = along_lanes(as_float(thr))
    hi_f = along_lanes(hi_col)
    need = along_lanes(cap - count_at_least(hi_col))
    tri = tri_ref[...]
    carry_eq = jnp.zeros((1, LANES), f32)
    carry_sel = jnp.zeros((1, LANES), f32)
    for j in range(T // tt):
        a = aff_ref[0, tt * j:tt * (j + 1), :]
        eq = (a >= thr_f) & (a < hi_f)
        eqf = jnp.where(eq, 1.0, 0.0)
        eq_rank = carry_eq + jnp.dot(tri, eqf.astype(bf16), preferred_element_type=f32) - eqf
        sel = (a >= hi_f) | (eq & (eq_rank < need))
        self_ = jnp.where(sel, 1.0, 0.0)
        pos = carry_sel + jnp.dot(tri, self_.astype(bf16), preferred_element_type=f32) - self_
        pos = jnp.where(sel, pos, -1.0).astype(jnp.int32)
        pos_ref[0, tt * j:tt * (j + 1), :] = pos
        post_ref[0, j] = pos.T[:N_EXPERTS]
        c0_ref[0, j:j + 1, :] = carry_sel.astype(jnp.int32)
        carry_eq = carry_eq + jnp.sum(eqf, axis=0, keepdims=True)
        carry_sel = carry_sel + jnp.sum(self_, axis=0, keepdims=True)


def _route(aff, cap):
    B, T, _ = aff.shape
    tt = min(ROUTE_TILE, T)
    nt = T // tt
    tri = jnp.tril(jnp.ones((tt, tt), bf16))
    return pl.pallas_call(
        functools.partial(_route_body, cap),
        grid=(B,),
        in_specs=[pl.BlockSpec((1, T, LANES), lambda b: (b, 0, 0)), pl.BlockSpec((tt, tt), lambda b: (0, 0))],
        out_specs=[
            pl.BlockSpec((1, T, LANES), lambda b: (b, 0, 0)),
            pl.BlockSpec((1, nt, N_EXPERTS, tt), lambda b: (b, 0, 0, 0)),
            pl.BlockSpec((1, nt, LANES), lambda b: (b, 0, 0)),
        ],
        out_shape=[
            jax.ShapeDtypeStruct((B, T, LANES), jnp.int32),
            jax.ShapeDtypeStruct((B, nt, N_EXPERTS, tt), jnp.int32),
            jax.ShapeDtypeStruct((B, nt, LANES), jnp.int32),
        ],
        scratch_shapes=[pltpu.VMEM((N_EXPERTS, T), f32)],
        compiler_params=_params("parallel"),
        name="route",
    )(aff, tri)


def _slots_body(nt, cap, c0_ref, post_ref, idx_ref):
    b = pl.program_id(0)
    tt = post_ref.shape[-1]
    tok = lax.broadcasted_iota(jnp.int32, (LANES, tt), 1)
    diag = lax.broadcasted_iota(jnp.int32, (LANES, LANES), 0) == lax.broadcasted_iota(jnp.int32, (LANES, LANES), 1)

    def per_expert(e, carry):
        def tiles_starting_at_or_before(slot):
            return sum((c0_ref[(b * nt + j) * N_EXPERTS + e] <= slot).astype(jnp.int32) for j in range(nt))

        for k in range(idx_ref.shape[2]):
            slot = LANES * k + lax.broadcasted_iota(jnp.int32, (LANES, 1), 0)

            def match(j, acc):
                return acc + jnp.where(post_ref[0, j, pl.ds(e, 1), :] == slot, tok + j * tt, 0)

            j_lo = tiles_starting_at_or_before(LANES * k) - 1
            j_hi = tiles_starting_at_or_before(min(LANES * (k + 1), cap) - 1)
            acc = lax.fori_loop(j_lo, j_hi, match, jnp.zeros((LANES, tt), jnp.int32))
            col = jnp.sum(acc.astype(f32), axis=1, keepdims=True)
            row = jnp.sum(jnp.where(diag, col, 0.0), axis=0, keepdims=True)
            idx_ref[0, e, k:k + 1, :] = row.astype(jnp.int32)
        return carry

    lax.fori_loop(0, N_EXPERTS, per_expert, 0)


def _slots(c0s, post, cap):
    B, nt, E, tt = post.shape
    nblk = -(-cap // LANES)
    idx = pl.pallas_call(
        functools.partial(_slots_body, nt, cap),
        grid_spec=pltpu.PrefetchScalarGridSpec(
            num_scalar_prefetch=1,
            grid=(B,),
            in_specs=[pl.BlockSpec((1, nt, E, tt), lambda b, c0: (b, 0, 0, 0))],
            out_specs=pl.BlockSpec((1, E, nblk, LANES), lambda b, c0: (b, 0, 0, 0)),
        ),
        out_shape=jax.ShapeDtypeStruct((B, E, nblk, LANES), jnp.int32),
        compiler_params=_params("arbitrary"),
        name="slots",
    )(c0s, post)
    return idx.reshape(B, E, nblk * LANES)[:, :, :cap]


EXPERT_COPY_BATCHES = 3


def _expert_body(cap, idx_ref, h_ref, wg_ref, wu_ref, wd_ref, y_ref, rows, xs_s, act_s, hm_s, wg_s, wu_s, wd_s, sem):
    n_b = pl.num_programs(1)
    step = pl.program_id(0) * n_b + pl.program_id(1)
    last = pl.num_programs(0) * n_b - 1
    T = h_ref.shape[0] // n_b
    bounds = [cap * k // EXPERT_COPY_BATCHES for k in range(EXPERT_COPY_BATCHES + 1)]

    def start_rows(s, buf, lo, hi):
        e, b = s // n_b, s % n_b
        for r in range(lo, hi):
            t = idx_ref[(b * N_EXPERTS + e) * cap + r]
            pltpu.make_async_copy(h_ref.at[b * T + t], rows.at[buf, pl.ds(r, 1), :], sem.at[buf]).start()

    def wait_rows(buf):
        pltpu.make_async_copy(h_ref.at[pl.ds(0, cap), 0, :], rows.at[buf], sem.at[buf]).wait()

    @pl.when(step == 0)
    def _():
        start_rows(step, 0, 0, cap)

    @pl.when(pl.program_id(1) == 0)
    def _():
        wg_s[...] = wg_ref[0, 0].astype(bf16)
        wu_s[...] = wu_ref[0, 0].astype(bf16)
        wd_s[...] = wd_ref[0, 0].astype(bf16)

    for cur in range(2):
        @pl.when(step % 2 == cur)
        def _():
            wait_rows(cur)
            xs_s[...] = rows[cur].astype(bf16)

            def prefetch(k):
                @pl.when(step < last)
                def _():
                    start_rows(step + 1, 1 - cur, bounds[k], bounds[k + 1])

            prefetch(0)
            act_s[...] = jax.nn.silu(jnp.dot(xs_s[...], wg_s[...], preferred_element_type=f32))
            prefetch(1)
            hm_s[...] = (act_s[...] * jnp.dot(xs_s[...], wu_s[...], preferred_element_type=f32)).astype(bf16)
            prefetch(2)
            y_ref[0, 0] = jnp.dot(hm_s[...], wd_s[...], preferred_element_type=f32).astype(y_ref.dtype)


def _expert_ffn(idx, h2, layer, wg, wu, wd):
    B, T, _, D = h2.shape
    _, E, _, F = wg.shape
    cap = idx.shape[-1]
    wsp = lambda e, b, idx: (layer, e, 0, 0)
    return pl.pallas_call(
        functools.partial(_expert_body, cap),
        grid_spec=pltpu.PrefetchScalarGridSpec(
            num_scalar_prefetch=1,
            grid=(E, B),
            in_specs=[
                pl.BlockSpec(memory_space=pl.ANY),
                pl.BlockSpec((1, 1, D, F), wsp), pl.BlockSpec((1, 1, D, F), wsp), pl.BlockSpec((1, 1, F, D), wsp),
            ],
            out_specs=pl.BlockSpec((1, 1, cap, D), lambda e, b, idx: (b, e, 0, 0)),
            scratch_shapes=[pltpu.VMEM((2, cap, D), f32), pltpu.VMEM((cap, D), bf16), pltpu.VMEM((cap, F), f32),
                            pltpu.VMEM((cap, F), bf16),
                            pltpu.VMEM((D, F), bf16), pltpu.VMEM((D, F), bf16), pltpu.VMEM((F, D), bf16),
                            pltpu.SemaphoreType.DMA((2,))],
        ),
        out_shape=jax.ShapeDtypeStruct((B, E, cap, D), bf16),
        compiler_params=pltpu.CompilerParams(dimension_semantics=("arbitrary", "arbitrary"),
                                             vmem_limit_bytes=EXPERT_VMEM_LIMIT_BYTES),
        name="expert_ffn",
    )(idx.reshape(-1), h2.reshape(B * T, 1, D), wg, wu, wd)


Y_ROW_ALIGN = 16


def _combine_body(nt, cap, win, c0_ref, pos_ref, aff_ref, x_ref, gate_ref, y_ref, o_ref, buf, extra, sem, sem_x):
    b = pl.program_id(0)
    j = pl.program_id(1)
    step = b * nt + j
    tt = pos_ref.shape[1]
    stride = win if win == cap else win - Y_ROW_ALIGN

    def c0(bb, jj, e):
        return c0_ref[(bb * nt + jj) * N_EXPERTS + e]

    def window_start(lo):
        return pl.multiple_of(jnp.minimum(lo & (-Y_ROW_ALIGN), cap - win), Y_ROW_ALIGN)

    def copies(bb, jj, p, dst, s):
        return [pltpu.make_async_copy(
            y_ref.at[bb, e, pl.ds(window_start(c0(bb, jj, e) + p * stride), win), :],
            dst.at[pl.ds(e * win, win), :], s) for e in range(N_EXPERTS)]

    @pl.when(step == 0)
    def _():
        for cp in copies(b, j, 0, buf.at[0], sem.at[0]):
            cp.start()

    nxt = step + 1

    @pl.when(nxt < pl.num_programs(0) * nt)
    def _():
        for cp in copies(nxt // nt, nxt % nt, 0, buf.at[nxt % 2], sem.at[nxt % 2]):
            cp.start()

    lane = _lane((1, LANES))
    pos = pos_ref[0]
    aff = aff_ref[0]
    per_group = LANES // win

    def lane_vector(fn):
        v = jnp.zeros((1, LANES), jnp.int32)
        for e in range(N_EXPERTS):
            v = jnp.where(lane == e, fn(e), v)
        return v

    def weights(p):
        lo = lane_vector(lambda e: c0(b, j, e) + p * stride)
        start = lane_vector(lambda e: window_start(c0(b, j, e) + p * stride))
        rel = jnp.where((pos >= lo) & (pos < lo + stride), pos - start, -LANES)
        lanes_full = _lane((tt, LANES))
        groups = []
        for g in range(N_EXPERTS // per_group):
            q = jnp.zeros((tt, LANES), f32)
            for i in range(per_group):
                e = g * per_group + i
                q = jnp.where(rel[:, e:e + 1] + i * win == lanes_full, aff[:, e:e + 1], q)
            groups.append(q.astype(bf16))
        return jnp.concatenate(groups, axis=1)

    for cp in copies(b, j, 0, buf.at[step % 2], sem.at[step % 2]):
        cp.wait()
    moe = jnp.dot(weights(0), buf[step % 2], preferred_element_type=f32)

    def count(e):
        end = jnp.where(j + 1 < nt, c0(b, jnp.minimum(j + 1, nt - 1), e), cap)
        return end - c0(b, j, e)

    max_count = functools.reduce(jnp.maximum, [count(e) for e in range(N_EXPERTS)])

    def more(p, acc):
        cps = copies(b, j, p, extra, sem_x.at[0])
        for cp in cps:
            cp.start()
        for cp in cps:
            cp.wait()
        return acc + jnp.dot(weights(p), extra[...], preferred_element_type=f32)

    moe = lax.fori_loop(1, (max_count + stride - 1) // stride, more, moe)
    o_ref[0] = x_ref[0] + gate_ref[0] * moe


def _combine(c0s, pos, aff, x, gate, y):
    B, T, D = x.shape
    E, cap = y.shape[1], y.shape[2]
    tt = min(ROUTE_TILE, T)
    nt = T // tt
    win = min(cap, LANES // 2)
    assert (cap - win) % Y_ROW_ALIGN == 0 and LANES % win == 0
    tok = lambda b, j, c0: (b, j, 0)
    return pl.pallas_call(
        functools.partial(_combine_body, nt, cap, win),
        grid_spec=pltpu.PrefetchScalarGridSpec(
            num_scalar_prefetch=1,
            grid=(B, nt),
            in_specs=[
                pl.BlockSpec((1, tt, LANES), tok), pl.BlockSpec((1, tt, LANES), tok), pl.BlockSpec((1, tt, D), tok),
                pl.BlockSpec((1, 1, D), lambda b, j, c0: (b, 0, 0)),
                pl.BlockSpec(memory_space=pl.ANY),
            ],
            out_specs=pl.BlockSpec((1, tt, D), tok),
            scratch_shapes=[pltpu.VMEM((2, E * win, D), bf16), pltpu.VMEM((E * win, D), bf16),
                            pltpu.SemaphoreType.DMA((2,)), pltpu.SemaphoreType.DMA((1,))],
        ),
        out_shape=jax.ShapeDtypeStruct((B, T, D), f32),
        compiler_params=_params("arbitrary", "arbitrary"),
        name="combine",
    )(c0s, pos, aff, x, gate, y)


def _expert_choice(h2, aff, x, gate2, p):
    T = x.shape[1]
    cap = CAPACITY_FACTOR * T // N_EXPERTS
    pos, post, c0 = _route(aff, cap)
    c0s = c0[:, :, :N_EXPERTS].reshape(-1)
    y = _expert_ffn(_slots(c0s, post, cap), h2, p['layer'], p['w_exp_gate'], p['w_exp_up'], p['w_exp_down'])
    return _combine(c0s, pos, aff, x, gate2, y)


def _rope_tables(T):
    rows = T // GRID_W
    row = jnp.repeat(jnp.arange(rows), GRID_W).astype(f32)
    col = jnp.tile(jnp.arange(GRID_W), rows).astype(f32)
    n_freq = HEAD_DIM // 4
    inv = jnp.power(ROPE_BASE, -jnp.arange(n_freq, dtype=f32) / n_freq)
    ang = jnp.concatenate([row[:, None] * inv, col[:, None] * inv], axis=-1)
    cos, sin = jnp.cos(ang), jnp.sin(ang)
    reps = LANES // HEAD_DIM
    return jnp.tile(jnp.concatenate([cos, cos], axis=-1), (1, reps)), jnp.tile(
        jnp.concatenate([-sin, sin], axis=-1), (1, reps))


def _identity_rope(T):
    return jnp.ones((T, LANES), f32), jnp.zeros((T, LANES), f32)


def _retention_tables(decay_fwd, decay_bwd):
    lgf = jax.nn.log_sigmoid(decay_fwd.astype(f32))
    lgb = jax.nn.log_sigmoid(decay_bwd.astype(f32))
    C = CHUNK
    k_scale = HEAD_DIM ** -0.5
    idx = jnp.arange(C, dtype=f32)
    npair = RET_HEADS // 2

    def lanes(t):
        return jnp.repeat(t.reshape(npair, 2, C).transpose(0, 2, 1), HEAD_DIM, axis=-1)

    diff = idx[:, None] - idx[None, :]
    dm = jnp.where(diff >= 0, jnp.exp(jnp.maximum(diff, 0.0)[None] * lgf[:, None, None]),
                   jnp.exp(jnp.maximum(-diff, 0.0)[None] * lgb[:, None, None])) * k_scale

    def chunk_decay(lg):
        rows = jnp.repeat(jnp.exp(C * lg).reshape(npair, 2), HEAD_DIM, axis=-1)
        return jnp.broadcast_to(rows[:, :, None], (npair, LANES, 2 * LANES))

    r = jnp.arange(LANES)[:, None] // HEAD_DIM
    c = jnp.arange(2 * LANES)[None, :] // LANES
    return {
        'wkf': lanes(jnp.exp((C - 1 - idx)[None, :] * lgf[:, None])) * k_scale,
        'wkb': lanes(jnp.exp(idx[None, :] * lgb[:, None])) * k_scale,
        'qdf': lanes(jnp.exp((idx + 1.0)[None, :] * lgf[:, None])),
        'qdb': lanes(jnp.exp((C - idx)[None, :] * lgb[:, None])),
        'dm': dm, 'cdf': chunk_decay(lgf), 'cdb': chunk_decay(lgb), 'bm': (r == c).astype(f32),
    }


def _attention_tables(cos, sin, q_gain, k_gain):
    reps = LANES // HEAD_DIM
    half = HEAD_DIM // 2
    swap = lambda g: jnp.concatenate([g[half:], g[:half]])
    gq, gk = jnp.tile(q_gain, reps)[None], jnp.tile(k_gain, reps)[None]
    gqs, gks = jnp.tile(swap(q_gain), reps)[None], jnp.tile(swap(k_gain), reps)[None]
    r = jnp.arange(LANES) // HEAD_DIM
    return {'qa': cos * gq, 'qb': sin * gqs, 'ka': cos * gk, 'kb': sin * gks, 'gq': gq, 'gk': gk,
            'ones': (r[:, None] == r[None, :]).astype(bf16)}


def _layer(x, xc, mod_lat, mod_ctx, rope, p, last):
    B, S, D = x.shape
    L = xc.shape[1]
    lay = _layout(D)
    sh1, sc1, g1, sh2, sc2, g2 = jnp.split(mod_lat, 6, axis=-1)
    csh1, csc1, cg1, csh2, csc2, cg2 = jnp.split(mod_ctx, 6, axis=-1)
    n1 = p['norm1_g'][None, None, :]
    n2 = p['norm2_g'][None, None, :]

    def bcast(t):
        return jnp.broadcast_to(t, (B, 1, D))

    z = _proj_in(x, n1 * (1 + sc1), sh1, p['w_in'])
    zc = _proj_in(xc, bcast(n1 * (1 + csc1)), bcast(csh1), p['w_in'])

    cos, sin = rope
    cos_c, sin_c = _identity_rope(L)
    rtabs = _retention_tables(p['ret_decay_fwd'], p['ret_decay_bwd'])
    atabs = _attention_tables(cos, sin, p['q_norm_g'], p['k_norm_g'])
    sgu_w = p['sgu_w'].astype(bf16)
    gw = D // 2 // SGU_GROUPS
    sgu_bias = jnp.repeat(p['sgu_b'].T, gw, axis=-1)

    zero = jnp.zeros((B, RET_HEADS // 2, LANES, 2 * LANES), f32)
    sfc, sbc, s_f, s_b = _ret_states(zc, lay, cos_c, sin_c, rtabs, zero, zero)
    sf, sb, _, _ = _ret_states(z, lay, cos, sin, rtabs, s_f, s_b)
    ret, sgu = _ret_sgu(z, lay, cos, sin, sf, sb, rtabs, sgu_w, sgu_bias)
    att = _attention(z, zc, lay, atabs, p['attn_sink'], True)

    x, h2, aff = _merge(ret, sgu, att, z, lay['gates'][0], x, p['w_branch'], p['w_out'], g1, n2 * (1 + sc2), sh2,
                        p['w_router'])
    x = _expert_choice(h2, aff, x, g2, p)
    if last:
        return x, None

    ret_c, sgu_c = _ret_sgu(zc, lay, cos_c, sin_c, sfc, sbc, rtabs, sgu_w, sgu_bias)
    att_c = _attention(zc, zc, lay, atabs, p['attn_sink'], False)
    xc, h2c, affc = _merge(ret_c, sgu_c, att_c, zc, lay['gates'][0], xc, p['w_branch'], p['w_out'], bcast(cg1),
                           bcast(n2 * (1 + csc2)), bcast(csh2), p['w_router'])
    xc = _expert_choice(h2c, affc, xc, bcast(cg2), p)
    return x, xc


def kernel(x, c, ctx, c_ctx, w_mod, b_mod, norm1_g, norm2_g, w_in, ret_decay_fwd, ret_decay_bwd, sgu_w, sgu_b,
           q_norm_g, k_norm_g, attn_sink, w_branch, w_out, w_router, w_exp_gate, w_exp_up, w_exp_down):
    B, S, D = x.shape
    depth = w_in.shape[0]
    rope = _rope_tables(S)
    c_act = jax.nn.silu(c)
    c_ctx_act = jax.nn.silu(c_ctx)
    xc = ctx
    hp = lax.Precision.HIGHEST
    for l in range(depth):
        wr = jnp.zeros((D, LANES), f32).at[:, :N_EXPERTS].set(w_router[l]).astype(bf16)
        p = {
            'norm1_g': norm1_g[l], 'norm2_g': norm2_g[l], 'w_in': w_in[l].astype(bf16),
            'ret_decay_fwd': ret_decay_fwd[l], 'ret_decay_bwd': ret_decay_bwd[l],
            'sgu_w': sgu_w[l], 'sgu_b': sgu_b[l],
            'q_norm_g': q_norm_g[l], 'k_norm_g': k_norm_g[l], 'attn_sink': attn_sink[l],
            'w_branch': w_branch[l].astype(bf16), 'w_out': w_out[l].astype(bf16), 'w_router': wr,
            'layer': l, 'w_exp_gate': w_exp_gate, 'w_exp_up': w_exp_up, 'w_exp_down': w_exp_down,
        }
        mod_lat = (jnp.dot(c_act, w_mod[l], precision=hp) + b_mod[l])[:, None, :]
        mod_ctx = (jnp.dot(c_ctx_act, w_mod[l], precision=hp) + b_mod[l])[None, None, :]
        x, xc = _layer(x, xc, mod_lat, mod_ctx, rope, p, l == depth - 1)
    return x
```

```python
import functools

import jax
import jax.numpy as jnp
from jax import lax
from jax.experimental import pallas as pl
from jax.experimental.pallas import tpu as pltpu

GRID_W = 64
N_BRANCHES = 3
RET_HEADS = 4
SGU_GROUPS = 4
CHUNK = 128
HEAD_DIM = 64
ATT_KV_HEADS = 2
ROPE_BASE = 10000.0
N_EXPERTS = 16
CAPACITY_FACTOR = 2
EPS = 1e-6
NEG_INF = -1e30

LANES = 128
VMEM_LIMIT_BYTES = 48 * 1024 * 1024
EXPERT_VMEM_LIMIT_BYTES = 56 * 1024 * 1024

f32 = jnp.float32
bf16 = jnp.bfloat16


def _layout(d_model):
    bw = d_model // 2
    qk = RET_HEADS * HEAD_DIM
    kv = ATT_KV_HEADS * HEAD_DIM
    widths = (('ret_q', qk), ('ret_k', qk), ('ret_v', bw), ('ret_g', bw), ('sgu_u', bw), ('sgu_v', bw),
              ('att_q', bw), ('att_k', kv), ('att_v', kv), ('gates', N_BRANCHES * d_model))
    out, start = {}, 0
    for name, w in widths:
        out[name] = (start, w)
        start += w
    return out


def _chunks_per_step(T):
    return 2 if (T // CHUNK) % 2 == 0 else 1


def _pick_tile(n, pref):
    t = min(n, pref)
    while n % t:
        t //= 2
    return t


def _params(*sem):
    return pltpu.CompilerParams(dimension_semantics=sem, vmem_limit_bytes=VMEM_LIMIT_BYTES)


def _proj_in_body(x_ref, scale_ref, shift_ref, w_ref, o_ref, h_ref):
    @pl.when(pl.program_id(2) == 0)
    def _():
        x = x_ref[0]
        y = x * lax.rsqrt(jnp.mean(x * x, axis=-1, keepdims=True) + EPS)
        h_ref[...] = (y * scale_ref[0] + shift_ref[0]).astype(bf16)

    o_ref[0] = jnp.dot(h_ref[...], w_ref[...], preferred_element_type=f32).astype(o_ref.dtype)


def _proj_in(x, scale, shift, w):
    B, T, D = x.shape
    N = w.shape[1]
    tm = _pick_tile(T, 2048)
    tn = 1280 if N % 1280 == 0 else _pick_tile(N, 1024)
    return pl.pallas_call(
        _proj_in_body,
        grid=(B, T // tm, N // tn),
        in_specs=[
            pl.BlockSpec((1, tm, D), lambda b, i, j: (b, i, 0)),
            pl.BlockSpec((1, 1, D), lambda b, i, j: (b, 0, 0)),
            pl.BlockSpec((1, 1, D), lambda b, i, j: (b, 0, 0)),
            pl.BlockSpec((D, tn), lambda b, i, j: (0, j)),
        ],
        out_specs=pl.BlockSpec((1, tm, tn), lambda b, i, j: (b, i, j)),
        out_shape=jax.ShapeDtypeStruct((B, T, N), bf16),
        scratch_shapes=[pltpu.VMEM((tm, D), bf16)],
        compiler_params=_params("parallel", "parallel", "arbitrary"),
        name="proj_in",
    )(x, scale, shift, w)


def _lane(shape):
    return lax.broadcasted_iota(jnp.int32, shape, 1)


def _swap_halves(x):
    return jnp.where((_lane(x.shape) & (HEAD_DIM // 2)) == 0,
                     pltpu.roll(x, LANES - HEAD_DIM // 2, 1), pltpu.roll(x, HEAD_DIM // 2, 1))


def _rope(x, cos, sin_signed):
    return x * cos + _swap_halves(x) * sin_signed


def _head_sumsq(x, ones_blockdiag):
    x2 = x * x
    hi = x2.astype(bf16)
    lo = (x2 - hi.astype(f32)).astype(bf16)
    return (jnp.dot(hi, ones_blockdiag, preferred_element_type=f32)
            + jnp.dot(lo, ones_blockdiag, preferred_element_type=f32))


def _dot_nt(a, b):
    return lax.dot_general(a, b, (((1,), (1,)), ((), ())), preferred_element_type=f32)


def _dot_tn(a, b):
    return lax.dot_general(a, b, (((0,), (0,)), ((), ())), preferred_element_type=f32)


def _ret_state_body(cpb, kf_ref, vf_ref, kb_ref, vb_ref, cosf_ref, sinf_ref, cosb_ref, sinb_ref,
                    wkf_ref, wkb_ref, cdf_ref, cdb_ref, bm_ref, s0f_ref, s0b_ref,
                    sf_ref, sb_ref, ff_ref, fb_ref, accf, accb):
    n = pl.program_id(1)

    @pl.when(n == 0)
    def _():
        accf[...] = s0f_ref[0]
        accb[...] = s0b_ref[0]

    def one_chunk(c, k_ref, v_ref, cos_ref, sin_ref, wk_ref, cd_ref, out_ref, acc):
        rows = slice(CHUNK * c, CHUNK * (c + 1))
        k = k_ref[0, rows, :].astype(f32)
        v = v_ref[0, rows, :]
        for p in range(RET_HEADS // 2):
            kp = _rope(k[:, LANES * p:LANES * (p + 1)], cos_ref[rows, :], sin_ref[rows, :])
            kw = (kp * wk_ref[p]).astype(bf16)
            upd = _dot_tn(kw, v[:, 2 * LANES * p:2 * LANES * (p + 1)])
            s = acc[p]
            out_ref[0, c, p] = s.astype(out_ref.dtype)
            acc[p] = cd_ref[p] * s + bm_ref[...] * upd

    for c in range(cpb):
        one_chunk(c, kf_ref, vf_ref, cosf_ref, sinf_ref, wkf_ref, cdf_ref, sf_ref, accf)
        one_chunk(cpb - 1 - c, kb_ref, vb_ref, cosb_ref, sinb_ref, wkb_ref, cdb_ref, sb_ref, accb)

    @pl.when(n == pl.num_programs(1) - 1)
    def _():
        ff_ref[0] = accf[...]
        fb_ref[0] = accb[...]


def _ret_states(z, lay, cos, sin, tabs, s0f, s0b):
    B, T, _ = z.shape
    cpb = _chunks_per_step(T)
    rows = cpb * CHUNK
    nc = T // rows
    kq = lay['ret_k'][0] // (RET_HEADS * HEAD_DIM)
    vq = lay['ret_v'][0] // lay['ret_v'][1]
    vw = lay['ret_v'][1]
    kw = RET_HEADS * HEAD_DIM
    npair = RET_HEADS // 2
    st_shape = (npair, LANES, 2 * LANES)
    const3 = lambda b, n: (0, 0, 0)
    return pl.pallas_call(
        functools.partial(_ret_state_body, cpb),
        grid=(B, nc),
        in_specs=[
            pl.BlockSpec((1, rows, kw), lambda b, n: (b, n, kq)),
            pl.BlockSpec((1, rows, vw), lambda b, n: (b, n, vq)),
            pl.BlockSpec((1, rows, kw), lambda b, n: (b, nc - 1 - n, kq)),
            pl.BlockSpec((1, rows, vw), lambda b, n: (b, nc - 1 - n, vq)),
            pl.BlockSpec((rows, LANES), lambda b, n: (n, 0)),
            pl.BlockSpec((rows, LANES), lambda b, n: (n, 0)),
            pl.BlockSpec((rows, LANES), lambda b, n: (nc - 1 - n, 0)),
            pl.BlockSpec((rows, LANES), lambda b, n: (nc - 1 - n, 0)),
            pl.BlockSpec((npair, CHUNK, LANES), const3),
            pl.BlockSpec((npair, CHUNK, LANES), const3),
            pl.BlockSpec(st_shape, const3),
            pl.BlockSpec(st_shape, const3),
            pl.BlockSpec((LANES, 2 * LANES), lambda b, n: (0, 0)),
            pl.BlockSpec((1,) + st_shape, lambda b, n: (b, 0, 0, 0)),
            pl.BlockSpec((1,) + st_shape, lambda b, n: (b, 0, 0, 0)),
        ],
        out_specs=[
            pl.BlockSpec((1, cpb) + st_shape, lambda b, n: (b, n, 0, 0, 0)),
            pl.BlockSpec((1, cpb) + st_shape, lambda b, n: (b, nc - 1 - n, 0, 0, 0)),
            pl.BlockSpec((1,) + st_shape, lambda b, n: (b, 0, 0, 0)),
            pl.BlockSpec((1,) + st_shape, lambda b, n: (b, 0, 0, 0)),
        ],
        out_shape=[
            jax.ShapeDtypeStruct((B, nc * cpb) + st_shape, bf16),
            jax.ShapeDtypeStruct((B, nc * cpb) + st_shape, bf16),
            jax.ShapeDtypeStruct((B,) + st_shape, f32),
            jax.ShapeDtypeStruct((B,) + st_shape, f32),
        ],
        scratch_shapes=[pltpu.VMEM(st_shape, f32), pltpu.VMEM(st_shape, f32)],
        compiler_params=_params("parallel", "arbitrary"),
        name="ret_states",
    )(z, z, z, z, cos, sin, cos, sin, tabs['wkf'], tabs['wkb'], tabs['cdf'], tabs['cdb'], tabs['bm'], s0f, s0b)


def _ret_sgu_body(cpb, *refs):
    for c in range(cpb):
        _ret_sgu_chunk(c, *refs)


def _ret_sgu_chunk(c, q_ref, k_ref, v_ref, g_ref, u_ref, w_ref, cos_ref, sin_ref, sf_ref, sb_ref,
                   dm_ref, qdf_ref, qdb_ref, sw_ref, sbias_ref, ret_ref, sgu_ref):
    rows = slice(CHUNK * c, CHUNK * (c + 1))
    q = q_ref[0, rows, :].astype(f32)
    k = k_ref[0, rows, :].astype(f32)
    v = v_ref[0, rows, :]
    g = g_ref[0, rows, :].astype(f32)
    cos = cos_ref[rows, :]
    sin = sin_ref[rows, :]
    first_half = _lane((CHUNK, LANES)) < HEAD_DIM
    for p in range(RET_HEADS // 2):
        qp = _rope(q[:, LANES * p:LANES * (p + 1)], cos, sin)
        kp = _rope(k[:, LANES * p:LANES * (p + 1)], cos, sin).astype(bf16)
        cross = (jnp.dot((qp * qdf_ref[p]).astype(bf16), sf_ref[0, c, p], preferred_element_type=f32)
                 + jnp.dot((qp * qdb_ref[p]).astype(bf16), sb_ref[0, c, p], preferred_element_type=f32))
        for a in range(2):
            h = 2 * p + a
            qm = jnp.where(first_half if a == 0 else ~first_half, qp, 0.0).astype(bf16)
            sc = (_dot_nt(qm, kp) * dm_ref[h]).astype(bf16)
            o = jnp.dot(sc, v[:, LANES * h:LANES * (h + 1)], preferred_element_type=f32)
            o = o + cross[:, LANES * a:LANES * (a + 1)]
            oc = o - jnp.mean(o, axis=-1, keepdims=True)
            y = oc * lax.rsqrt(jnp.mean(oc * oc, axis=-1, keepdims=True) + EPS)
            ret_ref[0, rows, LANES * h:LANES * (h + 1)] = (
                y * jax.nn.silu(g[:, LANES * h:LANES * (h + 1)])).astype(ret_ref.dtype)

    u = jax.nn.gelu(u_ref[0, rows, :].astype(f32))
    w = jax.nn.gelu(w_ref[0, rows, :].astype(f32))
    wc = w - jnp.mean(w, axis=-1, keepdims=True)
    wn = (wc * lax.rsqrt(jnp.mean(wc * wc, axis=-1, keepdims=True) + EPS)).astype(bf16)
    gw = wn.shape[-1] // SGU_GROUPS
    for i in range(SGU_GROUPS):
        sl = slice(gw * i, gw * (i + 1))
        mixed = jnp.dot(sw_ref[i], wn[:, sl], preferred_element_type=f32) + sbias_ref[:, sl]
        sgu_ref[0, rows, sl] = (u[:, sl] * mixed).astype(sgu_ref.dtype)


def _ret_sgu(z, lay, cos, sin, sf, sb, tabs, sgu_w, sgu_bias):
    B, T, _ = z.shape
    cpb = _chunks_per_step(T)
    rows = cpb * CHUNK
    nc = T // rows
    bw = lay['ret_v'][1]

    def zs(name):
        s, w = lay[name]
        return pl.BlockSpec((1, rows, w), lambda b, n, j=s // w: (b, n, j))

    st_shape = (RET_HEADS // 2, LANES, 2 * LANES)
    const3 = lambda b, n: (0, 0, 0)
    out_spec = pl.BlockSpec((1, rows, bw), lambda b, n: (b, n, 0))
    return pl.pallas_call(
        functools.partial(_ret_sgu_body, cpb),
        grid=(B, nc),
        in_specs=[
            zs('ret_q'), zs('ret_k'), zs('ret_v'), zs('ret_g'), zs('sgu_u'), zs('sgu_v'),
            pl.BlockSpec((rows, LANES), lambda b, n: (n, 0)),
            pl.BlockSpec((rows, LANES), lambda b, n: (n, 0)),
            pl.BlockSpec((1, cpb) + st_shape, lambda b, n: (b, n, 0, 0, 0)),
            pl.BlockSpec((1, cpb) + st_shape, lambda b, n: (b, n, 0, 0, 0)),
            pl.BlockSpec((RET_HEADS, CHUNK, CHUNK), const3),
            pl.BlockSpec((RET_HEADS // 2, CHUNK, LANES), const3),
            pl.BlockSpec((RET_HEADS // 2, CHUNK, LANES), const3),
            pl.BlockSpec((SGU_GROUPS, CHUNK, CHUNK), const3),
            pl.BlockSpec((CHUNK, bw), lambda b, n: (0, 0)),
        ],
        out_specs=[out_spec, out_spec],
        out_shape=[jax.ShapeDtypeStruct((B, T, bw), bf16), jax.ShapeDtypeStruct((B, T, bw), bf16)],
        compiler_params=_params("parallel", "parallel"),
        name="ret_sgu",
    )(z, z, z, z, z, z, cos, sin, sf, sb, tabs['dm'], tabs['qdf'], tabs['qdb'], sgu_w, sgu_bias)


def _att_body(local, qb, sink_ref, q_ref, *refs):
    if local:
        (kp_ref, kc_ref, kn_ref, vp_ref, vc_ref, vn_ref, qa_ref, qb_ref,
         kap_ref, kbp_ref, kac_ref, kbc_ref, kan_ref, kbn_ref, ck_ref, cv_ref, gk_ref, ones_ref, o_ref) = refs
    else:
        qa_ref, ck_ref, cv_ref, gk_ref, ones_ref, o_ref = refs
    n = pl.program_id(1)
    n_blocks = pl.num_programs(1) * qb
    ones = ones_ref[...]
    inv_hd = 1.0 / HEAD_DIM

    def norm_rope(x_ref_val, ta, tb):
        x = x_ref_val.astype(f32)
        r = lax.rsqrt(_head_sumsq(x, ones) * inv_hd + EPS)
        y = x * ta if tb is None else x * ta + _swap_halves(x) * tb
        return y * r

    def dup(x):
        lo = _lane(x.shape) < HEAD_DIM
        xr = pltpu.roll(x, HEAD_DIM, 1)
        return jnp.where(lo, x, xr).astype(bf16), jnp.where(lo, xr, x).astype(bf16)

    def blk(i):
        return slice(CHUNK * i, CHUNK * (i + 1))

    k_ctx = dup(norm_rope(ck_ref[0], gk_ref[...], None))
    v_ctx = dup(cv_ref[0].astype(f32))
    if local:
        k_loc = ([dup(norm_rope(kp_ref[0], kap_ref[...], kbp_ref[...]))]
                 + [dup(norm_rope(kc_ref[0, blk(i), :], kac_ref[blk(i), :], kbc_ref[blk(i), :])) for i in range(qb)]
                 + [dup(norm_rope(kn_ref[0], kan_ref[...], kbn_ref[...]))])
        v_loc = ([dup(vp_ref[0].astype(f32))] + [dup(vc_ref[0, blk(i), :].astype(f32)) for i in range(qb)]
                 + [dup(vn_ref[0].astype(f32))])
    n_loc = 3 * CHUNK if local else 0

    n_groups = q_ref.shape[-1] // LANES
    groups_per_kv = n_groups // ATT_KV_HEADS
    rows = 2 * groups_per_kv * CHUNK
    log2e = 1.4426950408889634
    first_half = _lane((CHUNK, LANES)) < HEAD_DIM
    for qi in range(qb):
        if local:
            i = lax.broadcasted_iota(jnp.int32, (CHUNK, n_loc), 0)
            j = lax.broadcasted_iota(jnp.int32, (CHUNK, n_loc), 1)
            has_prev = (n * qb + qi > 0).astype(jnp.int32)
            has_next = (n * qb + qi < n_blocks - 1).astype(jnp.int32)
            valid = (j >= i * has_prev + CHUNK * (1 - has_prev)) & (j <= 2 * CHUNK - 1 + has_next * (i + 1))
            bias = jnp.where(valid, 0.0, NEG_INF)
            bias = jnp.concatenate([bias] * (rows // CHUNK), axis=0)
        for h in range(ATT_KV_HEADS):
            keys = ([k_loc[qi + d][h] for d in range(3)] if local else []) + [k_ctx[h]]
            vals = ([v_loc[qi + d][h] for d in range(3)] if local else []) + [v_ctx[h]]
            keys = jnp.concatenate(keys, axis=0) if local else keys[0]
            vals = jnp.concatenate(vals, axis=0) if local else vals[0]
            q_rows, sink_rows = [], []
            for c in range(groups_per_kv * h, groups_per_kv * (h + 1)):
                xq = q_ref[0, blk(qi), LANES * c:LANES * (c + 1)]
                qn = norm_rope(xq, qa_ref[blk(qi), :] if local else qa_ref[...],
                               qb_ref[blk(qi), :] if local else None) * (HEAD_DIM ** -0.5 * log2e)
                for a in range(2):
                    q_rows.append(jnp.where(first_half if a == 0 else ~first_half, qn, 0.0).astype(bf16))
                    sink_rows.append(jnp.full((CHUNK, 1), sink_ref[2 * c + a] * log2e, f32))
            s = _dot_nt(jnp.concatenate(q_rows, axis=0), keys)
            sk = jnp.concatenate(sink_rows, axis=0)
            s_ctx = s[:, n_loc:]
            m = jnp.maximum(jnp.max(s_ctx, axis=-1, keepdims=True), sk)
            if local:
                s_loc = s[:, :n_loc] + bias
                m = jnp.maximum(m, jnp.max(s_loc, axis=-1, keepdims=True))
            p_ctx = jnp.exp2(s_ctx - m)
            d = jnp.sum(p_ctx, axis=-1, keepdims=True) + jnp.exp2(sk - m)
            o = jnp.dot(p_ctx.astype(bf16), vals[n_loc:], preferred_element_type=f32)
            if local:
                p_loc = jnp.exp2(s_loc - m)
                d = d + jnp.sum(p_loc, axis=-1, keepdims=True)
                o = o + jnp.dot(p_loc.astype(bf16), vals[:n_loc], preferred_element_type=f32)
            o = o * (1.0 / d)
            for ci in range(groups_per_kv):
                c = groups_per_kv * h + ci
                o0 = o[2 * CHUNK * ci:2 * CHUNK * ci + CHUNK]
                o1 = o[2 * CHUNK * ci + CHUNK:2 * CHUNK * (ci + 1)]
                o_ref[0, blk(qi), LANES * c:LANES * (c + 1)] = jnp.where(first_half, o0, o1).astype(o_ref.dtype)


def _attention(z, zc, lay, att_tabs, sink, local):
    B, T, _ = z.shape
    L = zc.shape[1]
    nb = T // CHUNK
    qb = _chunks_per_step(T)
    rows = qb * CHUNK
    qs, qw = lay['att_q']
    ks, kw = lay['att_k']
    vs, _ = lay['att_v']
    kj, vj = ks // kw, vs // kw
    prev = lambda n: jnp.maximum(n * qb - 1, 0)
    nxt = lambda n: jnp.minimum(n * qb + qb, nb - 1)
    one = lambda f: pl.BlockSpec((CHUNK, LANES), lambda b, n: (f(n), 0))
    cur = pl.BlockSpec((rows, LANES), lambda b, n: (n, 0))
    in_specs = [pl.BlockSpec(memory_space=pltpu.SMEM),
                pl.BlockSpec((1, rows, qw), lambda b, n: (b, n, qs // qw))]
    args = [sink, z]
    if local:
        for j in (kj, vj):
            in_specs += [pl.BlockSpec((1, CHUNK, kw), lambda b, n, j=j: (b, prev(n), j)),
                         pl.BlockSpec((1, rows, kw), lambda b, n, j=j: (b, n, j)),
                         pl.BlockSpec((1, CHUNK, kw), lambda b, n, j=j: (b, nxt(n), j))]
        args += [z] * 6
        in_specs += [cur, cur, one(prev), one(prev), cur, cur, one(nxt), one(nxt)]
        args += [att_tabs['qa'], att_tabs['qb'], att_tabs['ka'], att_tabs['kb'], att_tabs['ka'], att_tabs['kb'],
                 att_tabs['ka'], att_tabs['kb']]
    else:
        in_specs += [pl.BlockSpec((1, LANES), lambda b, n: (0, 0))]
        args += [att_tabs['gq']]
    in_specs += [pl.BlockSpec((1, L, kw), lambda b, n: (b, 0, kj)),
                 pl.BlockSpec((1, L, kw), lambda b, n: (b, 0, vj)),
                 pl.BlockSpec((1, LANES), lambda b, n: (0, 0)),
                 pl.BlockSpec((LANES, LANES), lambda b, n: (0, 0))]
    args += [zc, zc, att_tabs['gk'], att_tabs['ones']]
    return pl.pallas_call(
        functools.partial(_att_body, local, qb),
        grid=(B, nb // qb),
        in_specs=in_specs,
        out_specs=pl.BlockSpec((1, rows, qw), lambda b, n: (b, n, 0)),
        out_shape=jax.ShapeDtypeStruct((B, T, qw), bf16),
        compiler_params=_params("parallel", "parallel"),
        name="attention" if local else "ctx_attention",
    )(*args)


def _merge_body(n_gate_blocks, ret_ref, sgu_ref, att_ref, *refs):
    g_refs = refs[:N_BRANCHES * n_gate_blocks]
    x_ref, wb_ref, wo_ref, gate_ref, scale_ref, shift_ref, wr_ref, xo_ref, h2_ref, aff_ref = refs[len(g_refs):]
    merged = None
    for i, br_ref in enumerate((ret_ref, sgu_ref, att_ref)):
        p = jnp.dot(br_ref[0], wb_ref[i], preferred_element_type=f32)
        g = jnp.concatenate([r[0] for r in g_refs[n_gate_blocks * i:n_gate_blocks * (i + 1)]], axis=1)
        t = jax.nn.sigmoid(g.astype(f32)) * p
        merged = t if merged is None else merged + t
    o = jnp.dot(merged.astype(bf16), wo_ref[...], preferred_element_type=f32)
    x = x_ref[0] + gate_ref[0] * o
    xo_ref[0] = x
    y = x * lax.rsqrt(jnp.mean(x * x, axis=-1, keepdims=True) + EPS)
    h2 = y * scale_ref[0] + shift_ref[0]
    h2_ref[0, :, 0, :] = h2
    logits = jnp.dot(h2.astype(bf16), wr_ref[...], preferred_element_type=f32)
    logits = jnp.where(_lane(logits.shape) < N_EXPERTS, logits, NEG_INF)
    m = jnp.max(logits, axis=-1, keepdims=True)
    e = jnp.exp(logits - m)
    aff_ref[0] = e / jnp.sum(e, axis=-1, keepdims=True)


def _merge(ret, sgu, att, z, gates_at, x, wb, wo, gate, scale, shift, wr):
    B, T, D = x.shape
    bw = ret.shape[-1]
    tm = _pick_tile(T, 512)
    tok = lambda b, i: (b, i, 0)
    vec = lambda b, i: (b, 0, 0)
    gw = 2 * LANES
    assert gates_at % gw == 0 and D % gw == 0
    n_gate_blocks = D // gw
    gate_specs = [pl.BlockSpec((1, tm, gw), lambda b, i, j=gates_at // gw + k: (b, i, j))
                  for k in range(N_BRANCHES * n_gate_blocks)]
    return pl.pallas_call(
        functools.partial(_merge_body, n_gate_blocks),
        grid=(B, T // tm),
        in_specs=[
            pl.BlockSpec((1, tm, bw), tok), pl.BlockSpec((1, tm, bw), tok), pl.BlockSpec((1, tm, bw), tok),
            *gate_specs,
            pl.BlockSpec((1, tm, D), tok),
            pl.BlockSpec((N_BRANCHES, bw, D), lambda b, i: (0, 0, 0)),
            pl.BlockSpec((D, D), lambda b, i: (0, 0)),
            pl.BlockSpec((1, 1, D), vec), pl.BlockSpec((1, 1, D), vec), pl.BlockSpec((1, 1, D), vec),
            pl.BlockSpec((D, LANES), lambda b, i: (0, 0)),
        ],
        out_specs=[
            pl.BlockSpec((1, tm, D), tok), pl.BlockSpec((1, tm, 1, D), lambda b, i: (b, i, 0, 0)),
            pl.BlockSpec((1, tm, LANES), tok),
        ],
        out_shape=[
            jax.ShapeDtypeStruct((B, T, D), f32),
            jax.ShapeDtypeStruct((B, T, 1, D), f32),
            jax.ShapeDtypeStruct((B, T, LANES), f32),
        ],
        compiler_params=_params("parallel", "parallel"),
        name="merge",
    )(ret, sgu, att, *([z] * len(gate_specs)), x, wb, wo, gate, scale, shift, wr)


ROUTE_TILE = 256
AFF_BITS = 31
MIN_NORMAL_BITS = 0x00800000


def _route_body(cap, aff_ref, tri_ref, pos_ref, post_ref, c0_ref, afft):
    T = aff_ref.shape[1]
    tt = tri_ref.shape[0]
    for j in range(T // tt):
        afft[:, tt * j:tt * (j + 1)] = aff_ref[0, tt * j:tt * (j + 1), :].T[:N_EXPERTS]

    def as_float(bits):
        return pltpu.bitcast(jnp.broadcast_to(bits, (N_EXPERTS, LANES)), f32)[:, :1]

    def count_at_least(v):
        return jnp.sum(jnp.where(afft[...] >= v, 1.0, 0.0), axis=1, keepdims=True)

    def refine(i, thr):
        cand = thr | jnp.left_shift(jnp.int32(1), AFF_BITS - 1 - i)
        return jnp.where(count_at_least(as_float(cand)) >= cap, cand, thr)

    thr = lax.fori_loop(0, AFF_BITS, refine, jnp.zeros((N_EXPERTS, 1), jnp.int32))
    hi_col = as_float(jnp.maximum(thr + 1, MIN_NORMAL_BITS))
    diag = (lax.broadcasted_iota(jnp.int32, (N_EXPERTS, LANES), 0)
            == lax.broadcasted_iota(jnp.int32, (N_EXPERTS, LANES), 1))

    def along_lanes(col):
        return jnp.sum(jnp.where(diag, col, 0.0), axis=0, keepdims=True)

    thr_f = along_lanes(as_float(thr))
    hi_f = along_lanes(hi_col)
    need = along_lanes(cap - count_at_least(hi_col))
    tri = tri_ref[...]
    carry_eq = jnp.zeros((1, LANES), f32)
    carry_sel = jnp.zeros((1, LANES), f32)
    for j in range(T // tt):
        a = aff_ref[0, tt * j:tt * (j + 1), :]
        eq = (a >= thr_f) & (a < hi_f)
        eqf = jnp.where(eq, 1.0, 0.0)
        eq_rank = carry_eq + jnp.dot(tri, eqf.astype(bf16), preferred_element_type=f32) - eqf
        sel = (a >= hi_f) | (eq & (eq_rank < need))
        self_ = jnp.where(sel, 1.0, 0.0)
        pos = carry_sel + jnp.dot(tri, self_.astype(bf16), preferred_element_type=f32) - self_
        pos = jnp.where(sel, pos, -1.0).astype(jnp.int32)
        pos_ref[0, tt * j:tt * (j + 1), :] = pos
        post_ref[0, j] = pos.T[:N_EXPERTS]
        c0_ref[0, j:j + 1, :] = carry_sel.astype(jnp.int32)
        carry_eq = carry_eq + jnp.sum(eqf, axis=0, keepdims=True)
        carry_sel = carry_sel + jnp.sum(self_, axis=0, keepdims=True)


def _route(aff, cap):
    B, T, _ = aff.shape
    tt = min(ROUTE_TILE, T)
    nt = T // tt
    tri = jnp.tril(jnp.ones((tt, tt), bf16))
    return pl.pallas_call(
        functools.partial(_route_body, cap),
        grid=(B,),
        in_specs=[pl.BlockSpec((1, T, LANES), lambda b: (b, 0, 0)), pl.BlockSpec((tt, tt), lambda b: (0, 0))],
        out_specs=[
            pl.BlockSpec((1, T, LANES), lambda b: (b, 0, 0)),
            pl.BlockSpec((1, nt, N_EXPERTS, tt), lambda b: (b, 0, 0, 0)),
            pl.BlockSpec((1, nt, LANES), lambda b: (b, 0, 0)),
        ],
        out_shape=[
            jax.ShapeDtypeStruct((B, T, LANES), jnp.int32),
            jax.ShapeDtypeStruct((B, nt, N_EXPERTS, tt), jnp.int32),
            jax.ShapeDtypeStruct((B, nt, LANES), jnp.int32),
        ],
        scratch_shapes=[pltpu.VMEM((N_EXPERTS, T), f32)],
        compiler_params=_params("parallel"),
        name="route",
    )(aff, tri)


def _slots_body(nt, cap, c0_ref, post_ref, idx_ref):
    b = pl.program_id(0)
    tt = post_ref.shape[-1]
    tok = lax.broadcasted_iota(jnp.int32, (LANES, tt), 1)
    diag = lax.broadcasted_iota(jnp.int32, (LANES, LANES), 0) == lax.broadcasted_iota(jnp.int32, (LANES, LANES), 1)

    def per_expert(e, carry):
        def tiles_starting_at_or_before(slot):
            return sum((c0_ref[(b * nt + j) * N_EXPERTS + e] <= slot).astype(jnp.int32) for j in range(nt))

        for k in range(idx_ref.shape[2]):
            slot = LANES * k + lax.broadcasted_iota(jnp.int32, (LANES, 1), 0)

            def match(j, acc):
                return acc + jnp.where(post_ref[0, j, pl.ds(e, 1), :] == slot, tok + j * tt, 0)

            j_lo = tiles_starting_at_or_before(LANES * k) - 1
            j_hi = tiles_starting_at_or_before(min(LANES * (k + 1), cap) - 1)
            acc = lax.fori_loop(j_lo, j_hi, match, jnp.zeros((LANES, tt), jnp.int32))
            col = jnp.sum(acc.astype(f32), axis=1, keepdims=True)
            row = jnp.sum(jnp.where(diag, col, 0.0), axis=0, keepdims=True)
            idx_ref[0, e, k:k + 1, :] = row.astype(jnp.int32)
        return carry

    lax.fori_loop(0, N_EXPERTS, per_expert, 0)


def _slots(c0s, post, cap):
    B, nt, E, tt = post.shape
    nblk = -(-cap // LANES)
    idx = pl.pallas_call(
        functools.partial(_slots_body, nt, cap),
        grid_spec=pltpu.PrefetchScalarGridSpec(
            num_scalar_prefetch=1,
            grid=(B,),
            in_specs=[pl.BlockSpec((1, nt, E, tt), lambda b, c0: (b, 0, 0, 0))],
            out_specs=pl.BlockSpec((1, E, nblk, LANES), lambda b, c0: (b, 0, 0, 0)),
        ),
        out_shape=jax.ShapeDtypeStruct((B, E, nblk, LANES), jnp.int32),
        compiler_params=_params("arbitrary"),
        name="slots",
    )(c0s, post)
    return idx.reshape(B, E, nblk * LANES)[:, :, :cap]


EXPERT_COPY_BATCHES = 3
EXPERT_MIN_ROWS = 256


def _expert_body(cap, spb, T, idx_ref, h_ref, wg_ref, wu_ref, wd_ref, y_ref, rows, xs_s, act_s, hm_s, wg_s, wu_s, wd_s,
                 sem):
    n_b = pl.num_programs(1)
    step = pl.program_id(0) * n_b + pl.program_id(1)
    last = pl.num_programs(0) * n_b - 1
    n_rows = spb * cap
    bounds = [n_rows * k // EXPERT_COPY_BATCHES for k in range(EXPERT_COPY_BATCHES + 1)]

    def start_rows(s, buf, lo, hi):
        e, g = s // n_b, s % n_b
        for r in range(lo, hi):
            b = g * spb + r // cap
            t = idx_ref[(b * N_EXPERTS + e) * cap + r % cap]
            pltpu.make_async_copy(h_ref.at[b * T + t], rows.at[buf, pl.ds(r, 1), :], sem.at[buf]).start()

    def wait_rows(buf):
        pltpu.make_async_copy(h_ref.at[pl.ds(0, n_rows), 0, :], rows.at[buf], sem.at[buf]).wait()

    @pl.when(step == 0)
    def _():
        start_rows(step, 0, 0, n_rows)

    @pl.when(pl.program_id(1) == 0)
    def _():
        wg_s[...] = wg_ref[0, 0].astype(bf16)
        wu_s[...] = wu_ref[0, 0].astype(bf16)
        wd_s[...] = wd_ref[0, 0].astype(bf16)

    for cur in range(2):
        @pl.when(step % 2 == cur)
        def _():
            wait_rows(cur)
            xs_s[...] = rows[cur].astype(bf16)

            def prefetch(k):
                @pl.when(step < last)
                def _():
                    start_rows(step + 1, 1 - cur, bounds[k], bounds[k + 1])

            prefetch(0)
            act_s[...] = jax.nn.silu(jnp.dot(xs_s[...], wg_s[...], preferred_element_type=f32))
            prefetch(1)
            hm_s[...] = (act_s[...] * jnp.dot(xs_s[...], wu_s[...], preferred_element_type=f32)).astype(bf16)
            prefetch(2)
            y = jnp.dot(hm_s[...], wd_s[...], preferred_element_type=f32).astype(y_ref.dtype)
            for i in range(spb):
                y_ref[i, 0] = y[cap * i:cap * (i + 1)]


def _expert_ffn(idx, h2, layer, wg, wu, wd):
    B, T, _, D = h2.shape
    _, E, _, F = wg.shape
    cap = idx.shape[-1]
    spb = _pick_tile(B, max(1, EXPERT_MIN_ROWS // cap))
    n_rows = spb * cap
    wsp = lambda e, g, idx: (layer, e, 0, 0)
    return pl.pallas_call(
        functools.partial(_expert_body, cap, spb, T),
        grid_spec=pltpu.PrefetchScalarGridSpec(
            num_scalar_prefetch=1,
            grid=(E, B // spb),
            in_specs=[
                pl.BlockSpec(memory_space=pl.ANY),
                pl.BlockSpec((1, 1, D, F), wsp), pl.BlockSpec((1, 1, D, F), wsp), pl.BlockSpec((1, 1, F, D), wsp),
            ],
            out_specs=pl.BlockSpec((spb, 1, cap, D), lambda e, g, idx: (g, e, 0, 0)),
            scratch_shapes=[pltpu.VMEM((2, n_rows, D), f32), pltpu.VMEM((n_rows, D), bf16),
                            pltpu.VMEM((n_rows, F), f32), pltpu.VMEM((n_rows, F), bf16),
                            pltpu.VMEM((D, F), bf16), pltpu.VMEM((D, F), bf16), pltpu.VMEM((F, D), bf16),
                            pltpu.SemaphoreType.DMA((2,))],
        ),
        out_shape=jax.ShapeDtypeStruct((B, E, cap, D), bf16),
        compiler_params=pltpu.CompilerParams(dimension_semantics=("arbitrary", "arbitrary"),
                                             vmem_limit_bytes=EXPERT_VMEM_LIMIT_BYTES),
        name="expert_ffn",
    )(idx.reshape(-1), h2.reshape(B * T, 1, D), wg, wu, wd)


Y_ROW_ALIGN = 16


def _combine_body(nt, cap, win, c0_ref, pos_ref, aff_ref, x_ref, gate_ref, spread_ref, y_ref, o_ref, buf, extra,
                  sem, sem_x):
    b = pl.program_id(0)
    j = pl.program_id(1)
    step = b * nt + j
    tt = pos_ref.shape[1]
    stride = win if win == cap else win - Y_ROW_ALIGN

    def c0(bb, jj, e):
        return c0_ref[(bb * nt + jj) * N_EXPERTS + e]

    def window_start(lo):
        return pl.multiple_of(jnp.minimum(lo & (-Y_ROW_ALIGN), cap - win), Y_ROW_ALIGN)

    def copies(bb, jj, p, dst, s):
        return [pltpu.make_async_copy(
            y_ref.at[bb, e, pl.ds(window_start(c0(bb, jj, e) + p * stride), win), :],
            dst.at[pl.ds(e * win, win), :], s) for e in range(N_EXPERTS)]

    @pl.when(step == 0)
    def _():
        for cp in copies(b, j, 0, buf.at[0], sem.at[0]):
            cp.start()

    nxt = step + 1

    @pl.when(nxt < pl.num_programs(0) * nt)
    def _():
        for cp in copies(nxt // nt, nxt % nt, 0, buf.at[nxt % 2], sem.at[nxt % 2]):
            cp.start()

    lane = _lane((1, LANES))
    pos = pos_ref[0]
    aff = aff_ref[0]

    def lane_vector(fn):
        v = jnp.zeros((1, LANES), jnp.int32)
        for e in range(N_EXPERTS):
            v = jnp.where(lane == e, fn(e), v)
        return v

    spread = spread_ref[...]
    col_slot = _lane((1, N_EXPERTS * win)) & (win - 1)
    aff_cols = jnp.dot(aff.astype(bf16), spread, preferred_element_type=f32)

    def weights(p):
        lo = lane_vector(lambda e: c0(b, j, e) + p * stride)
        start = lane_vector(lambda e: window_start(c0(b, j, e) + p * stride))
        rel = jnp.where((pos >= lo) & (pos < lo + stride), pos - start, -1).astype(f32)
        rel_cols = jnp.dot(rel.astype(bf16), spread, preferred_element_type=f32)
        return jnp.where(rel_cols == col_slot.astype(f32), aff_cols, 0.0).astype(bf16)

    for cp in copies(b, j, 0, buf.at[step % 2], sem.at[step % 2]):
        cp.wait()
    moe = jnp.dot(weights(0), buf[step % 2], preferred_element_type=f32)

    def count(e):
        end = jnp.where(j + 1 < nt, c0(b, jnp.minimum(j + 1, nt - 1), e), cap)
        return end - c0(b, j, e)

    max_count = functools.reduce(jnp.maximum, [count(e) for e in range(N_EXPERTS)])

    def more(p, acc):
        cps = copies(b, j, p, extra, sem_x.at[0])
        for cp in cps:
            cp.start()
        for cp in cps:
            cp.wait()
        return acc + jnp.dot(weights(p), extra[...], preferred_element_type=f32)

    moe = lax.fori_loop(1, (max_count + stride - 1) // stride, more, moe)
    o_ref[0] = x_ref[0] + gate_ref[0] * moe


def _combine(c0s, pos, aff, x, gate, y):
    B, T, D = x.shape
    E, cap = y.shape[1], y.shape[2]
    tt = min(ROUTE_TILE, T)
    nt = T // tt
    win = min(cap, LANES // 2)
    assert (cap - win) % Y_ROW_ALIGN == 0 and win & (win - 1) == 0
    spread = (jnp.arange(LANES)[:, None] == jnp.arange(E * win)[None, :] // win).astype(bf16)
    tok = lambda b, j, c0: (b, j, 0)
    return pl.pallas_call(
        functools.partial(_combine_body, nt, cap, win),
        grid_spec=pltpu.PrefetchScalarGridSpec(
            num_scalar_prefetch=1,
            grid=(B, nt),
            in_specs=[
                pl.BlockSpec((1, tt, LANES), tok), pl.BlockSpec((1, tt, LANES), tok), pl.BlockSpec((1, tt, D), tok),
                pl.BlockSpec((1, 1, D), lambda b, j, c0: (b, 0, 0)),
                pl.BlockSpec((LANES, E * win), lambda b, j, c0: (0, 0)),
                pl.BlockSpec(memory_space=pl.ANY),
            ],
            out_specs=pl.BlockSpec((1, tt, D), tok),
            scratch_shapes=[pltpu.VMEM((2, E * win, D), bf16), pltpu.VMEM((E * win, D), bf16),
                            pltpu.SemaphoreType.DMA((2,)), pltpu.SemaphoreType.DMA((1,))],
        ),
        out_shape=jax.ShapeDtypeStruct((B, T, D), f32),
        compiler_params=_params("arbitrary", "arbitrary"),
        name="combine",
    )(c0s, pos, aff, x, gate, spread, y)


def _expert_choice(h2, aff, x, gate2, p):
    T = x.shape[1]
    cap = CAPACITY_FACTOR * T // N_EXPERTS
    pos, post, c0 = _route(aff, cap)
    c0s = c0[:, :, :N_EXPERTS].reshape(-1)
    y = _expert_ffn(_slots(c0s, post, cap), h2, p['layer'], p['w_exp_gate'], p['w_exp_up'], p['w_exp_down'])
    return _combine(c0s, pos, aff, x, gate2, y)


def _rope_tables(T):
    rows = T // GRID_W
    row = jnp.repeat(jnp.arange(rows), GRID_W).astype(f32)
    col = jnp.tile(jnp.arange(GRID_W), rows).astype(f32)
    n_freq = HEAD_DIM // 4
    inv = jnp.power(ROPE_BASE, -jnp.arange(n_freq, dtype=f32) / n_freq)
    ang = jnp.concatenate([row[:, None] * inv, col[:, None] * inv], axis=-1)
    cos, sin = jnp.cos(ang), jnp.sin(ang)
    reps = LANES // HEAD_DIM
    return jnp.tile(jnp.concatenate([cos, cos], axis=-1), (1, reps)), jnp.tile(
        jnp.concatenate([-sin, sin], axis=-1), (1, reps))


def _identity_rope(T):
    return jnp.ones((T, LANES), f32), jnp.zeros((T, LANES), f32)


def _retention_tables(decay_fwd, decay_bwd):
    lgf = jax.nn.log_sigmoid(decay_fwd.astype(f32))
    lgb = jax.nn.log_sigmoid(decay_bwd.astype(f32))
    C = CHUNK
    k_scale = HEAD_DIM ** -0.5
    idx = jnp.arange(C, dtype=f32)
    npair = RET_HEADS // 2

    def lanes(t):
        return jnp.repeat(t.reshape(npair, 2, C).transpose(0, 2, 1), HEAD_DIM, axis=-1)

    diff = idx[:, None] - idx[None, :]
    dm = jnp.where(diff >= 0, jnp.exp(jnp.maximum(diff, 0.0)[None] * lgf[:, None, None]),
                   jnp.exp(jnp.maximum(-diff, 0.0)[None] * lgb[:, None, None])) * k_scale

    def chunk_decay(lg):
        rows = jnp.repeat(jnp.exp(C * lg).reshape(npair, 2), HEAD_DIM, axis=-1)
        return jnp.broadcast_to(rows[:, :, None], (npair, LANES, 2 * LANES))

    r = jnp.arange(LANES)[:, None] // HEAD_DIM
    c = jnp.arange(2 * LANES)[None, :] // LANES
    return {
        'wkf': lanes(jnp.exp((C - 1 - idx)[None, :] * lgf[:, None])) * k_scale,
        'wkb': lanes(jnp.exp(idx[None, :] * lgb[:, None])) * k_scale,
        'qdf': lanes(jnp.exp((idx + 1.0)[None, :] * lgf[:, None])),
        'qdb': lanes(jnp.exp((C - idx)[None, :] * lgb[:, None])),
        'dm': dm, 'cdf': chunk_decay(lgf), 'cdb': chunk_decay(lgb), 'bm': (r == c).astype(f32),
    }


def _attention_tables(cos, sin, q_gain, k_gain):
    reps = LANES // HEAD_DIM
    half = HEAD_DIM // 2
    swap = lambda g: jnp.concatenate([g[half:], g[:half]])
    gq, gk = jnp.tile(q_gain, reps)[None], jnp.tile(k_gain, reps)[None]
    gqs, gks = jnp.tile(swap(q_gain), reps)[None], jnp.tile(swap(k_gain), reps)[None]
    r = jnp.arange(LANES) // HEAD_DIM
    return {'qa': cos * gq, 'qb': sin * gqs, 'ka': cos * gk, 'kb': sin * gks, 'gq': gq, 'gk': gk,
            'ones': (r[:, None] == r[None, :]).astype(bf16)}


def _layer(x, xc, mod_lat, mod_ctx, rope, p, last):
    B, S, D = x.shape
    L = xc.shape[1]
    lay = _layout(D)
    sh1, sc1, g1, sh2, sc2, g2 = jnp.split(mod_lat, 6, axis=-1)
    csh1, csc1, cg1, csh2, csc2, cg2 = jnp.split(mod_ctx, 6, axis=-1)
    n1 = p['norm1_g'][None, None, :]
    n2 = p['norm2_g'][None, None, :]

    def bcast(t):
        return jnp.broadcast_to(t, (B, 1, D))

    z = _proj_in(x, n1 * (1 + sc1), sh1, p['w_in'])
    zc = _proj_in(xc, bcast(n1 * (1 + csc1)), bcast(csh1), p['w_in'])

    cos, sin = rope
    cos_c, sin_c = _identity_rope(L)
    rtabs = _retention_tables(p['ret_decay_fwd'], p['ret_decay_bwd'])
    atabs = _attention_tables(cos, sin, p['q_norm_g'], p['k_norm_g'])
    sgu_w = p['sgu_w'].astype(bf16)
    gw = D // 2 // SGU_GROUPS
    sgu_bias = jnp.repeat(p['sgu_b'].T, gw, axis=-1)

    zero = jnp.zeros((B, RET_HEADS // 2, LANES, 2 * LANES), f32)
    sfc, sbc, s_f, s_b = _ret_states(zc, lay, cos_c, sin_c, rtabs, zero, zero)
    sf, sb, _, _ = _ret_states(z, lay, cos, sin, rtabs, s_f, s_b)
    ret, sgu = _ret_sgu(z, lay, cos, sin, sf, sb, rtabs, sgu_w, sgu_bias)
    att = _attention(z, zc, lay, atabs, p['attn_sink'], True)

    x, h2, aff = _merge(ret, sgu, att, z, lay['gates'][0], x, p['w_branch'], p['w_out'], g1, n2 * (1 + sc2), sh2,
                        p['w_router'])
    x = _expert_choice(h2, aff, x, g2, p)
    if last:
        return x, None

    ret_c, sgu_c = _ret_sgu(zc, lay, cos_c, sin_c, sfc, sbc, rtabs, sgu_w, sgu_bias)
    att_c = _attention(zc, zc, lay, atabs, p['attn_sink'], False)
    xc, h2c, affc = _merge(ret_c, sgu_c, att_c, zc, lay['gates'][0], xc, p['w_branch'], p['w_out'], bcast(cg1),
                           bcast(n2 * (1 + csc2)), bcast(csh2), p['w_router'])
    xc = _expert_choice(h2c, affc, xc, bcast(cg2), p)
    return x, xc


def kernel(x, c, ctx, c_ctx, w_mod, b_mod, norm1_g, norm2_g, w_in, ret_decay_fwd, ret_decay_bwd, sgu_w, sgu_b,
           q_norm_g, k_norm_g, attn_sink, w_branch, w_out, w_router, w_exp_gate, w_exp_up, w_exp_down):
    B, S, D = x.shape
    depth = w_in.shape[0]
    rope = _rope_tables(S)
    c_act = jax.nn.silu(c)
    c_ctx_act = jax.nn.silu(c_ctx)
    xc = ctx
    hp = lax.Precision.HIGHEST
    for l in range(depth):
        wr = jnp.zeros((D, LANES), f32).at[:, :N_EXPERTS].set(w_router[l]).astype(bf16)
        p = {
            'norm1_g': norm1_g[l], 'norm2_g': norm2_g[l], 'w_in': w_in[l].astype(bf16),
            'ret_decay_fwd': ret_decay_fwd[l], 'ret_decay_bwd': ret_decay_bwd[l],
            'sgu_w': sgu_w[l], 'sgu_b': sgu_b[l],
            'q_norm_g': q_norm_g[l], 'k_norm_g': k_norm_g[l], 'attn_sink': attn_sink[l],
            'w_branch': w_branch[l].astype(bf16), 'w_out': w_out[l].astype(bf16), 'w_router': wr,
            'layer': l, 'w_exp_gate': w_exp_gate, 'w_exp_up': w_exp_up, 'w_exp_down': w_exp_down,
        }
        mod_lat = (jnp.dot(c_act, w_mod[l], precision=hp) + b_mod[l])[:, None, :]
        mod_ctx = (jnp.dot(c_ctx_act, w_mod[l], precision=hp) + b_mod[l])[None, None, :]
        x, xc = _layer(x, xc, mod_lat, mod_ctx, rope, p, l == depth - 1)
    return x
```

```python
import functools

import jax
import jax.numpy as jnp
from jax import lax
from jax.experimental import pallas as pl
from jax.experimental.pallas import tpu as pltpu

GRID_W = 64
N_BRANCHES = 3
RET_HEADS = 4
SGU_GROUPS = 4
CHUNK = 128
HEAD_DIM = 64
ATT_KV_HEADS = 2
ROPE_BASE = 10000.0
N_EXPERTS = 16
CAPACITY_FACTOR = 2
EPS = 1e-6
NEG_INF = -1e30

LANES = 128
VMEM_LIMIT_BYTES = 48 * 1024 * 1024
EXPERT_VMEM_LIMIT_BYTES = 56 * 1024 * 1024

f32 = jnp.float32
bf16 = jnp.bfloat16


def _layout(d_model):
    bw = d_model // 2
    qk = RET_HEADS * HEAD_DIM
    kv = ATT_KV_HEADS * HEAD_DIM
    widths = (('ret_q', qk), ('ret_k', qk), ('ret_v', bw), ('ret_g', bw), ('sgu_u', bw), ('sgu_v', bw),
              ('att_q', bw), ('att_k', kv), ('att_v', kv), ('gates', N_BRANCHES * d_model))
    out, start = {}, 0
    for name, w in widths:
        out[name] = (start, w)
        start += w
    return out


def _chunks_per_step(T):
    return next(c for c in (4, 2, 1) if (T // CHUNK) % c == 0)


def _pick_tile(n, pref):
    t = min(n, pref)
    while n % t:
        t //= 2
    return t


def _params(*sem):
    return pltpu.CompilerParams(dimension_semantics=sem, vmem_limit_bytes=VMEM_LIMIT_BYTES)


def _proj_in_body(x_ref, scale_ref, shift_ref, w_ref, o_ref, h_ref):
    @pl.when(pl.program_id(2) == 0)
    def _():
        x = x_ref[0]
        y = x * lax.rsqrt(jnp.mean(x * x, axis=-1, keepdims=True) + EPS)
        h_ref[...] = (y * scale_ref[0] + shift_ref[0]).astype(bf16)

    o_ref[0] = jnp.dot(h_ref[...], w_ref[...], preferred_element_type=f32).astype(o_ref.dtype)


def _proj_in(x, scale, shift, w):
    B, T, D = x.shape
    N = w.shape[1]
    tm = _pick_tile(T, 2048)
    tn = 1280 if N % 1280 == 0 else _pick_tile(N, 1024)
    return pl.pallas_call(
        _proj_in_body,
        grid=(B, T // tm, N // tn),
        in_specs=[
            pl.BlockSpec((1, tm, D), lambda b, i, j: (b, i, 0)),
            pl.BlockSpec((1, 1, D), lambda b, i, j: (b, 0, 0)),
            pl.BlockSpec((1, 1, D), lambda b, i, j: (b, 0, 0)),
            pl.BlockSpec((D, tn), lambda b, i, j: (0, j)),
        ],
        out_specs=pl.BlockSpec((1, tm, tn), lambda b, i, j: (b, i, j)),
        out_shape=jax.ShapeDtypeStruct((B, T, N), bf16),
        scratch_shapes=[pltpu.VMEM((tm, D), bf16)],
        compiler_params=_params("parallel", "parallel", "arbitrary"),
        name="proj_in",
    )(x, scale, shift, w)


def _lane(shape):
    return lax.broadcasted_iota(jnp.int32, shape, 1)


def _swap_halves(x):
    return jnp.where((_lane(x.shape) & (HEAD_DIM // 2)) == 0,
                     pltpu.roll(x, LANES - HEAD_DIM // 2, 1), pltpu.roll(x, HEAD_DIM // 2, 1))


def _rope(x, cos, sin_signed):
    return x * cos + _swap_halves(x) * sin_signed


def _head_sumsq(x, ones_blockdiag):
    x2 = x * x
    hi = x2.astype(bf16)
    lo = (x2 - hi.astype(f32)).astype(bf16)
    return (jnp.dot(hi, ones_blockdiag, preferred_element_type=f32)
            + jnp.dot(lo, ones_blockdiag, preferred_element_type=f32))


def _dot_nt(a, b):
    return lax.dot_general(a, b, (((1,), (1,)), ((), ())), preferred_element_type=f32)


def _dot_tn(a, b):
    return lax.dot_general(a, b, (((0,), (0,)), ((), ())), preferred_element_type=f32)


def _ret_state_body(cpb, kf_ref, vf_ref, kb_ref, vb_ref, cosf_ref, sinf_ref, cosb_ref, sinb_ref,
                    wkf_ref, wkb_ref, cdf_ref, cdb_ref, bm_ref, s0f_ref, s0b_ref,
                    sf_ref, sb_ref, ff_ref, fb_ref, accf, accb):
    n = pl.program_id(1)

    @pl.when(n == 0)
    def _():
        accf[...] = s0f_ref[0]
        accb[...] = s0b_ref[0]

    def one_chunk(c, k_ref, v_ref, cos_ref, sin_ref, wk_ref, cd_ref, out_ref, acc):
        rows = slice(CHUNK * c, CHUNK * (c + 1))
        k = k_ref[0, rows, :].astype(f32)
        v = v_ref[0, rows, :]
        for p in range(RET_HEADS // 2):
            kp = _rope(k[:, LANES * p:LANES * (p + 1)], cos_ref[rows, :], sin_ref[rows, :])
            kw = (kp * wk_ref[p]).astype(bf16)
            upd = _dot_tn(kw, v[:, 2 * LANES * p:2 * LANES * (p + 1)])
            s = acc[p]
            out_ref[0, c, p] = s.astype(out_ref.dtype)
            acc[p] = cd_ref[p] * s + bm_ref[...] * upd

    for c in range(cpb):
        one_chunk(c, kf_ref, vf_ref, cosf_ref, sinf_ref, wkf_ref, cdf_ref, sf_ref, accf)
        one_chunk(cpb - 1 - c, kb_ref, vb_ref, cosb_ref, sinb_ref, wkb_ref, cdb_ref, sb_ref, accb)

    @pl.when(n == pl.num_programs(1) - 1)
    def _():
        ff_ref[0] = accf[...]
        fb_ref[0] = accb[...]


def _ret_states(z, lay, cos, sin, tabs, s0f, s0b):
    B, T, _ = z.shape
    cpb = _chunks_per_step(T)
    rows = cpb * CHUNK
    nc = T // rows
    kq = lay['ret_k'][0] // (RET_HEADS * HEAD_DIM)
    vq = lay['ret_v'][0] // lay['ret_v'][1]
    vw = lay['ret_v'][1]
    kw = RET_HEADS * HEAD_DIM
    npair = RET_HEADS // 2
    st_shape = (npair, LANES, 2 * LANES)
    const3 = lambda b, n: (0, 0, 0)
    return pl.pallas_call(
        functools.partial(_ret_state_body, cpb),
        grid=(B, nc),
        in_specs=[
            pl.BlockSpec((1, rows, kw), lambda b, n: (b, n, kq)),
            pl.BlockSpec((1, rows, vw), lambda b, n: (b, n, vq)),
            pl.BlockSpec((1, rows, kw), lambda b, n: (b, nc - 1 - n, kq)),
            pl.BlockSpec((1, rows, vw), lambda b, n: (b, nc - 1 - n, vq)),
            pl.BlockSpec((rows, LANES), lambda b, n: (n, 0)),
            pl.BlockSpec((rows, LANES), lambda b, n: (n, 0)),
            pl.BlockSpec((rows, LANES), lambda b, n: (nc - 1 - n, 0)),
            pl.BlockSpec((rows, LANES), lambda b, n: (nc - 1 - n, 0)),
            pl.BlockSpec((npair, CHUNK, LANES), const3),
            pl.BlockSpec((npair, CHUNK, LANES), const3),
            pl.BlockSpec(st_shape, const3),
            pl.BlockSpec(st_shape, const3),
            pl.BlockSpec((LANES, 2 * LANES), lambda b, n: (0, 0)),
            pl.BlockSpec((1,) + st_shape, lambda b, n: (b, 0, 0, 0)),
            pl.BlockSpec((1,) + st_shape, lambda b, n: (b, 0, 0, 0)),
        ],
        out_specs=[
            pl.BlockSpec((1, cpb) + st_shape, lambda b, n: (b, n, 0, 0, 0)),
            pl.BlockSpec((1, cpb) + st_shape, lambda b, n: (b, nc - 1 - n, 0, 0, 0)),
            pl.BlockSpec((1,) + st_shape, lambda b, n: (b, 0, 0, 0)),
            pl.BlockSpec((1,) + st_shape, lambda b, n: (b, 0, 0, 0)),
        ],
        out_shape=[
            jax.ShapeDtypeStruct((B, nc * cpb) + st_shape, bf16),
            jax.ShapeDtypeStruct((B, nc * cpb) + st_shape, bf16),
            jax.ShapeDtypeStruct((B,) + st_shape, f32),
            jax.ShapeDtypeStruct((B,) + st_shape, f32),
        ],
        scratch_shapes=[pltpu.VMEM(st_shape, f32), pltpu.VMEM(st_shape, f32)],
        compiler_params=_params("parallel", "arbitrary"),
        name="ret_states",
    )(z, z, z, z, cos, sin, cos, sin, tabs['wkf'], tabs['wkb'], tabs['cdf'], tabs['cdb'], tabs['bm'], s0f, s0b)


def _ret_sgu_body(cpb, *refs):
    for c in range(cpb):
        _ret_sgu_chunk(c, *refs)


def _ret_sgu_chunk(c, q_ref, k_ref, v_ref, g_ref, u_ref, w_ref, cos_ref, sin_ref, sf_ref, sb_ref,
                   dm_ref, qdf_ref, qdb_ref, sw_ref, sbias_ref, ret_ref, sgu_ref):
    rows = slice(CHUNK * c, CHUNK * (c + 1))
    q = q_ref[0, rows, :].astype(f32)
    k = k_ref[0, rows, :].astype(f32)
    v = v_ref[0, rows, :]
    g = g_ref[0, rows, :].astype(f32)
    cos = cos_ref[rows, :]
    sin = sin_ref[rows, :]
    first_half = _lane((CHUNK, LANES)) < HEAD_DIM
    for p in range(RET_HEADS // 2):
        qp = _rope(q[:, LANES * p:LANES * (p + 1)], cos, sin)
        kp = _rope(k[:, LANES * p:LANES * (p + 1)], cos, sin).astype(bf16)
        cross = (jnp.dot((qp * qdf_ref[p]).astype(bf16), sf_ref[0, c, p], preferred_element_type=f32)
                 + jnp.dot((qp * qdb_ref[p]).astype(bf16), sb_ref[0, c, p], preferred_element_type=f32))
        for a in range(2):
            h = 2 * p + a
            qm = jnp.where(first_half if a == 0 else ~first_half, qp, 0.0).astype(bf16)
            sc = (_dot_nt(qm, kp) * dm_ref[h]).astype(bf16)
            o = jnp.dot(sc, v[:, LANES * h:LANES * (h + 1)], preferred_element_type=f32)
            o = o + cross[:, LANES * a:LANES * (a + 1)]
            oc = o - jnp.mean(o, axis=-1, keepdims=True)
            y = oc * lax.rsqrt(jnp.mean(oc * oc, axis=-1, keepdims=True) + EPS)
            ret_ref[0, rows, LANES * h:LANES * (h + 1)] = (
                y * jax.nn.silu(g[:, LANES * h:LANES * (h + 1)])).astype(ret_ref.dtype)

    u = jax.nn.gelu(u_ref[0, rows, :].astype(f32))
    w = jax.nn.gelu(w_ref[0, rows, :].astype(f32))
    wc = w - jnp.mean(w, axis=-1, keepdims=True)
    wn = (wc * lax.rsqrt(jnp.mean(wc * wc, axis=-1, keepdims=True) + EPS)).astype(bf16)
    gw = wn.shape[-1] // SGU_GROUPS
    for i in range(SGU_GROUPS):
        sl = slice(gw * i, gw * (i + 1))
        mixed = jnp.dot(sw_ref[i], wn[:, sl], preferred_element_type=f32) + sbias_ref[:, sl]
        sgu_ref[0, rows, sl] = (u[:, sl] * mixed).astype(sgu_ref.dtype)


def _ret_sgu(z, lay, cos, sin, sf, sb, tabs, sgu_w, sgu_bias):
    B, T, _ = z.shape
    cpb = _chunks_per_step(T)
    rows = cpb * CHUNK
    nc = T // rows
    bw = lay['ret_v'][1]

    def zs(name):
        s, w = lay[name]
        return pl.BlockSpec((1, rows, w), lambda b, n, j=s // w: (b, n, j))

    st_shape = (RET_HEADS // 2, LANES, 2 * LANES)
    const3 = lambda b, n: (0, 0, 0)
    out_spec = pl.BlockSpec((1, rows, bw), lambda b, n: (b, n, 0))
    return pl.pallas_call(
        functools.partial(_ret_sgu_body, cpb),
        grid=(B, nc),
        in_specs=[
            zs('ret_q'), zs('ret_k'), zs('ret_v'), zs('ret_g'), zs('sgu_u'), zs('sgu_v'),
            pl.BlockSpec((rows, LANES), lambda b, n: (n, 0)),
            pl.BlockSpec((rows, LANES), lambda b, n: (n, 0)),
            pl.BlockSpec((1, cpb) + st_shape, lambda b, n: (b, n, 0, 0, 0)),
            pl.BlockSpec((1, cpb) + st_shape, lambda b, n: (b, n, 0, 0, 0)),
            pl.BlockSpec((RET_HEADS, CHUNK, CHUNK), const3),
            pl.BlockSpec((RET_HEADS // 2, CHUNK, LANES), const3),
            pl.BlockSpec((RET_HEADS // 2, CHUNK, LANES), const3),
            pl.BlockSpec((SGU_GROUPS, CHUNK, CHUNK), const3),
            pl.BlockSpec((CHUNK, bw), lambda b, n: (0, 0)),
        ],
        out_specs=[out_spec, out_spec],
        out_shape=[jax.ShapeDtypeStruct((B, T, bw), bf16), jax.ShapeDtypeStruct((B, T, bw), bf16)],
        compiler_params=_params("parallel", "parallel"),
        name="ret_sgu",
    )(z, z, z, z, z, z, cos, sin, sf, sb, tabs['dm'], tabs['qdf'], tabs['qdb'], sgu_w, sgu_bias)


def _att_body(local, qb, sink_ref, q_ref, *refs):
    if local:
        (kp_ref, kc_ref, kn_ref, vp_ref, vc_ref, vn_ref, qa_ref, qb_ref,
         kap_ref, kbp_ref, kac_ref, kbc_ref, kan_ref, kbn_ref, ck_ref, cv_ref, gk_ref, ones_ref, o_ref) = refs
    else:
        qa_ref, ck_ref, cv_ref, gk_ref, ones_ref, o_ref = refs
    n = pl.program_id(1)
    n_blocks = pl.num_programs(1) * qb
    ones = ones_ref[...]
    inv_hd = 1.0 / HEAD_DIM

    def norm_rope(x_ref_val, ta, tb):
        x = x_ref_val.astype(f32)
        r = lax.rsqrt(_head_sumsq(x, ones) * inv_hd + EPS)
        y = x * ta if tb is None else x * ta + _swap_halves(x) * tb
        return y * r

    def dup(x):
        lo = _lane(x.shape) < HEAD_DIM
        xr = pltpu.roll(x, HEAD_DIM, 1)
        return jnp.where(lo, x, xr).astype(bf16), jnp.where(lo, xr, x).astype(bf16)

    def blk(i):
        return slice(CHUNK * i, CHUNK * (i + 1))

    k_ctx = dup(norm_rope(ck_ref[0], gk_ref[...], None))
    v_ctx = dup(cv_ref[0].astype(f32))
    if local:
        k_loc = ([dup(norm_rope(kp_ref[0], kap_ref[...], kbp_ref[...]))]
                 + [dup(norm_rope(kc_ref[0, blk(i), :], kac_ref[blk(i), :], kbc_ref[blk(i), :])) for i in range(qb)]
                 + [dup(norm_rope(kn_ref[0], kan_ref[...], kbn_ref[...]))])
        v_loc = ([dup(vp_ref[0].astype(f32))] + [dup(vc_ref[0, blk(i), :].astype(f32)) for i in range(qb)]
                 + [dup(vn_ref[0].astype(f32))])
    n_loc = 3 * CHUNK if local else 0

    n_groups = q_ref.shape[-1] // LANES
    groups_per_kv = n_groups // ATT_KV_HEADS
    rows = 2 * groups_per_kv * CHUNK
    log2e = 1.4426950408889634
    first_half = _lane((CHUNK, LANES)) < HEAD_DIM
    for qi in range(qb):
        if local:
            i = lax.broadcasted_iota(jnp.int32, (CHUNK, n_loc), 0)
            j = lax.broadcasted_iota(jnp.int32, (CHUNK, n_loc), 1)
            has_prev = (n * qb + qi > 0).astype(jnp.int32)
            has_next = (n * qb + qi < n_blocks - 1).astype(jnp.int32)
            valid = (j >= i * has_prev + CHUNK * (1 - has_prev)) & (j <= 2 * CHUNK - 1 + has_next * (i + 1))
            bias = jnp.where(valid, 0.0, NEG_INF)
            bias = jnp.concatenate([bias] * (rows // CHUNK), axis=0)
        for h in range(ATT_KV_HEADS):
            keys = ([k_loc[qi + d][h] for d in range(3)] if local else []) + [k_ctx[h]]
            vals = ([v_loc[qi + d][h] for d in range(3)] if local else []) + [v_ctx[h]]
            keys = jnp.concatenate(keys, axis=0) if local else keys[0]
            vals = jnp.concatenate(vals, axis=0) if local else vals[0]
            q_rows, sink_rows = [], []
            for c in range(groups_per_kv * h, groups_per_kv * (h + 1)):
                xq = q_ref[0, blk(qi), LANES * c:LANES * (c + 1)]
                qn = norm_rope(xq, qa_ref[blk(qi), :] if local else qa_ref[...],
                               qb_ref[blk(qi), :] if local else None) * (HEAD_DIM ** -0.5 * log2e)
                for a in range(2):
                    q_rows.append(jnp.where(first_half if a == 0 else ~first_half, qn, 0.0).astype(bf16))
                    sink_rows.append(jnp.full((CHUNK, 1), sink_ref[2 * c + a] * log2e, f32))
            s = _dot_nt(jnp.concatenate(q_rows, axis=0), keys)
            sk = jnp.concatenate(sink_rows, axis=0)
            s_ctx = s[:, n_loc:]
            m = jnp.maximum(jnp.max(s_ctx, axis=-1, keepdims=True), sk)
            if local:
                s_loc = s[:, :n_loc] + bias
                m = jnp.maximum(m, jnp.max(s_loc, axis=-1, keepdims=True))
            p_ctx = jnp.exp2(s_ctx - m)
            d = jnp.sum(p_ctx, axis=-1, keepdims=True) + jnp.exp2(sk - m)
            o = jnp.dot(p_ctx.astype(bf16), vals[n_loc:], preferred_element_type=f32)
            if local:
                p_loc = jnp.exp2(s_loc - m)
                d = d + jnp.sum(p_loc, axis=-1, keepdims=True)
                o = o + jnp.dot(p_loc.astype(bf16), vals[:n_loc], preferred_element_type=f32)
            o = o * (1.0 / d)
            for ci in range(groups_per_kv):
                c = groups_per_kv * h + ci
                o0 = o[2 * CHUNK * ci:2 * CHUNK * ci + CHUNK]
                o1 = o[2 * CHUNK * ci + CHUNK:2 * CHUNK * (ci + 1)]
                o_ref[0, blk(qi), LANES * c:LANES * (c + 1)] = jnp.where(first_half, o0, o1).astype(o_ref.dtype)


def _attention(z, zc, lay, att_tabs, sink, local):
    B, T, _ = z.shape
    L = zc.shape[1]
    nb = T // CHUNK
    qb = _chunks_per_step(T)
    rows = qb * CHUNK
    qs, qw = lay['att_q']
    ks, kw = lay['att_k']
    vs, _ = lay['att_v']
    kj, vj = ks // kw, vs // kw
    prev = lambda n: jnp.maximum(n * qb - 1, 0)
    nxt = lambda n: jnp.minimum(n * qb + qb, nb - 1)
    one = lambda f: pl.BlockSpec((CHUNK, LANES), lambda b, n: (f(n), 0))
    cur = pl.BlockSpec((rows, LANES), lambda b, n: (n, 0))
    in_specs = [pl.BlockSpec(memory_space=pltpu.SMEM),
                pl.BlockSpec((1, rows, qw), lambda b, n: (b, n, qs // qw))]
    args = [sink, z]
    if local:
        for j in (kj, vj):
            in_specs += [pl.BlockSpec((1, CHUNK, kw), lambda b, n, j=j: (b, prev(n), j)),
                         pl.BlockSpec((1, rows, kw), lambda b, n, j=j: (b, n, j)),
                         pl.BlockSpec((1, CHUNK, kw), lambda b, n, j=j: (b, nxt(n), j))]
        args += [z] * 6
        in_specs += [cur, cur, one(prev), one(prev), cur, cur, one(nxt), one(nxt)]
        args += [att_tabs['qa'], att_tabs['qb'], att_tabs['ka'], att_tabs['kb'], att_tabs['ka'], att_tabs['kb'],
                 att_tabs['ka'], att_tabs['kb']]
    else:
        in_specs += [pl.BlockSpec((1, LANES), lambda b, n: (0, 0))]
        args += [att_tabs['gq']]
    in_specs += [pl.BlockSpec((1, L, kw), lambda b, n: (b, 0, kj)),
                 pl.BlockSpec((1, L, kw), lambda b, n: (b, 0, vj)),
                 pl.BlockSpec((1, LANES), lambda b, n: (0, 0)),
                 pl.BlockSpec((LANES, LANES), lambda b, n: (0, 0))]
    args += [zc, zc, att_tabs['gk'], att_tabs['ones']]
    return pl.pallas_call(
        functools.partial(_att_body, local, qb),
        grid=(B, nb // qb),
        in_specs=in_specs,
        out_specs=pl.BlockSpec((1, rows, qw), lambda b, n: (b, n, 0)),
        out_shape=jax.ShapeDtypeStruct((B, T, qw), bf16),
        compiler_params=_params("parallel", "parallel"),
        name="attention" if local else "ctx_attention",
    )(*args)


def _merge_body(n_gate_blocks, ret_ref, sgu_ref, att_ref, *refs):
    g_refs = refs[:N_BRANCHES * n_gate_blocks]
    x_ref, wb_ref, wo_ref, gate_ref, scale_ref, shift_ref, wr_ref, xo_ref, h2_ref, aff_ref = refs[len(g_refs):]
    merged = None
    for i, br_ref in enumerate((ret_ref, sgu_ref, att_ref)):
        p = jnp.dot(br_ref[0], wb_ref[i], preferred_element_type=f32)
        g = jnp.concatenate([r[0] for r in g_refs[n_gate_blocks * i:n_gate_blocks * (i + 1)]], axis=1)
        t = jax.nn.sigmoid(g.astype(f32)) * p
        merged = t if merged is None else merged + t
    o = jnp.dot(merged.astype(bf16), wo_ref[...], preferred_element_type=f32)
    x = x_ref[0] + gate_ref[0] * o
    xo_ref[0] = x
    y = x * lax.rsqrt(jnp.mean(x * x, axis=-1, keepdims=True) + EPS)
    h2 = y * scale_ref[0] + shift_ref[0]
    h2_ref[0, :, 0, :] = h2
    logits = jnp.dot(h2.astype(bf16), wr_ref[...], preferred_element_type=f32)
    logits = jnp.where(_lane(logits.shape) < N_EXPERTS, logits, NEG_INF)
    m = jnp.max(logits, axis=-1, keepdims=True)
    e = jnp.exp(logits - m)
    aff_ref[0] = e / jnp.sum(e, axis=-1, keepdims=True)


def _merge(ret, sgu, att, z, gates_at, x, wb, wo, gate, scale, shift, wr):
    B, T, D = x.shape
    bw = ret.shape[-1]
    tm = _pick_tile(T, 512)
    tok = lambda b, i: (b, i, 0)
    vec = lambda b, i: (b, 0, 0)
    gw = 2 * LANES
    assert gates_at % gw == 0 and D % gw == 0
    n_gate_blocks = D // gw
    gate_specs = [pl.BlockSpec((1, tm, gw), lambda b, i, j=gates_at // gw + k: (b, i, j))
                  for k in range(N_BRANCHES * n_gate_blocks)]
    return pl.pallas_call(
        functools.partial(_merge_body, n_gate_blocks),
        grid=(B, T // tm),
        in_specs=[
            pl.BlockSpec((1, tm, bw), tok), pl.BlockSpec((1, tm, bw), tok), pl.BlockSpec((1, tm, bw), tok),
            *gate_specs,
            pl.BlockSpec((1, tm, D), tok),
            pl.BlockSpec((N_BRANCHES, bw, D), lambda b, i: (0, 0, 0)),
            pl.BlockSpec((D, D), lambda b, i: (0, 0)),
            pl.BlockSpec((1, 1, D), vec), pl.BlockSpec((1, 1, D), vec), pl.BlockSpec((1, 1, D), vec),
            pl.BlockSpec((D, LANES), lambda b, i: (0, 0)),
        ],
        out_specs=[
            pl.BlockSpec((1, tm, D), tok), pl.BlockSpec((1, tm, 1, D), lambda b, i: (b, i, 0, 0)),
            pl.BlockSpec((1, tm, LANES), tok),
        ],
        out_shape=[
            jax.ShapeDtypeStruct((B, T, D), f32),
            jax.ShapeDtypeStruct((B, T, 1, D), f32),
            jax.ShapeDtypeStruct((B, T, LANES), f32),
        ],
        compiler_params=_params("parallel", "parallel"),
        name="merge",
    )(ret, sgu, att, *([z] * len(gate_specs)), x, wb, wo, gate, scale, shift, wr)


ROUTE_TILE = 256
AFF_BITS = 31
MIN_NORMAL_BITS = 0x00800000


def _route_body(cap, aff_ref, tri_ref, pos_ref, post_ref, c0_ref, afft):
    T = aff_ref.shape[1]
    tt = tri_ref.shape[0]
    for j in range(T // tt):
        afft[:, tt * j:tt * (j + 1)] = aff_ref[0, tt * j:tt * (j + 1), :].T[:N_EXPERTS]

    def as_float(bits):
        return pltpu.bitcast(jnp.broadcast_to(bits, (N_EXPERTS, LANES)), f32)[:, :1]

    def count_at_least(v):
        return jnp.sum(jnp.where(afft[...] >= v, 1.0, 0.0), axis=1, keepdims=True)

    def refine(i, thr):
        cand = thr | jnp.left_shift(jnp.int32(1), AFF_BITS - 1 - i)
        return jnp.where(count_at_least(as_float(cand)) >= cap, cand, thr)

    thr = lax.fori_loop(0, AFF_BITS, refine, jnp.zeros((N_EXPERTS, 1), jnp.int32))
    hi_col = as_float(jnp.maximum(thr + 1, MIN_NORMAL_BITS))
    diag = (lax.broadcasted_iota(jnp.int32, (N_EXPERTS, LANES), 0)
            == lax.broadcasted_iota(jnp.int32, (N_EXPERTS, LANES), 1))

    def along_lanes(col):
        return jnp.sum(jnp.where(diag, col, 0.0), axis=0, keepdims=True)

    thr_f = along_lanes(as_float(thr))
    hi_f = along_lanes(hi_col)
    need = along_lanes(cap - count_at_least(hi_col))
    tri = tri_ref[...]
    carry_eq = jnp.zeros((1, LANES), f32)
    carry_sel = jnp.zeros((1, LANES), f32)
    for j in range(T // tt):
        a = aff_ref[0, tt * j:tt * (j + 1), :]
        eq = (a >= thr_f) & (a < hi_f)
        eqf = jnp.where(eq, 1.0, 0.0)
        eq_rank = carry_eq + jnp.dot(tri, eqf.astype(bf16), preferred_element_type=f32) - eqf
        sel = (a >= hi_f) | (eq & (eq_rank < need))
        self_ = jnp.where(sel, 1.0, 0.0)
        pos = carry_sel + jnp.dot(tri, self_.astype(bf16), preferred_element_type=f32) - self_
        pos = jnp.where(sel, pos, -1.0).astype(jnp.int32)
        pos_ref[0, tt * j:tt * (j + 1), :] = pos
        post_ref[0, j] = pos.T[:N_EXPERTS]
        c0_ref[0, j:j + 1, :] = carry_sel.astype(jnp.int32)
        carry_eq = carry_eq + jnp.sum(eqf, axis=0, keepdims=True)
        carry_sel = carry_sel + jnp.sum(self_, axis=0, keepdims=True)


def _route(aff, cap):
    B, T, _ = aff.shape
    tt = min(ROUTE_TILE, T)
    nt = T // tt
    tri = jnp.tril(jnp.ones((tt, tt), bf16))
    return pl.pallas_call(
        functools.partial(_route_body, cap),
        grid=(B,),
        in_specs=[pl.BlockSpec((1, T, LANES), lambda b: (b, 0, 0)), pl.BlockSpec((tt, tt), lambda b: (0, 0))],
        out_specs=[
            pl.BlockSpec((1, T, LANES), lambda b: (b, 0, 0)),
            pl.BlockSpec((1, nt, N_EXPERTS, tt), lambda b: (b, 0, 0, 0)),
            pl.BlockSpec((1, nt, LANES), lambda b: (b, 0, 0)),
        ],
        out_shape=[
            jax.ShapeDtypeStruct((B, T, LANES), jnp.int32),
            jax.ShapeDtypeStruct((B, nt, N_EXPERTS, tt), jnp.int32),
            jax.ShapeDtypeStruct((B, nt, LANES), jnp.int32),
        ],
        scratch_shapes=[pltpu.VMEM((N_EXPERTS, T), f32)],
        compiler_params=_params("parallel"),
        name="route",
    )(aff, tri)


def _slots_body(nt, cap, c0_ref, post_ref, idx_ref):
    b = pl.program_id(0)
    tt = post_ref.shape[-1]
    tok = lax.broadcasted_iota(jnp.int32, (LANES, tt), 1)
    diag = lax.broadcasted_iota(jnp.int32, (LANES, LANES), 0) == lax.broadcasted_iota(jnp.int32, (LANES, LANES), 1)

    def per_expert(e, carry):
        def tiles_starting_at_or_before(slot):
            return sum((c0_ref[(b * nt + j) * N_EXPERTS + e] <= slot).astype(jnp.int32) for j in range(nt))

        for k in range(idx_ref.shape[2]):
            slot = LANES * k + lax.broadcasted_iota(jnp.int32, (LANES, 1), 0)

            def match(j, acc):
                return acc + jnp.where(post_ref[0, j, pl.ds(e, 1), :] == slot, tok + j * tt, 0)

            j_lo = tiles_starting_at_or_before(LANES * k) - 1
            j_hi = tiles_starting_at_or_before(min(LANES * (k + 1), cap) - 1)
            acc = lax.fori_loop(j_lo, j_hi, match, jnp.zeros((LANES, tt), jnp.int32))
            col = jnp.sum(acc.astype(f32), axis=1, keepdims=True)
            row = jnp.sum(jnp.where(diag, col, 0.0), axis=0, keepdims=True)
            idx_ref[0, e, k:k + 1, :] = row.astype(jnp.int32)
        return carry

    lax.fori_loop(0, N_EXPERTS, per_expert, 0)


def _slots(c0s, post, cap):
    B, nt, E, tt = post.shape
    nblk = -(-cap // LANES)
    idx = pl.pallas_call(
        functools.partial(_slots_body, nt, cap),
        grid_spec=pltpu.PrefetchScalarGridSpec(
            num_scalar_prefetch=1,
            grid=(B,),
            in_specs=[pl.BlockSpec((1, nt, E, tt), lambda b, c0: (b, 0, 0, 0))],
            out_specs=pl.BlockSpec((1, E, nblk, LANES), lambda b, c0: (b, 0, 0, 0)),
        ),
        out_shape=jax.ShapeDtypeStruct((B, E, nblk, LANES), jnp.int32),
        compiler_params=_params("arbitrary"),
        name="slots",
    )(c0s, post)
    return idx.reshape(B, E, nblk * LANES)[:, :, :cap]


EXPERT_COPY_BATCHES = 3
EXPERT_MIN_ROWS = 256


def _expert_body(cap, spb, idx_ref, h_ref, wg_ref, wu_ref, wd_ref, y_ref, rows, xs_s, act_s, hm_s, wg_s, wu_s, wd_s,
                 sem):
    n_b = pl.num_programs(1)
    step = pl.program_id(0) * n_b + pl.program_id(1)
    last = pl.num_programs(0) * n_b - 1
    n_rows = spb * cap
    bounds = [n_rows * k // EXPERT_COPY_BATCHES for k in range(EXPERT_COPY_BATCHES + 1)]

    def start_rows(s, buf, lo, hi):
        e, g = s // n_b, s % n_b
        for r in range(lo, hi):
            b = g * spb + r // cap
            row = idx_ref[(b * N_EXPERTS + e) * cap + r % cap]
            pltpu.make_async_copy(h_ref.at[row], rows.at[buf, pl.ds(r, 1), :], sem.at[buf]).start()

    def wait_rows(buf):
        pltpu.make_async_copy(h_ref.at[pl.ds(0, n_rows), 0, :], rows.at[buf], sem.at[buf]).wait()

    @pl.when(step == 0)
    def _():
        start_rows(step, 0, 0, n_rows)

    @pl.when(pl.program_id(1) == 0)
    def _():
        wg_s[...] = wg_ref[0, 0].astype(bf16)
        wu_s[...] = wu_ref[0, 0].astype(bf16)
        wd_s[...] = wd_ref[0, 0].astype(bf16)

    for cur in range(2):
        @pl.when(step % 2 == cur)
        def _():
            wait_rows(cur)
            xs_s[...] = rows[cur].astype(bf16)

            def prefetch(k):
                @pl.when(step < last)
                def _():
                    start_rows(step + 1, 1 - cur, bounds[k], bounds[k + 1])

            prefetch(0)
            act_s[...] = jax.nn.silu(jnp.dot(xs_s[...], wg_s[...], preferred_element_type=f32))
            prefetch(1)
            hm_s[...] = (act_s[...] * jnp.dot(xs_s[...], wu_s[...], preferred_element_type=f32)).astype(bf16)
            prefetch(2)
            y = jnp.dot(hm_s[...], wd_s[...], preferred_element_type=f32).astype(y_ref.dtype)
            for i in range(spb):
                y_ref[i, 0] = y[cap * i:cap * (i + 1)]


def _expert_ffn(idx, h2, layer, wg, wu, wd):
    rows_of = idx + (jnp.arange(idx.shape[0], dtype=jnp.int32) * h2.shape[1])[:, None, None]
    B, T, _, D = h2.shape
    _, E, _, F = wg.shape
    cap = idx.shape[-1]
    spb = _pick_tile(B, max(1, EXPERT_MIN_ROWS // cap))
    n_rows = spb * cap
    wsp = lambda e, g, idx: (layer, e, 0, 0)
    return pl.pallas_call(
        functools.partial(_expert_body, cap, spb),
        grid_spec=pltpu.PrefetchScalarGridSpec(
            num_scalar_prefetch=1,
            grid=(E, B // spb),
            in_specs=[
                pl.BlockSpec(memory_space=pl.ANY),
                pl.BlockSpec((1, 1, D, F), wsp), pl.BlockSpec((1, 1, D, F), wsp), pl.BlockSpec((1, 1, F, D), wsp),
            ],
            out_specs=pl.BlockSpec((spb, 1, cap, D), lambda e, g, idx: (g, e, 0, 0)),
            scratch_shapes=[pltpu.VMEM((2, n_rows, D), f32), pltpu.VMEM((n_rows, D), bf16),
                            pltpu.VMEM((n_rows, F), f32), pltpu.VMEM((n_rows, F), bf16),
                            pltpu.VMEM((D, F), bf16), pltpu.VMEM((D, F), bf16), pltpu.VMEM((F, D), bf16),
                            pltpu.SemaphoreType.DMA((2,))],
        ),
        out_shape=jax.ShapeDtypeStruct((B, E, cap, D), bf16),
        compiler_params=pltpu.CompilerParams(dimension_semantics=("arbitrary", "arbitrary"),
                                             vmem_limit_bytes=EXPERT_VMEM_LIMIT_BYTES),
        name="expert_ffn",
    )(rows_of.reshape(-1), h2.reshape(B * T, 1, D), wg, wu, wd)


Y_ROW_ALIGN = 16


def _combine_body(nt, cap, win, c0_ref, pos_ref, aff_ref, x_ref, gate_ref, spread_ref, y_ref, o_ref, buf, extra,
                  sem, sem_x):
    b = pl.program_id(0)
    j = pl.program_id(1)
    step = b * nt + j
    tt = pos_ref.shape[1]
    stride = win if win == cap else win - Y_ROW_ALIGN

    def c0(bb, jj, e):
        return c0_ref[(bb * nt + jj) * N_EXPERTS + e]

    def window_start(lo):
        return pl.multiple_of(jnp.minimum(lo & (-Y_ROW_ALIGN), cap - win), Y_ROW_ALIGN)

    def copies(bb, jj, p, dst, s):
        return [pltpu.make_async_copy(
            y_ref.at[bb, e, pl.ds(window_start(c0(bb, jj, e) + p * stride), win), :],
            dst.at[pl.ds(e * win, win), :], s) for e in range(N_EXPERTS)]

    @pl.when(step == 0)
    def _():
        for cp in copies(b, j, 0, buf.at[0], sem.at[0]):
            cp.start()

    nxt = step + 1

    @pl.when(nxt < pl.num_programs(0) * nt)
    def _():
        for cp in copies(nxt // nt, nxt % nt, 0, buf.at[nxt % 2], sem.at[nxt % 2]):
            cp.start()

    lane = _lane((1, LANES))
    pos = pos_ref[0]
    aff = aff_ref[0]

    def lane_vector(fn):
        v = jnp.zeros((1, LANES), jnp.int32)
        for e in range(N_EXPERTS):
            v = jnp.where(lane == e, fn(e), v)
        return v

    spread = spread_ref[...]
    col_slot = _lane((1, N_EXPERTS * win)) & (win - 1)
    aff_cols = jnp.dot(aff.astype(bf16), spread, preferred_element_type=f32)

    def weights(p):
        lo = lane_vector(lambda e: c0(b, j, e) + p * stride)
        start = lane_vector(lambda e: window_start(c0(b, j, e) + p * stride))
        rel = jnp.where((pos >= lo) & (pos < lo + stride), pos - start, -1).astype(f32)
        rel_cols = jnp.dot(rel.astype(bf16), spread, preferred_element_type=f32)
        return jnp.where(rel_cols == col_slot.astype(f32), aff_cols, 0.0).astype(bf16)

    for cp in copies(b, j, 0, buf.at[step % 2], sem.at[step % 2]):
        cp.wait()
    moe = jnp.dot(weights(0), buf[step % 2], preferred_element_type=f32)

    def count(e):
        end = jnp.where(j + 1 < nt, c0(b, jnp.minimum(j + 1, nt - 1), e), cap)
        return end - c0(b, j, e)

    max_count = functools.reduce(jnp.maximum, [count(e) for e in range(N_EXPERTS)])

    def more(p, acc):
        cps = copies(b, j, p, extra, sem_x.at[0])
        for cp in cps:
            cp.start()
        for cp in cps:
            cp.wait()
        return acc + jnp.dot(weights(p), extra[...], preferred_element_type=f32)

    moe = lax.fori_loop(1, (max_count + stride - 1) // stride, more, moe)
    o_ref[0] = x_ref[0] + gate_ref[0] * moe


def _combine(c0s, pos, aff, x, gate, y):
    B, T, D = x.shape
    E, cap = y.shape[1], y.shape[2]
    tt = min(ROUTE_TILE, T)
    nt = T // tt
    win = min(cap, LANES // 2)
    assert (cap - win) % Y_ROW_ALIGN == 0 and win & (win - 1) == 0
    spread = (jnp.arange(LANES)[:, None] == jnp.arange(E * win)[None, :] // win).astype(bf16)
    tok = lambda b, j, c0: (b, j, 0)
    return pl.pallas_call(
        functools.partial(_combine_body, nt, cap, win),
        grid_spec=pltpu.PrefetchScalarGridSpec(
            num_scalar_prefetch=1,
            grid=(B, nt),
            in_specs=[
                pl.BlockSpec((1, tt, LANES), tok), pl.BlockSpec((1, tt, LANES), tok), pl.BlockSpec((1, tt, D), tok),
                pl.BlockSpec((1, 1, D), lambda b, j, c0: (b, 0, 0)),
                pl.BlockSpec((LANES, E * win), lambda b, j, c0: (0, 0)),
                pl.BlockSpec(memory_space=pl.ANY),
            ],
            out_specs=pl.BlockSpec((1, tt, D), tok),
            scratch_shapes=[pltpu.VMEM((2, E * win, D), bf16), pltpu.VMEM((E * win, D), bf16),
                            pltpu.SemaphoreType.DMA((2,)), pltpu.SemaphoreType.DMA((1,))],
        ),
        out_shape=jax.ShapeDtypeStruct((B, T, D), f32),
        compiler_params=_params("arbitrary", "arbitrary"),
        name="combine",
    )(c0s, pos, aff, x, gate, spread, y)


def _expert_choice(h2, aff, x, gate2, p):
    T = x.shape[1]
    cap = CAPACITY_FACTOR * T // N_EXPERTS
    pos, post, c0 = _route(aff, cap)
    c0s = c0[:, :, :N_EXPERTS].reshape(-1)
    y = _expert_ffn(_slots(c0s, post, cap), h2, p['layer'], p['w_exp_gate'], p['w_exp_up'], p['w_exp_down'])
    return _combine(c0s, pos, aff, x, gate2, y)


def _rope_tables(T):
    rows = T // GRID_W
    row = jnp.repeat(jnp.arange(rows), GRID_W).astype(f32)
    col = jnp.tile(jnp.arange(GRID_W), rows).astype(f32)
    n_freq = HEAD_DIM // 4
    inv = jnp.power(ROPE_BASE, -jnp.arange(n_freq, dtype=f32) / n_freq)
    ang = jnp.concatenate([row[:, None] * inv, col[:, None] * inv], axis=-1)
    cos, sin = jnp.cos(ang), jnp.sin(ang)
    reps = LANES // HEAD_DIM
    return jnp.tile(jnp.concatenate([cos, cos], axis=-1), (1, reps)), jnp.tile(
        jnp.concatenate([-sin, sin], axis=-1), (1, reps))


def _identity_rope(T):
    return jnp.ones((T, LANES), f32), jnp.zeros((T, LANES), f32)


def _retention_tables(decay_fwd, decay_bwd):
    lgf = jax.nn.log_sigmoid(decay_fwd.astype(f32))
    lgb = jax.nn.log_sigmoid(decay_bwd.astype(f32))
    C = CHUNK
    k_scale = HEAD_DIM ** -0.5
    idx = jnp.arange(C, dtype=f32)
    npair = RET_HEADS // 2

    def lanes(t):
        return jnp.repeat(t.reshape(npair, 2, C).transpose(0, 2, 1), HEAD_DIM, axis=-1)

    diff = idx[:, None] - idx[None, :]
    dm = jnp.where(diff >= 0, jnp.exp(jnp.maximum(diff, 0.0)[None] * lgf[:, None, None]),
                   jnp.exp(jnp.maximum(-diff, 0.0)[None] * lgb[:, None, None])) * k_scale

    def chunk_decay(lg):
        rows = jnp.repeat(jnp.exp(C * lg).reshape(npair, 2), HEAD_DIM, axis=-1)
        return jnp.broadcast_to(rows[:, :, None], (npair, LANES, 2 * LANES))

    r = jnp.arange(LANES)[:, None] // HEAD_DIM
    c = jnp.arange(2 * LANES)[None, :] // LANES
    return {
        'wkf': lanes(jnp.exp((C - 1 - idx)[None, :] * lgf[:, None])) * k_scale,
        'wkb': lanes(jnp.exp(idx[None, :] * lgb[:, None])) * k_scale,
        'qdf': lanes(jnp.exp((idx + 1.0)[None, :] * lgf[:, None])),
        'qdb': lanes(jnp.exp((C - idx)[None, :] * lgb[:, None])),
        'dm': dm, 'cdf': chunk_decay(lgf), 'cdb': chunk_decay(lgb), 'bm': (r == c).astype(f32),
    }


def _attention_tables(cos, sin, q_gain, k_gain):
    reps = LANES // HEAD_DIM
    half = HEAD_DIM // 2
    swap = lambda g: jnp.concatenate([g[half:], g[:half]])
    gq, gk = jnp.tile(q_gain, reps)[None], jnp.tile(k_gain, reps)[None]
    gqs, gks = jnp.tile(swap(q_gain), reps)[None], jnp.tile(swap(k_gain), reps)[None]
    r = jnp.arange(LANES) // HEAD_DIM
    return {'qa': cos * gq, 'qb': sin * gqs, 'ka': cos * gk, 'kb': sin * gks, 'gq': gq, 'gk': gk,
            'ones': (r[:, None] == r[None, :]).astype(bf16)}


def _layer(x, xc, mod_lat, mod_ctx, rope, p, last):
    B, S, D = x.shape
    L = xc.shape[1]
    lay = _layout(D)
    sh1, sc1, g1, sh2, sc2, g2 = jnp.split(mod_lat, 6, axis=-1)
    csh1, csc1, cg1, csh2, csc2, cg2 = jnp.split(mod_ctx, 6, axis=-1)
    n1 = p['norm1_g'][None, None, :]
    n2 = p['norm2_g'][None, None, :]

    def bcast(t):
        return jnp.broadcast_to(t, (B, 1, D))

    z = _proj_in(x, n1 * (1 + sc1), sh1, p['w_in'])
    zc = _proj_in(xc, bcast(n1 * (1 + csc1)), bcast(csh1), p['w_in'])

    cos, sin = rope
    cos_c, sin_c = _identity_rope(L)
    rtabs = _retention_tables(p['ret_decay_fwd'], p['ret_decay_bwd'])
    atabs = _attention_tables(cos, sin, p['q_norm_g'], p['k_norm_g'])
    sgu_w = p['sgu_w'].astype(bf16)
    gw = D // 2 // SGU_GROUPS
    sgu_bias = jnp.repeat(p['sgu_b'].T, gw, axis=-1)

    zero = jnp.zeros((B, RET_HEADS // 2, LANES, 2 * LANES), f32)
    sfc, sbc, s_f, s_b = _ret_states(zc, lay, cos_c, sin_c, rtabs, zero, zero)
    sf, sb, _, _ = _ret_states(z, lay, cos, sin, rtabs, s_f, s_b)
    ret, sgu = _ret_sgu(z, lay, cos, sin, sf, sb, rtabs, sgu_w, sgu_bias)
    att = _attention(z, zc, lay, atabs, p['attn_sink'], True)

    x, h2, aff = _merge(ret, sgu, att, z, lay['gates'][0], x, p['w_branch'], p['w_out'], g1, n2 * (1 + sc2), sh2,
                        p['w_router'])
    x = _expert_choice(h2, aff, x, g2, p)
    if last:
        return x, None

    ret_c, sgu_c = _ret_sgu(zc, lay, cos_c, sin_c, sfc, sbc, rtabs, sgu_w, sgu_bias)
    att_c = _attention(zc, zc, lay, atabs, p['attn_sink'], False)
    xc, h2c, affc = _merge(ret_c, sgu_c, att_c, zc, lay['gates'][0], xc, p['w_branch'], p['w_out'], bcast(cg1),
                           bcast(n2 * (1 + csc2)), bcast(csh2), p['w_router'])
    xc = _expert_choice(h2c, affc, xc, bcast(cg2), p)
    return x, xc


def kernel(x, c, ctx, c_ctx, w_mod, b_mod, norm1_g, norm2_g, w_in, ret_decay_fwd, ret_decay_bwd, sgu_w, sgu_b,
           q_norm_g, k_norm_g, attn_sink, w_branch, w_out, w_router, w_exp_gate, w_exp_up, w_exp_down):
    B, S, D = x.shape
    depth = w_in.shape[0]
    rope = _rope_tables(S)
    c_act = jax.nn.silu(c)
    c_ctx_act = jax.nn.silu(c_ctx)
    xc = ctx
    hp = lax.Precision.HIGHEST
    for l in range(depth):
        wr = jnp.zeros((D, LANES), f32).at[:, :N_EXPERTS].set(w_router[l]).astype(bf16)
        p = {
            'norm1_g': norm1_g[l], 'norm2_g': norm2_g[l], 'w_in': w_in[l].astype(bf16),
            'ret_decay_fwd': ret_decay_fwd[l], 'ret_decay_bwd': ret_decay_bwd[l],
            'sgu_w': sgu_w[l], 'sgu_b': sgu_b[l],
            'q_norm_g': q_norm_g[l], 'k_norm_g': k_norm_g[l], 'attn_sink': attn_sink[l],
            'w_branch': w_branch[l].astype(bf16), 'w_out': w_out[l].astype(bf16), 'w_router': wr,
            'layer': l, 'w_exp_gate': w_exp_gate, 'w_exp_up': w_exp_up, 'w_exp_down': w_exp_down,
        }
        mod = jnp.dot(jnp.concatenate([c_act, c_ctx_act[None]], axis=0), w_mod[l], precision=hp) + b_mod[l]
        mod_lat, mod_ctx = mod[:B, None, :], mod[B:, None, :]
        x, xc = _layer(x, xc, mod_lat, mod_ctx, rope, p, l == depth - 1)
    return x
```

```python
import functools

import jax
import jax.numpy as jnp
from jax import lax
from jax.experimental import pallas as pl
from jax.experimental.pallas import tpu as pltpu

GRID_W = 64
N_BRANCHES = 3
RET_HEADS = 4
SGU_GROUPS = 4
CHUNK = 128
HEAD_DIM = 64
ATT_KV_HEADS = 2
ROPE_BASE = 10000.0
N_EXPERTS = 16
CAPACITY_FACTOR = 2
EPS = 1e-6
NEG_INF = -1e30

LANES = 128
VMEM_LIMIT_BYTES = 48 * 1024 * 1024
EXPERT_VMEM_LIMIT_BYTES = 56 * 1024 * 1024

f32 = jnp.float32
bf16 = jnp.bfloat16


def _layout(d_model):
    bw = d_model // 2
    qk = RET_HEADS * HEAD_DIM
    kv = ATT_KV_HEADS * HEAD_DIM
    widths = (('ret_q', qk), ('ret_k', qk), ('ret_v', bw), ('ret_g', bw), ('sgu_u', bw), ('sgu_v', bw),
              ('att_q', bw), ('att_k', kv), ('att_v', kv), ('gates', N_BRANCHES * d_model))
    out, start = {}, 0
    for name, w in widths:
        out[name] = (start, w)
        start += w
    return out


def _chunks_per_step(T):
    return next(c for c in (4, 2, 1) if (T // CHUNK) % c == 0)


def _pick_tile(n, pref):
    t = min(n, pref)
    while n % t:
        t //= 2
    return t


def _params(*sem):
    return pltpu.CompilerParams(dimension_semantics=sem, vmem_limit_bytes=VMEM_LIMIT_BYTES)


PROJ_NORM_ROWS = 256


def _proj_in_body(x_ref, scale_ref, shift_ref, w_ref, o_ref, h_ref):
    @pl.when(pl.program_id(2) == 0)
    def _():
        step = min(PROJ_NORM_ROWS, h_ref.shape[0])
        for r in range(0, h_ref.shape[0], step):
            x = x_ref[0, r:r + step, :]
            y = x * lax.rsqrt(jnp.mean(x * x, axis=-1, keepdims=True) + EPS)
            h = (y * scale_ref[0] + shift_ref[0]).astype(bf16)
            h_ref[r:r + step, :] = h
            o_ref[0, r:r + step, :] = jnp.dot(h, w_ref[...], preferred_element_type=f32).astype(o_ref.dtype)

    @pl.when(pl.program_id(2) != 0)
    def _():
        o_ref[0] = jnp.dot(h_ref[...], w_ref[...], preferred_element_type=f32).astype(o_ref.dtype)


def _proj_in(x, scale, shift, w):
    B, T, D = x.shape
    N = w.shape[1]
    tm = _pick_tile(T, 2048)
    tn = 1280 if N % 1280 == 0 else _pick_tile(N, 1024)
    return pl.pallas_call(
        _proj_in_body,
        grid=(B, T // tm, N // tn),
        in_specs=[
            pl.BlockSpec((1, tm, D), lambda b, i, j: (b, i, 0)),
            pl.BlockSpec((1, 1, D), lambda b, i, j: (b, 0, 0)),
            pl.BlockSpec((1, 1, D), lambda b, i, j: (b, 0, 0)),
            pl.BlockSpec((D, tn), lambda b, i, j: (0, j)),
        ],
        out_specs=pl.BlockSpec((1, tm, tn), lambda b, i, j: (b, i, j)),
        out_shape=jax.ShapeDtypeStruct((B, T, N), bf16),
        scratch_shapes=[pltpu.VMEM((tm, D), bf16)],
        compiler_params=_params("parallel", "parallel", "arbitrary"),
        name="proj_in",
    )(x, scale, shift, w)


def _lane(shape):
    return lax.broadcasted_iota(jnp.int32, shape, 1)


def _swap_halves(x):
    return jnp.where((_lane(x.shape) & (HEAD_DIM // 2)) == 0,
                     pltpu.roll(x, LANES - HEAD_DIM // 2, 1), pltpu.roll(x, HEAD_DIM // 2, 1))


def _rope(x, cos, sin_signed):
    return x * cos + _swap_halves(x) * sin_signed


def _head_sumsq(x, ones_blockdiag):
    x2 = x * x
    hi = x2.astype(bf16)
    lo = (x2 - hi.astype(f32)).astype(bf16)
    return (jnp.dot(hi, ones_blockdiag, preferred_element_type=f32)
            + jnp.dot(lo, ones_blockdiag, preferred_element_type=f32))


def _dot_nt(a, b):
    return lax.dot_general(a, b, (((1,), (1,)), ((), ())), preferred_element_type=f32)


def _dot_tn(a, b):
    return lax.dot_general(a, b, (((0,), (0,)), ((), ())), preferred_element_type=f32)


def _ret_state_body(cpb, kf_ref, vf_ref, kb_ref, vb_ref, cosf_ref, sinf_ref, cosb_ref, sinb_ref,
                    wkf_ref, wkb_ref, cdf_ref, cdb_ref, bm_ref, s0f_ref, s0b_ref,
                    sf_ref, sb_ref, ff_ref, fb_ref, accf, accb):
    n = pl.program_id(1)

    @pl.when(n == 0)
    def _():
        accf[...] = s0f_ref[0]
        accb[...] = s0b_ref[0]

    def one_chunk(c, k_ref, v_ref, cos_ref, sin_ref, wk_ref, cd_ref, out_ref, acc):
        rows = slice(CHUNK * c, CHUNK * (c + 1))
        k = k_ref[0, rows, :].astype(f32)
        v = v_ref[0, rows, :]
        for p in range(RET_HEADS // 2):
            kp = _rope(k[:, LANES * p:LANES * (p + 1)], cos_ref[rows, :], sin_ref[rows, :])
            kw = (kp * wk_ref[p]).astype(bf16)
            upd = _dot_tn(kw, v[:, 2 * LANES * p:2 * LANES * (p + 1)])
            s = acc[p]
            out_ref[0, c, p] = s.astype(out_ref.dtype)
            acc[p] = cd_ref[p] * s + bm_ref[...] * upd

    for c in range(cpb):
        one_chunk(c, kf_ref, vf_ref, cosf_ref, sinf_ref, wkf_ref, cdf_ref, sf_ref, accf)
        one_chunk(cpb - 1 - c, kb_ref, vb_ref, cosb_ref, sinb_ref, wkb_ref, cdb_ref, sb_ref, accb)

    @pl.when(n == pl.num_programs(1) - 1)
    def _():
        ff_ref[0] = accf[...]
        fb_ref[0] = accb[...]


def _ret_states(z, lay, cos, sin, tabs, s0f, s0b):
    B, T, _ = z.shape
    cpb = _chunks_per_step(T)
    rows = cpb * CHUNK
    nc = T // rows
    kq = lay['ret_k'][0] // (RET_HEADS * HEAD_DIM)
    vq = lay['ret_v'][0] // lay['ret_v'][1]
    vw = lay['ret_v'][1]
    kw = RET_HEADS * HEAD_DIM
    npair = RET_HEADS // 2
    st_shape = (npair, LANES, 2 * LANES)
    const3 = lambda b, n: (0, 0, 0)
    return pl.pallas_call(
        functools.partial(_ret_state_body, cpb),
        grid=(B, nc),
        in_specs=[
            pl.BlockSpec((1, rows, kw), lambda b, n: (b, n, kq)),
            pl.BlockSpec((1, rows, vw), lambda b, n: (b, n, vq)),
            pl.BlockSpec((1, rows, kw), lambda b, n: (b, nc - 1 - n, kq)),
            pl.BlockSpec((1, rows, vw), lambda b, n: (b, nc - 1 - n, vq)),
            pl.BlockSpec((rows, LANES), lambda b, n: (n, 0)),
            pl.BlockSpec((rows, LANES), lambda b, n: (n, 0)),
            pl.BlockSpec((rows, LANES), lambda b, n: (nc - 1 - n, 0)),
            pl.BlockSpec((rows, LANES), lambda b, n: (nc - 1 - n, 0)),
            pl.BlockSpec((npair, CHUNK, LANES), const3),
            pl.BlockSpec((npair, CHUNK, LANES), const3),
            pl.BlockSpec(st_shape, const3),
            pl.BlockSpec(st_shape, const3),
            pl.BlockSpec((LANES, 2 * LANES), lambda b, n: (0, 0)),
            pl.BlockSpec((1,) + st_shape, lambda b, n: (b, 0, 0, 0)),
            pl.BlockSpec((1,) + st_shape, lambda b, n: (b, 0, 0, 0)),
        ],
        out_specs=[
            pl.BlockSpec((1, cpb) + st_shape, lambda b, n: (b, n, 0, 0, 0)),
            pl.BlockSpec((1, cpb) + st_shape, lambda b, n: (b, nc - 1 - n, 0, 0, 0)),
            pl.BlockSpec((1,) + st_shape, lambda b, n: (b, 0, 0, 0)),
            pl.BlockSpec((1,) + st_shape, lambda b, n: (b, 0, 0, 0)),
        ],
        out_shape=[
            jax.ShapeDtypeStruct((B, nc * cpb) + st_shape, bf16),
            jax.ShapeDtypeStruct((B, nc * cpb) + st_shape, bf16),
            jax.ShapeDtypeStruct((B,) + st_shape, f32),
            jax.ShapeDtypeStruct((B,) + st_shape, f32),
        ],
        scratch_shapes=[pltpu.VMEM(st_shape, f32), pltpu.VMEM(st_shape, f32)],
        compiler_params=_params("parallel", "arbitrary"),
        name="ret_states",
    )(z, z, z, z, cos, sin, cos, sin, tabs['wkf'], tabs['wkb'], tabs['cdf'], tabs['cdb'], tabs['bm'], s0f, s0b)


def _ret_sgu_body(cpb, *refs):
    for c in range(cpb):
        _ret_sgu_chunk(c, *refs)


def _ret_sgu_chunk(c, q_ref, k_ref, v_ref, g_ref, u_ref, w_ref, cos_ref, sin_ref, sf_ref, sb_ref,
                   dm_ref, qdf_ref, qdb_ref, sw_ref, sbias_ref, ret_ref, sgu_ref):
    rows = slice(CHUNK * c, CHUNK * (c + 1))
    q = q_ref[0, rows, :].astype(f32)
    k = k_ref[0, rows, :].astype(f32)
    v = v_ref[0, rows, :]
    g = g_ref[0, rows, :].astype(f32)
    cos = cos_ref[rows, :]
    sin = sin_ref[rows, :]
    first_half = _lane((CHUNK, LANES)) < HEAD_DIM
    for p in range(RET_HEADS // 2):
        qp = _rope(q[:, LANES * p:LANES * (p + 1)], cos, sin)
        kp = _rope(k[:, LANES * p:LANES * (p + 1)], cos, sin).astype(bf16)
        cross = (jnp.dot((qp * qdf_ref[p]).astype(bf16), sf_ref[0, c, p], preferred_element_type=f32)
                 + jnp.dot((qp * qdb_ref[p]).astype(bf16), sb_ref[0, c, p], preferred_element_type=f32))
        for a in range(2):
            h = 2 * p + a
            qm = jnp.where(first_half if a == 0 else ~first_half, qp, 0.0).astype(bf16)
            sc = (_dot_nt(qm, kp) * dm_ref[h]).astype(bf16)
            o = jnp.dot(sc, v[:, LANES * h:LANES * (h + 1)], preferred_element_type=f32)
            o = o + cross[:, LANES * a:LANES * (a + 1)]
            oc = o - jnp.mean(o, axis=-1, keepdims=True)
            y = oc * lax.rsqrt(jnp.mean(oc * oc, axis=-1, keepdims=True) + EPS)
            ret_ref[0, rows, LANES * h:LANES * (h + 1)] = (
                y * jax.nn.silu(g[:, LANES * h:LANES * (h + 1)])).astype(ret_ref.dtype)

    u = jax.nn.gelu(u_ref[0, rows, :].astype(f32))
    w = jax.nn.gelu(w_ref[0, rows, :].astype(f32))
    wc = w - jnp.mean(w, axis=-1, keepdims=True)
    wn = (wc * lax.rsqrt(jnp.mean(wc * wc, axis=-1, keepdims=True) + EPS)).astype(bf16)
    gw = wn.shape[-1] // SGU_GROUPS
    for i in range(SGU_GROUPS):
        sl = slice(gw * i, gw * (i + 1))
        mixed = jnp.dot(sw_ref[i], wn[:, sl], preferred_element_type=f32) + sbias_ref[:, sl]
        sgu_ref[0, rows, sl] = (u[:, sl] * mixed).astype(sgu_ref.dtype)


def _ret_sgu(z, lay, cos, sin, sf, sb, tabs, sgu_w, sgu_bias):
    B, T, _ = z.shape
    cpb = _chunks_per_step(T)
    rows = cpb * CHUNK
    nc = T // rows
    bw = lay['ret_v'][1]

    def zs(name):
        s, w = lay[name]
        return pl.BlockSpec((1, rows, w), lambda b, n, j=s // w: (b, n, j))

    st_shape = (RET_HEADS // 2, LANES, 2 * LANES)
    const3 = lambda b, n: (0, 0, 0)
    out_spec = pl.BlockSpec((1, rows, bw), lambda b, n: (b, n, 0))
    return pl.pallas_call(
        functools.partial(_ret_sgu_body, cpb),
        grid=(B, nc),
        in_specs=[
            zs('ret_q'), zs('ret_k'), zs('ret_v'), zs('ret_g'), zs('sgu_u'), zs('sgu_v'),
            pl.BlockSpec((rows, LANES), lambda b, n: (n, 0)),
            pl.BlockSpec((rows, LANES), lambda b, n: (n, 0)),
            pl.BlockSpec((1, cpb) + st_shape, lambda b, n: (b, n, 0, 0, 0)),
            pl.BlockSpec((1, cpb) + st_shape, lambda b, n: (b, n, 0, 0, 0)),
            pl.BlockSpec((RET_HEADS, CHUNK, CHUNK), const3),
            pl.BlockSpec((RET_HEADS // 2, CHUNK, LANES), const3),
            pl.BlockSpec((RET_HEADS // 2, CHUNK, LANES), const3),
            pl.BlockSpec((SGU_GROUPS, CHUNK, CHUNK), const3),
            pl.BlockSpec((CHUNK, bw), lambda b, n: (0, 0)),
        ],
        out_specs=[out_spec, out_spec],
        out_shape=[jax.ShapeDtypeStruct((B, T, bw), bf16), jax.ShapeDtypeStruct((B, T, bw), bf16)],
        compiler_params=_params("parallel", "parallel"),
        name="ret_sgu",
    )(z, z, z, z, z, z, cos, sin, sf, sb, tabs['dm'], tabs['qdf'], tabs['qdb'], sgu_w, sgu_bias)


def _att_body(local, qb, sink_ref, q_ref, *refs):
    if local:
        (kp_ref, kc_ref, kn_ref, vp_ref, vc_ref, vn_ref, qa_ref, qb_ref,
         kap_ref, kbp_ref, kac_ref, kbc_ref, kan_ref, kbn_ref, ck_ref, cv_ref, gk_ref, ones_ref, o_ref) = refs
    else:
        qa_ref, ck_ref, cv_ref, gk_ref, ones_ref, o_ref = refs
    n = pl.program_id(1)
    n_blocks = pl.num_programs(1) * qb
    ones = ones_ref[...]
    inv_hd = 1.0 / HEAD_DIM

    def norm_rope(x_ref_val, ta, tb):
        x = x_ref_val.astype(f32)
        r = lax.rsqrt(_head_sumsq(x, ones) * inv_hd + EPS)
        y = x * ta if tb is None else x * ta + _swap_halves(x) * tb
        return y * r

    def dup(x):
        lo = _lane(x.shape) < HEAD_DIM
        xr = pltpu.roll(x, HEAD_DIM, 1)
        return jnp.where(lo, x, xr).astype(bf16), jnp.where(lo, xr, x).astype(bf16)

    def blk(i):
        return slice(CHUNK * i, CHUNK * (i + 1))

    k_ctx = dup(norm_rope(ck_ref[0], gk_ref[...], None))
    v_ctx = dup(cv_ref[0].astype(f32))
    if local:
        k_loc = ([dup(norm_rope(kp_ref[0], kap_ref[...], kbp_ref[...]))]
                 + [dup(norm_rope(kc_ref[0, blk(i), :], kac_ref[blk(i), :], kbc_ref[blk(i), :])) for i in range(qb)]
                 + [dup(norm_rope(kn_ref[0], kan_ref[...], kbn_ref[...]))])
        v_loc = ([dup(vp_ref[0].astype(f32))] + [dup(vc_ref[0, blk(i), :].astype(f32)) for i in range(qb)]
                 + [dup(vn_ref[0].astype(f32))])
    n_loc = 3 * CHUNK if local else 0

    n_groups = q_ref.shape[-1] // LANES
    groups_per_kv = n_groups // ATT_KV_HEADS
    rows = 2 * groups_per_kv * CHUNK
    log2e = 1.4426950408889634
    first_half = _lane((CHUNK, LANES)) < HEAD_DIM
    for qi in range(qb):
        if local:
            i = lax.broadcasted_iota(jnp.int32, (CHUNK, n_loc), 0)
            j = lax.broadcasted_iota(jnp.int32, (CHUNK, n_loc), 1)
            has_prev = (n * qb + qi > 0).astype(jnp.int32)
            has_next = (n * qb + qi < n_blocks - 1).astype(jnp.int32)
            valid = (j >= i * has_prev + CHUNK * (1 - has_prev)) & (j <= 2 * CHUNK - 1 + has_next * (i + 1))
            bias = jnp.where(valid, 0.0, NEG_INF)
            bias = jnp.concatenate([bias] * (rows // CHUNK), axis=0)
        for h in range(ATT_KV_HEADS):
            keys = ([k_loc[qi + d][h] for d in range(3)] if local else []) + [k_ctx[h]]
            vals = ([v_loc[qi + d][h] for d in range(3)] if local else []) + [v_ctx[h]]
            keys = jnp.concatenate(keys, axis=0) if local else keys[0]
            vals = jnp.concatenate(vals, axis=0) if local else vals[0]
            q_rows, sink_rows = [], []
            for c in range(groups_per_kv * h, groups_per_kv * (h + 1)):
                xq = q_ref[0, blk(qi), LANES * c:LANES * (c + 1)]
                qn = norm_rope(xq, qa_ref[blk(qi), :] if local else qa_ref[...],
                               qb_ref[blk(qi), :] if local else None) * (HEAD_DIM ** -0.5 * log2e)
                for a in range(2):
                    q_rows.append(jnp.where(first_half if a == 0 else ~first_half, qn, 0.0).astype(bf16))
                    sink_rows.append(jnp.full((CHUNK, 1), sink_ref[2 * c + a] * log2e, f32))
            s = _dot_nt(jnp.concatenate(q_rows, axis=0), keys)
            sk = jnp.concatenate(sink_rows, axis=0)
            s_ctx = s[:, n_loc:]
            m = jnp.maximum(jnp.max(s_ctx, axis=-1, keepdims=True), sk)
            if local:
                s_loc = s[:, :n_loc] + bias
                m = jnp.maximum(m, jnp.max(s_loc, axis=-1, keepdims=True))
            p_ctx = jnp.exp2(s_ctx - m)
            d = jnp.sum(p_ctx, axis=-1, keepdims=True) + jnp.exp2(sk - m)
            o = jnp.dot(p_ctx.astype(bf16), vals[n_loc:], preferred_element_type=f32)
            if local:
                p_loc = jnp.exp2(s_loc - m)
                d = d + jnp.sum(p_loc, axis=-1, keepdims=True)
                o = o + jnp.dot(p_loc.astype(bf16), vals[:n_loc], preferred_element_type=f32)
            o = o * (1.0 / d)
            for ci in range(groups_per_kv):
                c = groups_per_kv * h + ci
                o0 = o[2 * CHUNK * ci:2 * CHUNK * ci + CHUNK]
                o1 = o[2 * CHUNK * ci + CHUNK:2 * CHUNK * (ci + 1)]
                o_ref[0, blk(qi), LANES * c:LANES * (c + 1)] = jnp.where(first_half, o0, o1).astype(o_ref.dtype)


def _attention(z, zc, lay, att_tabs, sink, local):
    B, T, _ = z.shape
    L = zc.shape[1]
    nb = T // CHUNK
    qb = _chunks_per_step(T)
    rows = qb * CHUNK
    qs, qw = lay['att_q']
    ks, kw = lay['att_k']
    vs, _ = lay['att_v']
    kj, vj = ks // kw, vs // kw
    prev = lambda n: jnp.maximum(n * qb - 1, 0)
    nxt = lambda n: jnp.minimum(n * qb + qb, nb - 1)
    one = lambda f: pl.BlockSpec((CHUNK, LANES), lambda b, n: (f(n), 0))
    cur = pl.BlockSpec((rows, LANES), lambda b, n: (n, 0))
    in_specs = [pl.BlockSpec(memory_space=pltpu.SMEM),
                pl.BlockSpec((1, rows, qw), lambda b, n: (b, n, qs // qw))]
    args = [sink, z]
    if local:
        for j in (kj, vj):
            in_specs += [pl.BlockSpec((1, CHUNK, kw), lambda b, n, j=j: (b, prev(n), j)),
                         pl.BlockSpec((1, rows, kw), lambda b, n, j=j: (b, n, j)),
                         pl.BlockSpec((1, CHUNK, kw), lambda b, n, j=j: (b, nxt(n), j))]
        args += [z] * 6
        in_specs += [cur, cur, one(prev), one(prev), cur, cur, one(nxt), one(nxt)]
        args += [att_tabs['qa'], att_tabs['qb'], att_tabs['ka'], att_tabs['kb'], att_tabs['ka'], att_tabs['kb'],
                 att_tabs['ka'], att_tabs['kb']]
    else:
        in_specs += [pl.BlockSpec((1, LANES), lambda b, n: (0, 0))]
        args += [att_tabs['gq']]
    in_specs += [pl.BlockSpec((1, L, kw), lambda b, n: (b, 0, kj)),
                 pl.BlockSpec((1, L, kw), lambda b, n: (b, 0, vj)),
                 pl.BlockSpec((1, LANES), lambda b, n: (0, 0)),
                 pl.BlockSpec((LANES, LANES), lambda b, n: (0, 0))]
    args += [zc, zc, att_tabs['gk'], att_tabs['ones']]
    return pl.pallas_call(
        functools.partial(_att_body, local, qb),
        grid=(B, nb // qb),
        in_specs=in_specs,
        out_specs=pl.BlockSpec((1, rows, qw), lambda b, n: (b, n, 0)),
        out_shape=jax.ShapeDtypeStruct((B, T, qw), bf16),
        compiler_params=_params("parallel", "parallel"),
        name="attention" if local else "ctx_attention",
    )(*args)


def _merge_body(n_gate_blocks, ret_ref, sgu_ref, att_ref, *refs):
    g_refs = refs[:N_BRANCHES * n_gate_blocks]
    x_ref, wb_ref, wo_ref, gate_ref, scale_ref, shift_ref, wr_ref, xo_ref, h2_ref, aff_ref = refs[len(g_refs):]
    merged = None
    for i, br_ref in enumerate((ret_ref, sgu_ref, att_ref)):
        p = jnp.dot(br_ref[0], wb_ref[i], preferred_element_type=f32)
        g = jnp.concatenate([r[0] for r in g_refs[n_gate_blocks * i:n_gate_blocks * (i + 1)]], axis=1)
        t = jax.nn.sigmoid(g.astype(f32)) * p
        merged = t if merged is None else merged + t
    o = jnp.dot(merged.astype(bf16), wo_ref[...], preferred_element_type=f32)
    x = x_ref[0] + gate_ref[0] * o
    xo_ref[0] = x
    y = x * lax.rsqrt(jnp.mean(x * x, axis=-1, keepdims=True) + EPS)
    h2 = y * scale_ref[0] + shift_ref[0]
    h2_ref[0, :, 0, :] = h2
    logits = jnp.dot(h2.astype(bf16), wr_ref[...], preferred_element_type=f32)
    logits = jnp.where(_lane(logits.shape) < N_EXPERTS, logits, NEG_INF)
    m = jnp.max(logits, axis=-1, keepdims=True)
    e = jnp.exp(logits - m)
    aff_ref[0] = e / jnp.sum(e, axis=-1, keepdims=True)


def _merge(ret, sgu, att, z, gates_at, x, wb, wo, gate, scale, shift, wr):
    B, T, D = x.shape
    bw = ret.shape[-1]
    tm = _pick_tile(T, 512)
    tok = lambda b, i: (b, i, 0)
    vec = lambda b, i: (b, 0, 0)
    gw = 2 * LANES
    assert gates_at % gw == 0 and D % gw == 0
    n_gate_blocks = D // gw
    gate_specs = [pl.BlockSpec((1, tm, gw), lambda b, i, j=gates_at // gw + k: (b, i, j))
                  for k in range(N_BRANCHES * n_gate_blocks)]
    return pl.pallas_call(
        functools.partial(_merge_body, n_gate_blocks),
        grid=(B, T // tm),
        in_specs=[
            pl.BlockSpec((1, tm, bw), tok), pl.BlockSpec((1, tm, bw), tok), pl.BlockSpec((1, tm, bw), tok),
            *gate_specs,
            pl.BlockSpec((1, tm, D), tok),
            pl.BlockSpec((N_BRANCHES, bw, D), lambda b, i: (0, 0, 0)),
            pl.BlockSpec((D, D), lambda b, i: (0, 0)),
            pl.BlockSpec((1, 1, D), vec), pl.BlockSpec((1, 1, D), vec), pl.BlockSpec((1, 1, D), vec),
            pl.BlockSpec((D, LANES), lambda b, i: (0, 0)),
        ],
        out_specs=[
            pl.BlockSpec((1, tm, D), tok), pl.BlockSpec((1, tm, 1, D), lambda b, i: (b, i, 0, 0)),
            pl.BlockSpec((1, tm, LANES), tok),
        ],
        out_shape=[
            jax.ShapeDtypeStruct((B, T, D), f32),
            jax.ShapeDtypeStruct((B, T, 1, D), f32),
            jax.ShapeDtypeStruct((B, T, LANES), f32),
        ],
        compiler_params=_params("parallel", "parallel"),
        name="merge",
    )(ret, sgu, att, *([z] * len(gate_specs)), x, wb, wo, gate, scale, shift, wr)


ROUTE_TILE = 256
AFF_BITS = 31
MIN_NORMAL_BITS = 0x00800000


def _route_body(cap, aff_ref, tri_ref, pos_ref, post_ref, c0_ref, afft):
    T = aff_ref.shape[1]
    tt = tri_ref.shape[0]
    for j in range(T // tt):
        afft[:, tt * j:tt * (j + 1)] = aff_ref[0, tt * j:tt * (j + 1), :].T[:N_EXPERTS]

    def as_float(bits):
        return pltpu.bitcast(jnp.broadcast_to(bits, (N_EXPERTS, LANES)), f32)[:, :1]

    def count_at_least(v):
        return jnp.sum(jnp.where(afft[...] >= v, 1.0, 0.0), axis=1, keepdims=True)

    def refine(i, thr):
        cand = thr | jnp.left_shift(jnp.int32(1), AFF_BITS - 1 - i)
        return jnp.where(count_at_least(as_float(cand)) >= cap, cand, thr)

    thr = lax.fori_loop(0, AFF_BITS, refine, jnp.zeros((N_EXPERTS, 1), jnp.int32))
    hi_col = as_float(jnp.maximum(thr + 1, MIN_NORMAL_BITS))
    diag = (lax.broadcasted_iota(jnp.int32, (N_EXPERTS, LANES), 0)
            == lax.broadcasted_iota(jnp.int32, (N_EXPERTS, LANES), 1))

    def along_lanes(col):
        return jnp.sum(jnp.where(diag, col, 0.0), axis=0, keepdims=True)

    thr_f = along_lanes(as_float(thr))
    hi_f = along_lanes(hi_col)
    need = along_lanes(cap - count_at_least(hi_col))
    tri = tri_ref[...]
    carry_eq = jnp.zeros((1, LANES), f32)
    carry_sel = jnp.zeros((1, LANES), f32)
    for j in range(T // tt):
        a = aff_ref[0, tt * j:tt * (j + 1), :]
        eq = (a >= thr_f) & (a < hi_f)
        eqf = jnp.where(eq, 1.0, 0.0)
        eq_rank = carry_eq + jnp.dot(tri, eqf.astype(bf16), preferred_element_type=f32) - eqf
        sel = (a >= hi_f) | (eq & (eq_rank < need))
        self_ = jnp.where(sel, 1.0, 0.0)
        pos = carry_sel + jnp.dot(tri, self_.astype(bf16), preferred_element_type=f32) - self_
        pos = jnp.where(sel, pos, -1.0).astype(jnp.int32)
        pos_ref[0, tt * j:tt * (j + 1), :] = pos
        post_ref[0, j] = pos.T[:N_EXPERTS]
        c0_ref[0, j:j + 1, :] = carry_sel.astype(jnp.int32)
        carry_eq = carry_eq + jnp.sum(eqf, axis=0, keepdims=True)
        carry_sel = carry_sel + jnp.sum(self_, axis=0, keepdims=True)


def _route(aff, cap):
    B, T, _ = aff.shape
    tt = min(ROUTE_TILE, T)
    nt = T // tt
    tri = jnp.tril(jnp.ones((tt, tt), bf16))
    return pl.pallas_call(
        functools.partial(_route_body, cap),
        grid=(B,),
        in_specs=[pl.BlockSpec((1, T, LANES), lambda b: (b, 0, 0)), pl.BlockSpec((tt, tt), lambda b: (0, 0))],
        out_specs=[
            pl.BlockSpec((1, T, LANES), lambda b: (b, 0, 0)),
            pl.BlockSpec((1, nt, N_EXPERTS, tt), lambda b: (b, 0, 0, 0)),
            pl.BlockSpec((1, nt, LANES), lambda b: (b, 0, 0)),
        ],
        out_shape=[
            jax.ShapeDtypeStruct((B, T, LANES), jnp.int32),
            jax.ShapeDtypeStruct((B, nt, N_EXPERTS, tt), jnp.int32),
            jax.ShapeDtypeStruct((B, nt, LANES), jnp.int32),
        ],
        scratch_shapes=[pltpu.VMEM((N_EXPERTS, T), f32)],
        compiler_params=_params("parallel"),
        name="route",
    )(aff, tri)


def _slots_body(nt, cap, c0_ref, post_ref, idx_ref):
    b = pl.program_id(0)
    tt = post_ref.shape[-1]
    tok = lax.broadcasted_iota(jnp.int32, (LANES, tt), 1)
    diag = lax.broadcasted_iota(jnp.int32, (LANES, LANES), 0) == lax.broadcasted_iota(jnp.int32, (LANES, LANES), 1)

    def per_expert(e, carry):
        def tiles_starting_at_or_before(slot):
            return sum((c0_ref[(b * nt + j) * N_EXPERTS + e] <= slot).astype(jnp.int32) for j in range(nt))

        for k in range(idx_ref.shape[2]):
            slot = LANES * k + lax.broadcasted_iota(jnp.int32, (LANES, 1), 0)

            def match(j, acc):
                return acc + jnp.where(post_ref[0, j, pl.ds(e, 1), :] == slot, tok + j * tt, 0)

            j_lo = tiles_starting_at_or_before(LANES * k) - 1
            j_hi = tiles_starting_at_or_before(min(LANES * (k + 1), cap) - 1)
            acc = lax.fori_loop(j_lo, j_hi, match, jnp.zeros((LANES, tt), jnp.int32))
            col = jnp.sum(acc.astype(f32), axis=1, keepdims=True)
            row = jnp.sum(jnp.where(diag, col, 0.0), axis=0, keepdims=True)
            idx_ref[0, e, k:k + 1, :] = row.astype(jnp.int32)
        return carry

    lax.fori_loop(0, N_EXPERTS, per_expert, 0)


def _slots(c0s, post, cap):
    B, nt, E, tt = post.shape
    nblk = -(-cap // LANES)
    idx = pl.pallas_call(
        functools.partial(_slots_body, nt, cap),
        grid_spec=pltpu.PrefetchScalarGridSpec(
            num_scalar_prefetch=1,
            grid=(B,),
            in_specs=[pl.BlockSpec((1, nt, E, tt), lambda b, c0: (b, 0, 0, 0))],
            out_specs=pl.BlockSpec((1, E, nblk, LANES), lambda b, c0: (b, 0, 0, 0)),
        ),
        out_shape=jax.ShapeDtypeStruct((B, E, nblk, LANES), jnp.int32),
        compiler_params=_params("arbitrary"),
        name="slots",
    )(c0s, post)
    return idx.reshape(B, E, nblk * LANES)[:, :, :cap]


EXPERT_MIN_ROWS = 256


def _expert_body(cap, spb, idx_ref, h_ref, wg_ref, wu_ref, wd_ref, y_ref, rows, wg_s, wu_s, wd_s, sem):
    n_b = pl.num_programs(1)
    step = pl.program_id(0) * n_b + pl.program_id(1)
    last = pl.num_programs(0) * n_b - 1
    n_rows = spb * cap

    def start_rows(s, buf):
        e, g = s // n_b, s % n_b
        for r in range(n_rows):
            b = g * spb + r // cap
            row = idx_ref[(b * N_EXPERTS + e) * cap + r % cap]
            pltpu.make_async_copy(h_ref.at[row], rows.at[buf, pl.ds(r, 1), :], sem.at[buf]).start()

    def wait_rows(buf):
        pltpu.make_async_copy(h_ref.at[pl.ds(0, n_rows), 0, :], rows.at[buf], sem.at[buf]).wait()

    @pl.when(step == 0)
    def _():
        start_rows(step, 0)

    @pl.when(pl.program_id(1) == 0)
    def _():
        wg_s[...] = wg_ref[0, 0].astype(bf16)
        wu_s[...] = wu_ref[0, 0].astype(bf16)
        wd_s[...] = wd_ref[0, 0].astype(bf16)

    for cur in range(2):
        @pl.when(step % 2 == cur)
        def _():
            wait_rows(cur)
            xs = rows[cur].astype(bf16)

            @pl.when(step < last)
            def _():
                start_rows(step + 1, 1 - cur)

            a = jnp.dot(xs, wg_s[...], preferred_element_type=f32)
            u = jnp.dot(xs, wu_s[...], preferred_element_type=f32)
            hmid = (jax.nn.silu(a) * u).astype(bf16)
            y = jnp.dot(hmid, wd_s[...], preferred_element_type=f32).astype(y_ref.dtype)
            for i in range(spb):
                y_ref[i, 0] = y[cap * i:cap * (i + 1)]


def _expert_ffn(idx, h2, layer, wg, wu, wd):
    rows_of = idx + (jnp.arange(idx.shape[0], dtype=jnp.int32) * h2.shape[1])[:, None, None]
    B, T, _, D = h2.shape
    _, E, _, F = wg.shape
    cap = idx.shape[-1]
    spb = _pick_tile(B, max(1, EXPERT_MIN_ROWS // cap))
    n_rows = spb * cap
    wsp = lambda e, g, idx: (layer, e, 0, 0)
    return pl.pallas_call(
        functools.partial(_expert_body, cap, spb),
        grid_spec=pltpu.PrefetchScalarGridSpec(
            num_scalar_prefetch=1,
            grid=(E, B // spb),
            in_specs=[
                pl.BlockSpec(memory_space=pl.ANY),
                pl.BlockSpec((1, 1, D, F), wsp), pl.BlockSpec((1, 1, D, F), wsp), pl.BlockSpec((1, 1, F, D), wsp),
            ],
            out_specs=pl.BlockSpec((spb, 1, cap, D), lambda e, g, idx: (g, e, 0, 0)),
            scratch_shapes=[pltpu.VMEM((2, n_rows, D), f32),
                            pltpu.VMEM((D, F), bf16), pltpu.VMEM((D, F), bf16), pltpu.VMEM((F, D), bf16),
                            pltpu.SemaphoreType.DMA((2,))],
        ),
        out_shape=jax.ShapeDtypeStruct((B, E, cap, D), bf16),
        compiler_params=pltpu.CompilerParams(dimension_semantics=("arbitrary", "arbitrary"),
                                             vmem_limit_bytes=EXPERT_VMEM_LIMIT_BYTES),
        name="expert_ffn",
    )(rows_of.reshape(-1), h2.reshape(B * T, 1, D), wg, wu, wd)


Y_ROW_ALIGN = 16


def _combine_body(nt, cap, win, c0_ref, pos_ref, aff_ref, x_ref, gate_ref, spread_ref, y_ref, o_ref, buf, extra,
                  sem, sem_x):
    b = pl.program_id(0)
    j = pl.program_id(1)
    step = b * nt + j
    tt = pos_ref.shape[1]
    stride = win if win == cap else win - Y_ROW_ALIGN

    def c0(bb, jj, e):
        return c0_ref[(bb * nt + jj) * N_EXPERTS + e]

    def window_start(lo):
        return pl.multiple_of(jnp.minimum(lo & (-Y_ROW_ALIGN), cap - win), Y_ROW_ALIGN)

    def copies(bb, jj, p, dst, s):
        return [pltpu.make_async_copy(
            y_ref.at[bb, e, pl.ds(window_start(c0(bb, jj, e) + p * stride), win), :],
            dst.at[pl.ds(e * win, win), :], s) for e in range(N_EXPERTS)]

    @pl.when(step == 0)
    def _():
        for cp in copies(b, j, 0, buf.at[0], sem.at[0]):
            cp.start()

    nxt = step + 1

    @pl.when(nxt < pl.num_programs(0) * nt)
    def _():
        for cp in copies(nxt // nt, nxt % nt, 0, buf.at[nxt % 2], sem.at[nxt % 2]):
            cp.start()

    lane = _lane((1, LANES))
    pos = pos_ref[0]
    aff = aff_ref[0]

    def lane_vector(fn):
        v = jnp.zeros((1, LANES), jnp.int32)
        for e in range(N_EXPERTS):
            v = jnp.where(lane == e, fn(e), v)
        return v

    spread = spread_ref[...]
    col_slot = _lane((1, N_EXPERTS * win)) & (win - 1)
    aff_cols = jnp.dot(aff.astype(bf16), spread, preferred_element_type=f32)

    def weights(p):
        lo = lane_vector(lambda e: c0(b, j, e) + p * stride)
        start = lane_vector(lambda e: window_start(c0(b, j, e) + p * stride))
        rel = jnp.where((pos >= lo) & (pos < lo + stride), pos - start, -1).astype(f32)
        rel_cols = jnp.dot(rel.astype(bf16), spread, preferred_element_type=f32)
        return jnp.where(rel_cols == col_slot.astype(f32), aff_cols, 0.0).astype(bf16)

    for cp in copies(b, j, 0, buf.at[step % 2], sem.at[step % 2]):
        cp.wait()
    moe = jnp.dot(weights(0), buf[step % 2], preferred_element_type=f32)

    def count(e):
        end = jnp.where(j + 1 < nt, c0(b, jnp.minimum(j + 1, nt - 1), e), cap)
        return end - c0(b, j, e)

    max_count = functools.reduce(jnp.maximum, [count(e) for e in range(N_EXPERTS)])

    def more(p, acc):
        cps = copies(b, j, p, extra, sem_x.at[0])
        for cp in cps:
            cp.start()
        for cp in cps:
            cp.wait()
        return acc + jnp.dot(weights(p), extra[...], preferred_element_type=f32)

    moe = lax.fori_loop(1, (max_count + stride - 1) // stride, more, moe)
    o_ref[0] = x_ref[0] + gate_ref[0] * moe


def _combine(c0s, pos, aff, x, gate, y):
    B, T, D = x.shape
    E, cap = y.shape[1], y.shape[2]
    tt = min(ROUTE_TILE, T)
    nt = T // tt
    win = min(cap, LANES // 2)
    assert (cap - win) % Y_ROW_ALIGN == 0 and win & (win - 1) == 0
    spread = (jnp.arange(LANES)[:, None] == jnp.arange(E * win)[None, :] // win).astype(bf16)
    tok = lambda b, j, c0: (b, j, 0)
    return pl.pallas_call(
        functools.partial(_combine_body, nt, cap, win),
        grid_spec=pltpu.PrefetchScalarGridSpec(
            num_scalar_prefetch=1,
            grid=(B, nt),
            in_specs=[
                pl.BlockSpec((1, tt, LANES), tok), pl.BlockSpec((1, tt, LANES), tok), pl.BlockSpec((1, tt, D), tok),
                pl.BlockSpec((1, 1, D), lambda b, j, c0: (b, 0, 0)),
                pl.BlockSpec((LANES, E * win), lambda b, j, c0: (0, 0)),
                pl.BlockSpec(memory_space=pl.ANY),
            ],
            out_specs=pl.BlockSpec((1, tt, D), tok),
            scratch_shapes=[pltpu.VMEM((2, E * win, D), bf16), pltpu.VMEM((E * win, D), bf16),
                            pltpu.SemaphoreType.DMA((2,)), pltpu.SemaphoreType.DMA((1,))],
        ),
        out_shape=jax.ShapeDtypeStruct((B, T, D), f32),
        compiler_params=_params("arbitrary", "arbitrary"),
        name="combine",
    )(c0s, pos, aff, x, gate, spread, y)


def _expert_choice(h2, aff, x, gate2, p):
    T = x.shape[1]
    cap = CAPACITY_FACTOR * T // N_EXPERTS
    pos, post, c0 = _route(aff, cap)
    c0s = c0[:, :, :N_EXPERTS].reshape(-1)
    y = _expert_ffn(_slots(c0s, post, cap), h2, p['layer'], p['w_exp_gate'], p['w_exp_up'], p['w_exp_down'])
    return _combine(c0s, pos, aff, x, gate2, y)


def _rope_tables(T):
    rows = T // GRID_W
    row = jnp.repeat(jnp.arange(rows), GRID_W).astype(f32)
    col = jnp.tile(jnp.arange(GRID_W), rows).astype(f32)
    n_freq = HEAD_DIM // 4
    inv = jnp.power(ROPE_BASE, -jnp.arange(n_freq, dtype=f32) / n_freq)
    ang = jnp.concatenate([row[:, None] * inv, col[:, None] * inv], axis=-1)
    cos, sin = jnp.cos(ang), jnp.sin(ang)
    reps = LANES // HEAD_DIM
    return jnp.tile(jnp.concatenate([cos, cos], axis=-1), (1, reps)), jnp.tile(
        jnp.concatenate([-sin, sin], axis=-1), (1, reps))


def _identity_rope(T):
    return jnp.ones((T, LANES), f32), jnp.zeros((T, LANES), f32)


def _retention_tables(decay_fwd, decay_bwd):
    lgf = jax.nn.log_sigmoid(decay_fwd.astype(f32))
    lgb = jax.nn.log_sigmoid(decay_bwd.astype(f32))
    C = CHUNK
    k_scale = HEAD_DIM ** -0.5
    idx = jnp.arange(C, dtype=f32)
    npair = RET_HEADS // 2

    def lanes(t):
        return jnp.repeat(t.reshape(npair, 2, C).transpose(0, 2, 1), HEAD_DIM, axis=-1)

    diff = idx[:, None] - idx[None, :]
    dm = jnp.where(diff >= 0, jnp.exp(jnp.maximum(diff, 0.0)[None] * lgf[:, None, None]),
                   jnp.exp(jnp.maximum(-diff, 0.0)[None] * lgb[:, None, None])) * k_scale

    def chunk_decay(lg):
        rows = jnp.repeat(jnp.exp(C * lg).reshape(npair, 2), HEAD_DIM, axis=-1)
        return jnp.broadcast_to(rows[:, :, None], (npair, LANES, 2 * LANES))

    r = jnp.arange(LANES)[:, None] // HEAD_DIM
    c = jnp.arange(2 * LANES)[None, :] // LANES
    return {
        'wkf': lanes(jnp.exp((C - 1 - idx)[None, :] * lgf[:, None])) * k_scale,
        'wkb': lanes(jnp.exp(idx[None, :] * lgb[:, None])) * k_scale,
        'qdf': lanes(jnp.exp((idx + 1.0)[None, :] * lgf[:, None])),
        'qdb': lanes(jnp.exp((C - idx)[None, :] * lgb[:, None])),
        'dm': dm, 'cdf': chunk_decay(lgf), 'cdb': chunk_decay(lgb), 'bm': (r == c).astype(f32),
    }


def _attention_tables(cos, sin, q_gain, k_gain):
    reps = LANES // HEAD_DIM
    half = HEAD_DIM // 2
    swap = lambda g: jnp.concatenate([g[half:], g[:half]])
    gq, gk = jnp.tile(q_gain, reps)[None], jnp.tile(k_gain, reps)[None]
    gqs, gks = jnp.tile(swap(q_gain), reps)[None], jnp.tile(swap(k_gain), reps)[None]
    r = jnp.arange(LANES) // HEAD_DIM
    return {'qa': cos * gq, 'qb': sin * gqs, 'ka': cos * gk, 'kb': sin * gks, 'gq': gq, 'gk': gk,
            'ones': (r[:, None] == r[None, :]).astype(bf16)}


def _layer(x, xc, mod_lat, mod_ctx, rope, p, last):
    B, S, D = x.shape
    L = xc.shape[1]
    lay = _layout(D)
    sh1, sc1, g1, sh2, sc2, g2 = jnp.split(mod_lat, 6, axis=-1)
    csh1, csc1, cg1, csh2, csc2, cg2 = jnp.split(mod_ctx, 6, axis=-1)
    n1 = p['norm1_g'][None, None, :]
    n2 = p['norm2_g'][None, None, :]

    def bcast(t):
        return jnp.broadcast_to(t, (B, 1, D))

    z = _proj_in(x, n1 * (1 + sc1), sh1, p['w_in'])
    zc = _proj_in(xc, bcast(n1 * (1 + csc1)), bcast(csh1), p['w_in'])

    cos, sin = rope
    cos_c, sin_c = _identity_rope(L)
    rtabs = _retention_tables(p['ret_decay_fwd'], p['ret_decay_bwd'])
    atabs = _attention_tables(cos, sin, p['q_norm_g'], p['k_norm_g'])
    sgu_w = p['sgu_w'].astype(bf16)
    gw = D // 2 // SGU_GROUPS
    sgu_bias = jnp.repeat(p['sgu_b'].T, gw, axis=-1)

    zero = jnp.zeros((B, RET_HEADS // 2, LANES, 2 * LANES), f32)
    sfc, sbc, s_f, s_b = _ret_states(zc, lay, cos_c, sin_c, rtabs, zero, zero)
    sf, sb, _, _ = _ret_states(z, lay, cos, sin, rtabs, s_f, s_b)
    ret, sgu = _ret_sgu(z, lay, cos, sin, sf, sb, rtabs, sgu_w, sgu_bias)
    att = _attention(z, zc, lay, atabs, p['attn_sink'], True)

    x, h2, aff = _merge(ret, sgu, att, z, lay['gates'][0], x, p['w_branch'], p['w_out'], g1, n2 * (1 + sc2), sh2,
                        p['w_router'])
    x = _expert_choice(h2, aff, x, g2, p)
    if last:
        return x, None

    ret_c, sgu_c = _ret_sgu(zc, lay, cos_c, sin_c, sfc, sbc, rtabs, sgu_w, sgu_bias)
    att_c = _attention(zc, zc, lay, atabs, p['attn_sink'], False)
    xc, h2c, affc = _merge(ret_c, sgu_c, att_c, zc, lay['gates'][0], xc, p['w_branch'], p['w_out'], bcast(cg1),
                           bcast(n2 * (1 + csc2)), bcast(csh2), p['w_router'])
    xc = _expert_choice(h2c, affc, xc, bcast(cg2), p)
    return x, xc


def kernel(x, c, ctx, c_ctx, w_mod, b_mod, norm1_g, norm2_g, w_in, ret_decay_fwd, ret_decay_bwd, sgu_w, sgu_b,
           q_norm_g, k_norm_g, attn_sink, w_branch, w_out, w_router, w_exp_gate, w_exp_up, w_exp_down):
    B, S, D = x.shape
    depth = w_in.shape[0]
    rope = _rope_tables(S)
    c_act = jax.nn.silu(c)
    c_ctx_act = jax.nn.silu(c_ctx)
    xc = ctx
    hp = lax.Precision.HIGHEST
    for l in range(depth):
        wr = jnp.zeros((D, LANES), f32).at[:, :N_EXPERTS].set(w_router[l]).astype(bf16)
        p = {
            'norm1_g': norm1_g[l], 'norm2_g': norm2_g[l], 'w_in': w_in[l].astype(bf16),
            'ret_decay_fwd': ret_decay_fwd[l], 'ret_decay_bwd': ret_decay_bwd[l],
            'sgu_w': sgu_w[l], 'sgu_b': sgu_b[l],
            'q_norm_g': q_norm_g[l], 'k_norm_g': k_norm_g[l], 'attn_sink': attn_sink[l],
            'w_branch': w_branch[l].astype(bf16), 'w_out': w_out[l].astype(bf16), 'w_router': wr,
            'layer': l, 'w_exp_gate': w_exp_gate, 'w_exp_up': w_exp_up, 'w_exp_down': w_exp_down,
        }
        mod = jnp.dot(jnp.concatenate([c_act, c_ctx_act[None]], axis=0), w_mod[l], precision=hp) + b_mod[l]
        mod_lat, mod_ctx = mod[:B, None, :], mod[B:, None, :]
        x, xc = _layer(x, xc, mod_lat, mod_ctx, rope, p, l == depth - 1)
    return x
```

```python
import functools

import jax
import jax.numpy as jnp
from jax import lax
from jax.experimental import pallas as pl
from jax.experimental.pallas import tpu as pltpu

GRID_W = 64
N_BRANCHES = 3
RET_HEADS = 4
SGU_GROUPS = 4
CHUNK = 128
HEAD_DIM = 64
ATT_KV_HEADS = 2
ROPE_BASE = 10000.0
N_EXPERTS = 16
CAPACITY_FACTOR = 2
EPS = 1e-6
NEG_INF = -1e30

LANES = 128
VMEM_LIMIT_BYTES = 48 * 1024 * 1024
EXPERT_VMEM_LIMIT_BYTES = 56 * 1024 * 1024

f32 = jnp.float32
bf16 = jnp.bfloat16


def _layout(d_model):
    bw = d_model // 2
    qk = RET_HEADS * HEAD_DIM
    kv = ATT_KV_HEADS * HEAD_DIM
    widths = (('ret_q', qk), ('ret_k', qk), ('ret_v', bw), ('ret_g', bw), ('sgu_u', bw), ('sgu_v', bw),
              ('att_q', bw), ('att_k', kv), ('att_v', kv), ('gates', N_BRANCHES * d_model))
    out, start = {}, 0
    for name, w in widths:
        out[name] = (start, w)
        start += w
    return out


def _chunks_per_step(T):
    return next(c for c in (4, 2, 1) if (T // CHUNK) % c == 0)


def _pick_tile(n, pref):
    t = min(n, pref)
    while n % t:
        t //= 2
    return t


def _params(*sem):
    return pltpu.CompilerParams(dimension_semantics=sem, vmem_limit_bytes=VMEM_LIMIT_BYTES)


PROJ_NORM_ROWS = 256


def _proj_in_body(x_ref, scale_ref, shift_ref, w_ref, o_ref, h_ref):
    @pl.when(pl.program_id(2) == 0)
    def _():
        step = min(PROJ_NORM_ROWS, h_ref.shape[0])
        for r in range(0, h_ref.shape[0], step):
            x = x_ref[0, r:r + step, :]
            y = x * lax.rsqrt(jnp.mean(x * x, axis=-1, keepdims=True) + EPS)
            h = (y * scale_ref[0] + shift_ref[0]).astype(bf16)
            h_ref[r:r + step, :] = h
            o_ref[0, r:r + step, :] = jnp.dot(h, w_ref[...], preferred_element_type=f32).astype(o_ref.dtype)

    @pl.when(pl.program_id(2) != 0)
    def _():
        o_ref[0] = jnp.dot(h_ref[...], w_ref[...], preferred_element_type=f32).astype(o_ref.dtype)


def _proj_in(x, scale, shift, w):
    B, T, D = x.shape
    N = w.shape[1]
    tm = _pick_tile(T, 2048)
    tn = 1280 if N % 1280 == 0 else _pick_tile(N, 1024)
    return pl.pallas_call(
        _proj_in_body,
        grid=(B, T // tm, N // tn),
        in_specs=[
            pl.BlockSpec((1, tm, D), lambda b, i, j: (b, i, 0)),
            pl.BlockSpec((1, 1, D), lambda b, i, j: (b, 0, 0)),
            pl.BlockSpec((1, 1, D), lambda b, i, j: (b, 0, 0)),
            pl.BlockSpec((D, tn), lambda b, i, j: (0, j)),
        ],
        out_specs=pl.BlockSpec((1, tm, tn), lambda b, i, j: (b, i, j)),
        out_shape=jax.ShapeDtypeStruct((B, T, N), bf16),
        scratch_shapes=[pltpu.VMEM((tm, D), bf16)],
        compiler_params=_params("parallel", "parallel", "arbitrary"),
        name="proj_in",
    )(x, scale, shift, w)


def _lane(shape):
    return lax.broadcasted_iota(jnp.int32, shape, 1)


def _swap_halves(x):
    return jnp.where((_lane(x.shape) & (HEAD_DIM // 2)) == 0,
                     pltpu.roll(x, LANES - HEAD_DIM // 2, 1), pltpu.roll(x, HEAD_DIM // 2, 1))


def _rope(x, cos, sin_signed):
    return x * cos + _swap_halves(x) * sin_signed


def _head_sumsq(x, ones_blockdiag):
    x2 = x * x
    hi = x2.astype(bf16)
    lo = (x2 - hi.astype(f32)).astype(bf16)
    return (jnp.dot(hi, ones_blockdiag, preferred_element_type=f32)
            + jnp.dot(lo, ones_blockdiag, preferred_element_type=f32))


def _dot_nt(a, b):
    return lax.dot_general(a, b, (((1,), (1,)), ((), ())), preferred_element_type=f32)


def _dot_tn(a, b):
    return lax.dot_general(a, b, (((0,), (0,)), ((), ())), preferred_element_type=f32)


def _ret_state_body(cpb, kf_ref, vf_ref, kb_ref, vb_ref, cosf_ref, sinf_ref, cosb_ref, sinb_ref,
                    wkf_ref, wkb_ref, cdf_ref, cdb_ref, bm_ref, s0f_ref, s0b_ref,
                    sf_ref, sb_ref, ff_ref, fb_ref, accf, accb):
    n = pl.program_id(1)

    @pl.when(n == 0)
    def _():
        accf[...] = s0f_ref[0]
        accb[...] = s0b_ref[0]

    def one_chunk(c, k_ref, v_ref, cos_ref, sin_ref, wk_ref, cd_ref, out_ref, acc):
        rows = slice(CHUNK * c, CHUNK * (c + 1))
        k = k_ref[0, rows, :].astype(f32)
        v = v_ref[0, rows, :]
        for p in range(RET_HEADS // 2):
            kp = _rope(k[:, LANES * p:LANES * (p + 1)], cos_ref[rows, :], sin_ref[rows, :])
            kw = (kp * wk_ref[p]).astype(bf16)
            upd = _dot_tn(kw, v[:, 2 * LANES * p:2 * LANES * (p + 1)])
            s = acc[p]
            out_ref[0, c, p] = s.astype(out_ref.dtype)
            acc[p] = cd_ref[p] * s + bm_ref[...] * upd

    for c in range(cpb):
        one_chunk(c, kf_ref, vf_ref, cosf_ref, sinf_ref, wkf_ref, cdf_ref, sf_ref, accf)
        one_chunk(cpb - 1 - c, kb_ref, vb_ref, cosb_ref, sinb_ref, wkb_ref, cdb_ref, sb_ref, accb)

    @pl.when(n == pl.num_programs(1) - 1)
    def _():
        ff_ref[0] = accf[...]
        fb_ref[0] = accb[...]


def _ret_states(z, lay, cos, sin, tabs, s0f, s0b):
    B, T, _ = z.shape
    cpb = _chunks_per_step(T)
    rows = cpb * CHUNK
    nc = T // rows
    kq = lay['ret_k'][0] // (RET_HEADS * HEAD_DIM)
    vq = lay['ret_v'][0] // lay['ret_v'][1]
    vw = lay['ret_v'][1]
    kw = RET_HEADS * HEAD_DIM
    npair = RET_HEADS // 2
    st_shape = (npair, LANES, 2 * LANES)
    const3 = lambda b, n: (0, 0, 0)
    return pl.pallas_call(
        functools.partial(_ret_state_body, cpb),
        grid=(B, nc),
        in_specs=[
            pl.BlockSpec((1, rows, kw), lambda b, n: (b, n, kq)),
            pl.BlockSpec((1, rows, vw), lambda b, n: (b, n, vq)),
            pl.BlockSpec((1, rows, kw), lambda b, n: (b, nc - 1 - n, kq)),
            pl.BlockSpec((1, rows, vw), lambda b, n: (b, nc - 1 - n, vq)),
            pl.BlockSpec((rows, LANES), lambda b, n: (n, 0)),
            pl.BlockSpec((rows, LANES), lambda b, n: (n, 0)),
            pl.BlockSpec((rows, LANES), lambda b, n: (nc - 1 - n, 0)),
            pl.BlockSpec((rows, LANES), lambda b, n: (nc - 1 - n, 0)),
            pl.BlockSpec((npair, CHUNK, LANES), const3),
            pl.BlockSpec((npair, CHUNK, LANES), const3),
            pl.BlockSpec(st_shape, const3),
            pl.BlockSpec(st_shape, const3),
            pl.BlockSpec((LANES, 2 * LANES), lambda b, n: (0, 0)),
            pl.BlockSpec((1,) + st_shape, lambda b, n: (b, 0, 0, 0)),
            pl.BlockSpec((1,) + st_shape, lambda b, n: (b, 0, 0, 0)),
        ],
        out_specs=[
            pl.BlockSpec((1, cpb) + st_shape, lambda b, n: (b, n, 0, 0, 0)),
            pl.BlockSpec((1, cpb) + st_shape, lambda b, n: (b, nc - 1 - n, 0, 0, 0)),
            pl.BlockSpec((1,) + st_shape, lambda b, n: (b, 0, 0, 0)),
            pl.BlockSpec((1,) + st_shape, lambda b, n: (b, 0, 0, 0)),
        ],
        out_shape=[
            jax.ShapeDtypeStruct((B, nc * cpb) + st_shape, bf16),
            jax.ShapeDtypeStruct((B, nc * cpb) + st_shape, bf16),
            jax.ShapeDtypeStruct((B,) + st_shape, f32),
            jax.ShapeDtypeStruct((B,) + st_shape, f32),
        ],
        scratch_shapes=[pltpu.VMEM(st_shape, f32), pltpu.VMEM(st_shape, f32)],
        compiler_params=_params("parallel", "arbitrary"),
        name="ret_states",
    )(z, z, z, z, cos, sin, cos, sin, tabs['wkf'], tabs['wkb'], tabs['cdf'], tabs['cdb'], tabs['bm'], s0f, s0b)


def _ret_sgu_body(cpb, *refs):
    for c in range(cpb):
        _ret_sgu_chunk(c, *refs)


def _ret_sgu_chunk(c, q_ref, k_ref, v_ref, g_ref, u_ref, w_ref, cos_ref, sin_ref, sf_ref, sb_ref,
                   dm_ref, qdf_ref, qdb_ref, sw_ref, sbias_ref, ret_ref, sgu_ref):
    rows = slice(CHUNK * c, CHUNK * (c + 1))
    q = q_ref[0, rows, :].astype(f32)
    k = k_ref[0, rows, :].astype(f32)
    v = v_ref[0, rows, :]
    g = g_ref[0, rows, :].astype(f32)
    cos = cos_ref[rows, :]
    sin = sin_ref[rows, :]
    first_half = _lane((CHUNK, LANES)) < HEAD_DIM
    for p in range(RET_HEADS // 2):
        qp = _rope(q[:, LANES * p:LANES * (p + 1)], cos, sin)
        kp = _rope(k[:, LANES * p:LANES * (p + 1)], cos, sin).astype(bf16)
        cross = (jnp.dot((qp * qdf_ref[p]).astype(bf16), sf_ref[0, c, p], preferred_element_type=f32)
                 + jnp.dot((qp * qdb_ref[p]).astype(bf16), sb_ref[0, c, p], preferred_element_type=f32))
        for a in range(2):
            h = 2 * p + a
            qm = jnp.where(first_half if a == 0 else ~first_half, qp, 0.0).astype(bf16)
            sc = (_dot_nt(qm, kp) * dm_ref[h]).astype(bf16)
            o = jnp.dot(sc, v[:, LANES * h:LANES * (h + 1)], preferred_element_type=f32)
            o = o + cross[:, LANES * a:LANES * (a + 1)]
            oc = o - jnp.mean(o, axis=-1, keepdims=True)
            y = oc * lax.rsqrt(jnp.mean(oc * oc, axis=-1, keepdims=True) + EPS)
            ret_ref[0, rows, LANES * h:LANES * (h + 1)] = (
                y * jax.nn.silu(g[:, LANES * h:LANES * (h + 1)])).astype(ret_ref.dtype)

    u = jax.nn.gelu(u_ref[0, rows, :].astype(f32))
    w = jax.nn.gelu(w_ref[0, rows, :].astype(f32))
    wc = w - jnp.mean(w, axis=-1, keepdims=True)
    wn = (wc * lax.rsqrt(jnp.mean(wc * wc, axis=-1, keepdims=True) + EPS)).astype(bf16)
    gw = wn.shape[-1] // SGU_GROUPS
    for i in range(SGU_GROUPS):
        sl = slice(gw * i, gw * (i + 1))
        mixed = jnp.dot(sw_ref[i], wn[:, sl], preferred_element_type=f32) + sbias_ref[:, sl]
        sgu_ref[0, rows, sl] = (u[:, sl] * mixed).astype(sgu_ref.dtype)


def _ret_sgu(z, lay, cos, sin, sf, sb, tabs, sgu_w, sgu_bias):
    B, T, _ = z.shape
    cpb = _chunks_per_step(T)
    rows = cpb * CHUNK
    nc = T // rows
    bw = lay['ret_v'][1]

    def zs(name):
        s, w = lay[name]
        return pl.BlockSpec((1, rows, w), lambda b, n, j=s // w: (b, n, j))

    st_shape = (RET_HEADS // 2, LANES, 2 * LANES)
    const3 = lambda b, n: (0, 0, 0)
    out_spec = pl.BlockSpec((1, rows, bw), lambda b, n: (b, n, 0))
    return pl.pallas_call(
        functools.partial(_ret_sgu_body, cpb),
        grid=(B, nc),
        in_specs=[
            zs('ret_q'), zs('ret_k'), zs('ret_v'), zs('ret_g'), zs('sgu_u'), zs('sgu_v'),
            pl.BlockSpec((rows, LANES), lambda b, n: (n, 0)),
            pl.BlockSpec((rows, LANES), lambda b, n: (n, 0)),
            pl.BlockSpec((1, cpb) + st_shape, lambda b, n: (b, n, 0, 0, 0)),
            pl.BlockSpec((1, cpb) + st_shape, lambda b, n: (b, n, 0, 0, 0)),
            pl.BlockSpec((RET_HEADS, CHUNK, CHUNK), const3),
            pl.BlockSpec((RET_HEADS // 2, CHUNK, LANES), const3),
            pl.BlockSpec((RET_HEADS // 2, CHUNK, LANES), const3),
            pl.BlockSpec((SGU_GROUPS, CHUNK, CHUNK), const3),
            pl.BlockSpec((CHUNK, bw), lambda b, n: (0, 0)),
        ],
        out_specs=[out_spec, out_spec],
        out_shape=[jax.ShapeDtypeStruct((B, T, bw), bf16), jax.ShapeDtypeStruct((B, T, bw), bf16)],
        compiler_params=_params("parallel", "parallel"),
        name="ret_sgu",
    )(z, z, z, z, z, z, cos, sin, sf, sb, tabs['dm'], tabs['qdf'], tabs['qdb'], sgu_w, sgu_bias)


def _att_body(local, qb, sink_ref, q_ref, *refs):
    if local:
        (kp_ref, kc_ref, kn_ref, vp_ref, vc_ref, vn_ref, qa_ref, qb_ref,
         kap_ref, kbp_ref, kac_ref, kbc_ref, kan_ref, kbn_ref, ck_ref, cv_ref, gk_ref, ones_ref, o_ref) = refs
    else:
        qa_ref, ck_ref, cv_ref, gk_ref, ones_ref, o_ref = refs
    n = pl.program_id(1)
    n_blocks = pl.num_programs(1) * qb
    ones = ones_ref[...]
    inv_hd = 1.0 / HEAD_DIM

    def norm_rope(x_ref_val, ta, tb):
        x = x_ref_val.astype(f32)
        r = lax.rsqrt(_head_sumsq(x, ones) * inv_hd + EPS)
        y = x * ta if tb is None else x * ta + _swap_halves(x) * tb
        return y * r

    def dup(x):
        lo = _lane(x.shape) < HEAD_DIM
        xr = pltpu.roll(x, HEAD_DIM, 1)
        return jnp.where(lo, x, xr).astype(bf16), jnp.where(lo, xr, x).astype(bf16)

    def blk(i):
        return slice(CHUNK * i, CHUNK * (i + 1))

    k_ctx = dup(norm_rope(ck_ref[0], gk_ref[...], None))
    v_ctx = dup(cv_ref[0].astype(f32))
    if local:
        k_loc = ([dup(norm_rope(kp_ref[0], kap_ref[...], kbp_ref[...]))]
                 + [dup(norm_rope(kc_ref[0, blk(i), :], kac_ref[blk(i), :], kbc_ref[blk(i), :])) for i in range(qb)]
                 + [dup(norm_rope(kn_ref[0], kan_ref[...], kbn_ref[...]))])
        v_loc = ([dup(vp_ref[0].astype(f32))] + [dup(vc_ref[0, blk(i), :].astype(f32)) for i in range(qb)]
                 + [dup(vn_ref[0].astype(f32))])
    n_loc = 3 * CHUNK if local else 0

    n_groups = q_ref.shape[-1] // LANES
    groups_per_kv = n_groups // ATT_KV_HEADS
    rows = 2 * groups_per_kv * CHUNK
    log2e = 1.4426950408889634
    first_half = _lane((CHUNK, LANES)) < HEAD_DIM
    for qi in range(qb):
        if local:
            i = lax.broadcasted_iota(jnp.int32, (CHUNK, n_loc), 0)
            j = lax.broadcasted_iota(jnp.int32, (CHUNK, n_loc), 1)
            has_prev = (n * qb + qi > 0).astype(jnp.int32)
            has_next = (n * qb + qi < n_blocks - 1).astype(jnp.int32)
            valid = (j >= i * has_prev + CHUNK * (1 - has_prev)) & (j <= 2 * CHUNK - 1 + has_next * (i + 1))
            bias = jnp.where(valid, 0.0, NEG_INF)
            bias = jnp.concatenate([bias] * (rows // CHUNK), axis=0)
        for h in range(ATT_KV_HEADS):
            keys = ([k_loc[qi + d][h] for d in range(3)] if local else []) + [k_ctx[h]]
            vals = ([v_loc[qi + d][h] for d in range(3)] if local else []) + [v_ctx[h]]
            keys = jnp.concatenate(keys, axis=0) if local else keys[0]
            vals = jnp.concatenate(vals, axis=0) if local else vals[0]
            q_rows, sink_rows = [], []
            for c in range(groups_per_kv * h, groups_per_kv * (h + 1)):
                xq = q_ref[0, blk(qi), LANES * c:LANES * (c + 1)]
                qn = norm_rope(xq, qa_ref[blk(qi), :] if local else qa_ref[...],
                               qb_ref[blk(qi), :] if local else None) * (HEAD_DIM ** -0.5 * log2e)
                for a in range(2):
                    q_rows.append(jnp.where(first_half if a == 0 else ~first_half, qn, 0.0).astype(bf16))
                    sink_rows.append(jnp.full((CHUNK, 1), sink_ref[2 * c + a] * log2e, f32))
            s = _dot_nt(jnp.concatenate(q_rows, axis=0), keys)
            sk = jnp.concatenate(sink_rows, axis=0)
            s_ctx = s[:, n_loc:]
            m = jnp.maximum(jnp.max(s_ctx, axis=-1, keepdims=True), sk)
            if local:
                s_loc = s[:, :n_loc] + bias
                m = jnp.maximum(m, jnp.max(s_loc, axis=-1, keepdims=True))
            p_ctx = jnp.exp2(s_ctx - m)
            d = jnp.sum(p_ctx, axis=-1, keepdims=True) + jnp.exp2(sk - m)
            o = jnp.dot(p_ctx.astype(bf16), vals[n_loc:], preferred_element_type=f32)
            if local:
                p_loc = jnp.exp2(s_loc - m)
                d = d + jnp.sum(p_loc, axis=-1, keepdims=True)
                o = o + jnp.dot(p_loc.astype(bf16), vals[:n_loc], preferred_element_type=f32)
            o = o * (1.0 / d)
            for ci in range(groups_per_kv):
                c = groups_per_kv * h + ci
                o0 = o[2 * CHUNK * ci:2 * CHUNK * ci + CHUNK]
                o1 = o[2 * CHUNK * ci + CHUNK:2 * CHUNK * (ci + 1)]
                o_ref[0, blk(qi), LANES * c:LANES * (c + 1)] = jnp.where(first_half, o0, o1).astype(o_ref.dtype)


def _attention(z, zc, lay, att_tabs, sink, local):
    B, T, _ = z.shape
    L = zc.shape[1]
    nb = T // CHUNK
    qb = _chunks_per_step(T)
    rows = qb * CHUNK
    qs, qw = lay['att_q']
    ks, kw = lay['att_k']
    vs, _ = lay['att_v']
    kj, vj = ks // kw, vs // kw
    prev = lambda n: jnp.maximum(n * qb - 1, 0)
    nxt = lambda n: jnp.minimum(n * qb + qb, nb - 1)
    one = lambda f: pl.BlockSpec((CHUNK, LANES), lambda b, n: (f(n), 0))
    cur = pl.BlockSpec((rows, LANES), lambda b, n: (n, 0))
    in_specs = [pl.BlockSpec(memory_space=pltpu.SMEM),
                pl.BlockSpec((1, rows, qw), lambda b, n: (b, n, qs // qw))]
    args = [sink, z]
    if local:
        for j in (kj, vj):
            in_specs += [pl.BlockSpec((1, CHUNK, kw), lambda b, n, j=j: (b, prev(n), j)),
                         pl.BlockSpec((1, rows, kw), lambda b, n, j=j: (b, n, j)),
                         pl.BlockSpec((1, CHUNK, kw), lambda b, n, j=j: (b, nxt(n), j))]
        args += [z] * 6
        in_specs += [cur, cur, one(prev), one(prev), cur, cur, one(nxt), one(nxt)]
        args += [att_tabs['qa'], att_tabs['qb'], att_tabs['ka'], att_tabs['kb'], att_tabs['ka'], att_tabs['kb'],
                 att_tabs['ka'], att_tabs['kb']]
    else:
        in_specs += [pl.BlockSpec((1, LANES), lambda b, n: (0, 0))]
        args += [att_tabs['gq']]
    in_specs += [pl.BlockSpec((1, L, kw), lambda b, n: (b, 0, kj)),
                 pl.BlockSpec((1, L, kw), lambda b, n: (b, 0, vj)),
                 pl.BlockSpec((1, LANES), lambda b, n: (0, 0)),
                 pl.BlockSpec((LANES, LANES), lambda b, n: (0, 0))]
    args += [zc, zc, att_tabs['gk'], att_tabs['ones']]
    return pl.pallas_call(
        functools.partial(_att_body, local, qb),
        grid=(B, nb // qb),
        in_specs=in_specs,
        out_specs=pl.BlockSpec((1, rows, qw), lambda b, n: (b, n, 0)),
        out_shape=jax.ShapeDtypeStruct((B, T, qw), bf16),
        compiler_params=_params("parallel", "parallel"),
        name="attention" if local else "ctx_attention",
    )(*args)


def _merge_body(n_gate_blocks, ret_ref, sgu_ref, att_ref, *refs):
    g_refs = refs[:N_BRANCHES * n_gate_blocks]
    x_ref, wb_ref, wo_ref, gate_ref, scale_ref, shift_ref, wr_ref, xo_ref, h2_ref, aff_ref = refs[len(g_refs):]
    merged = None
    for i, br_ref in enumerate((ret_ref, sgu_ref, att_ref)):
        p = jnp.dot(br_ref[0], wb_ref[i], preferred_element_type=f32)
        g = jnp.concatenate([r[0] for r in g_refs[n_gate_blocks * i:n_gate_blocks * (i + 1)]], axis=1)
        t = jax.nn.sigmoid(g.astype(f32)) * p
        merged = t if merged is None else merged + t
    o = jnp.dot(merged.astype(bf16), wo_ref[...], preferred_element_type=f32)
    x = x_ref[0] + gate_ref[0] * o
    xo_ref[0] = x
    y = x * lax.rsqrt(jnp.mean(x * x, axis=-1, keepdims=True) + EPS)
    h2 = y * scale_ref[0] + shift_ref[0]
    h2_ref[0, :, 0, :] = h2
    logits = jnp.dot(h2.astype(bf16), wr_ref[...], preferred_element_type=f32)
    logits = jnp.where(_lane(logits.shape) < N_EXPERTS, logits, NEG_INF)
    m = jnp.max(logits, axis=-1, keepdims=True)
    e = jnp.exp(logits - m)
    aff_ref[0] = e / jnp.sum(e, axis=-1, keepdims=True)


def _merge(ret, sgu, att, z, gates_at, x, wb, wo, gate, scale, shift, wr):
    B, T, D = x.shape
    bw = ret.shape[-1]
    tm = _pick_tile(T, 512)
    tok = lambda b, i: (b, i, 0)
    vec = lambda b, i: (b, 0, 0)
    gw = 2 * LANES
    assert gates_at % gw == 0 and D % gw == 0
    n_gate_blocks = D // gw
    gate_specs = [pl.BlockSpec((1, tm, gw), lambda b, i, j=gates_at // gw + k: (b, i, j))
                  for k in range(N_BRANCHES * n_gate_blocks)]
    return pl.pallas_call(
        functools.partial(_merge_body, n_gate_blocks),
        grid=(B, T // tm),
        in_specs=[
            pl.BlockSpec((1, tm, bw), tok), pl.BlockSpec((1, tm, bw), tok), pl.BlockSpec((1, tm, bw), tok),
            *gate_specs,
            pl.BlockSpec((1, tm, D), tok),
            pl.BlockSpec((N_BRANCHES, bw, D), lambda b, i: (0, 0, 0)),
            pl.BlockSpec((D, D), lambda b, i: (0, 0)),
            pl.BlockSpec((1, 1, D), vec), pl.BlockSpec((1, 1, D), vec), pl.BlockSpec((1, 1, D), vec),
            pl.BlockSpec((D, LANES), lambda b, i: (0, 0)),
        ],
        out_specs=[
            pl.BlockSpec((1, tm, D), tok), pl.BlockSpec((1, tm, 1, D), lambda b, i: (b, i, 0, 0)),
            pl.BlockSpec((1, tm, LANES), tok),
        ],
        out_shape=[
            jax.ShapeDtypeStruct((B, T, D), f32),
            jax.ShapeDtypeStruct((B, T, 1, D), f32),
            jax.ShapeDtypeStruct((B, T, LANES), f32),
        ],
        compiler_params=_params("parallel", "parallel"),
        name="merge",
    )(ret, sgu, att, *([z] * len(gate_specs)), x, wb, wo, gate, scale, shift, wr)


ROUTE_TILE = 256
AFF_BITS = 31
MIN_NORMAL_BITS = 0x00800000


def _route_body(cap, aff_ref, tri_ref, pos_ref, post_ref, c0_ref, afft):
    T = aff_ref.shape[1]
    tt = tri_ref.shape[0]
    for j in range(T // tt):
        afft[:, tt * j:tt * (j + 1)] = aff_ref[0, tt * j:tt * (j + 1), :].T[:N_EXPERTS]

    def as_float(bits):
        return pltpu.bitcast(jnp.broadcast_to(bits, (N_EXPERTS, LANES)), f32)[:, :1]

    def count_at_least(v):
        return jnp.sum(jnp.where(afft[...] >= v, 1.0, 0.0), axis=1, keepdims=True)

    def refine(i, thr):
        cand = thr | jnp.left_shift(jnp.int32(1), AFF_BITS - 1 - i)
        return jnp.where(count_at_least(as_float(cand)) >= cap, cand, thr)

    thr = lax.fori_loop(0, AFF_BITS, refine, jnp.zeros((N_EXPERTS, 1), jnp.int32))
    hi_col = as_float(jnp.maximum(thr + 1, MIN_NORMAL_BITS))
    diag = (lax.broadcasted_iota(jnp.int32, (N_EXPERTS, LANES), 0)
            == lax.broadcasted_iota(jnp.int32, (N_EXPERTS, LANES), 1))

    def along_lanes(col):
        return jnp.sum(jnp.where(diag, col, 0.0), axis=0, keepdims=True)

    thr_f = along_lanes(as_float(thr))
    hi_f = along_lanes(hi_col)
    need = along_lanes(cap - count_at_least(hi_col))
    tri = tri_ref[...]
    carry_eq = jnp.zeros((1, LANES), f32)
    carry_sel = jnp.zeros((1, LANES), f32)
    for j in range(T // tt):
        a = aff_ref[0, tt * j:tt * (j + 1), :]
        eq = (a >= thr_f) & (a < hi_f)
        eqf = jnp.where(eq, 1.0, 0.0)
        eq_rank = carry_eq + jnp.dot(tri, eqf.astype(bf16), preferred_element_type=f32) - eqf
        sel = (a >= hi_f) | (eq & (eq_rank < need))
        self_ = jnp.where(sel, 1.0, 0.0)
        pos = carry_sel + jnp.dot(tri, self_.astype(bf16), preferred_element_type=f32) - self_
        pos = jnp.where(sel, pos, -1.0).astype(jnp.int32)
        pos_ref[0, tt * j:tt * (j + 1), :] = pos
        post_ref[0, j] = pos.T[:N_EXPERTS]
        c0_ref[0, j:j + 1, :] = carry_sel.astype(jnp.int32)
        carry_eq = carry_eq + jnp.sum(eqf, axis=0, keepdims=True)
        carry_sel = carry_sel + jnp.sum(self_, axis=0, keepdims=True)


def _route(aff, cap):
    B, T, _ = aff.shape
    tt = min(ROUTE_TILE, T)
    nt = T // tt
    tri = jnp.tril(jnp.ones((tt, tt), bf16))
    return pl.pallas_call(
        functools.partial(_route_body, cap),
        grid=(B,),
        in_specs=[pl.BlockSpec((1, T, LANES), lambda b: (b, 0, 0)), pl.BlockSpec((tt, tt), lambda b: (0, 0))],
        out_specs=[
            pl.BlockSpec((1, T, LANES), lambda b: (b, 0, 0)),
            pl.BlockSpec((1, nt, N_EXPERTS, tt), lambda b: (b, 0, 0, 0)),
            pl.BlockSpec((1, nt, LANES), lambda b: (b, 0, 0)),
        ],
        out_shape=[
            jax.ShapeDtypeStruct((B, T, LANES), jnp.int32),
            jax.ShapeDtypeStruct((B, nt, N_EXPERTS, tt), jnp.int32),
            jax.ShapeDtypeStruct((B, nt, LANES), jnp.int32),
        ],
        scratch_shapes=[pltpu.VMEM((N_EXPERTS, T), f32)],
        compiler_params=_params("parallel"),
        name="route",
    )(aff, tri)


def _slots_body(nt, cap, c0_ref, post_ref, idx_ref):
    b = pl.program_id(0)
    tt = post_ref.shape[-1]
    tok = lax.broadcasted_iota(jnp.int32, (LANES, tt), 1)
    diag = lax.broadcasted_iota(jnp.int32, (LANES, LANES), 0) == lax.broadcasted_iota(jnp.int32, (LANES, LANES), 1)

    def per_expert(e, carry):
        def tiles_starting_at_or_before(slot):
            return sum((c0_ref[(b * nt + j) * N_EXPERTS + e] <= slot).astype(jnp.int32) for j in range(nt))

        for k in range(idx_ref.shape[2]):
            slot = LANES * k + lax.broadcasted_iota(jnp.int32, (LANES, 1), 0)

            def match(j, acc):
                return acc + jnp.where(post_ref[0, j, pl.ds(e, 1), :] == slot, tok + j * tt, 0)

            j_lo = tiles_starting_at_or_before(LANES * k) - 1
            j_hi = tiles_starting_at_or_before(min(LANES * (k + 1), cap) - 1)
            acc = lax.fori_loop(j_lo, j_hi, match, jnp.zeros((LANES, tt), jnp.int32))
            col = jnp.sum(acc.astype(f32), axis=1, keepdims=True)
            row = jnp.sum(jnp.where(diag, col, 0.0), axis=0, keepdims=True)
            idx_ref[0, e, k:k + 1, :] = row.astype(jnp.int32)
        return carry

    lax.fori_loop(0, N_EXPERTS, per_expert, 0)


def _slots(c0s, post, cap):
    B, nt, E, tt = post.shape
    nblk = -(-cap // LANES)
    idx = pl.pallas_call(
        functools.partial(_slots_body, nt, cap),
        grid_spec=pltpu.PrefetchScalarGridSpec(
            num_scalar_prefetch=1,
            grid=(B,),
            in_specs=[pl.BlockSpec((1, nt, E, tt), lambda b, c0: (b, 0, 0, 0))],
            out_specs=pl.BlockSpec((1, E, nblk, LANES), lambda b, c0: (b, 0, 0, 0)),
        ),
        out_shape=jax.ShapeDtypeStruct((B, E, nblk, LANES), jnp.int32),
        compiler_params=_params("arbitrary"),
        name="slots",
    )(c0s, post)
    return idx.reshape(B, E, nblk * LANES)[:, :, :cap]


EXPERT_MIN_ROWS = 256


def _expert_body(cap, spb, idx_ref, h_ref, wg_ref, wu_ref, wd_ref, y_ref, rows, wg_s, wu_s, wd_s, sem):
    n_b = pl.num_programs(1)
    step = pl.program_id(0) * n_b + pl.program_id(1)
    last = pl.num_programs(0) * n_b - 1
    n_rows = spb * cap

    def start_rows(s, buf):
        e, g = s // n_b, s % n_b
        for r in range(n_rows):
            b = g * spb + r // cap
            row = idx_ref[(b * N_EXPERTS + e) * cap + r % cap]
            pltpu.make_async_copy(h_ref.at[row], rows.at[buf, pl.ds(r, 1), :], sem.at[buf]).start(priority=r % 2)

    def wait_rows(buf):
        pltpu.make_async_copy(h_ref.at[pl.ds(0, n_rows), 0, :], rows.at[buf], sem.at[buf]).wait()

    @pl.when(step == 0)
    def _():
        start_rows(step, 0)

    @pl.when(pl.program_id(1) == 0)
    def _():
        wg_s[...] = wg_ref[0, 0].astype(bf16)
        wu_s[...] = wu_ref[0, 0].astype(bf16)
        wd_s[...] = wd_ref[0, 0].astype(bf16)

    for cur in range(2):
        @pl.when(step % 2 == cur)
        def _():
            wait_rows(cur)
            xs = rows[cur].astype(bf16)

            @pl.when(step < last)
            def _():
                start_rows(step + 1, 1 - cur)

            a = jnp.dot(xs, wg_s[...], preferred_element_type=f32)
            u = jnp.dot(xs, wu_s[...], preferred_element_type=f32)
            hmid = (jax.nn.silu(a) * u).astype(bf16)
            y = jnp.dot(hmid, wd_s[...], preferred_element_type=f32).astype(y_ref.dtype)
            for i in range(spb):
                y_ref[i, 0] = y[cap * i:cap * (i + 1)]


def _expert_ffn(idx, h2, layer, wg, wu, wd):
    rows_of = idx + (jnp.arange(idx.shape[0], dtype=jnp.int32) * h2.shape[1])[:, None, None]
    B, T, _, D = h2.shape
    _, E, _, F = wg.shape
    cap = idx.shape[-1]
    spb = _pick_tile(B, max(1, EXPERT_MIN_ROWS // cap))
    n_rows = spb * cap
    wsp = lambda e, g, idx: (layer, e, 0, 0)
    return pl.pallas_call(
        functools.partial(_expert_body, cap, spb),
        grid_spec=pltpu.PrefetchScalarGridSpec(
            num_scalar_prefetch=1,
            grid=(E, B // spb),
            in_specs=[
                pl.BlockSpec(memory_space=pl.ANY),
                pl.BlockSpec((1, 1, D, F), wsp), pl.BlockSpec((1, 1, D, F), wsp), pl.BlockSpec((1, 1, F, D), wsp),
            ],
            out_specs=pl.BlockSpec((spb, 1, cap, D), lambda e, g, idx: (g, e, 0, 0)),
            scratch_shapes=[pltpu.VMEM((2, n_rows, D), f32),
                            pltpu.VMEM((D, F), bf16), pltpu.VMEM((D, F), bf16), pltpu.VMEM((F, D), bf16),
                            pltpu.SemaphoreType.DMA((2,))],
        ),
        out_shape=jax.ShapeDtypeStruct((B, E, cap, D), bf16),
        compiler_params=pltpu.CompilerParams(dimension_semantics=("arbitrary", "arbitrary"),
                                             vmem_limit_bytes=EXPERT_VMEM_LIMIT_BYTES),
        name="expert_ffn",
    )(rows_of.reshape(-1), h2.reshape(B * T, 1, D), wg, wu, wd)


Y_ROW_ALIGN = 16


def _combine_body(nt, cap, win, c0_ref, pos_ref, aff_ref, x_ref, gate_ref, spread_ref, y_ref, o_ref, buf, extra,
                  sem, sem_x):
    b = pl.program_id(0)
    j = pl.program_id(1)
    step = b * nt + j
    tt = pos_ref.shape[1]
    stride = win if win == cap else win - Y_ROW_ALIGN

    def c0(bb, jj, e):
        return c0_ref[(bb * nt + jj) * N_EXPERTS + e]

    def window_start(lo):
        return pl.multiple_of(jnp.minimum(lo & (-Y_ROW_ALIGN), cap - win), Y_ROW_ALIGN)

    def copies(bb, jj, p, dst, s):
        return [pltpu.make_async_copy(
            y_ref.at[bb, e, pl.ds(window_start(c0(bb, jj, e) + p * stride), win), :],
            dst.at[pl.ds(e * win, win), :], s) for e in range(N_EXPERTS)]

    @pl.when(step == 0)
    def _():
        for cp in copies(b, j, 0, buf.at[0], sem.at[0]):
            cp.start()

    nxt = step + 1

    @pl.when(nxt < pl.num_programs(0) * nt)
    def _():
        for cp in copies(nxt // nt, nxt % nt, 0, buf.at[nxt % 2], sem.at[nxt % 2]):
            cp.start()

    lane = _lane((1, LANES))
    pos = pos_ref[0]
    aff = aff_ref[0]

    def lane_vector(fn):
        v = jnp.zeros((1, LANES), jnp.int32)
        for e in range(N_EXPERTS):
            v = jnp.where(lane == e, fn(e), v)
        return v

    spread = spread_ref[...]
    col_slot = _lane((1, N_EXPERTS * win)) & (win - 1)
    aff_cols = jnp.dot(aff.astype(bf16), spread, preferred_element_type=f32)

    def weights(p):
        lo = lane_vector(lambda e: c0(b, j, e) + p * stride)
        start = lane_vector(lambda e: window_start(c0(b, j, e) + p * stride))
        rel = jnp.where((pos >= lo) & (pos < lo + stride), pos - start, -1).astype(f32)
        rel_cols = jnp.dot(rel.astype(bf16), spread, preferred_element_type=f32)
        return jnp.where(rel_cols == col_slot.astype(f32), aff_cols, 0.0).astype(bf16)

    for cp in copies(b, j, 0, buf.at[step % 2], sem.at[step % 2]):
        cp.wait()
    moe = jnp.dot(weights(0), buf[step % 2], preferred_element_type=f32)

    def count(e):
        end = jnp.where(j + 1 < nt, c0(b, jnp.minimum(j + 1, nt - 1), e), cap)
        return end - c0(b, j, e)

    max_count = functools.reduce(jnp.maximum, [count(e) for e in range(N_EXPERTS)])

    def more(p, acc):
        cps = copies(b, j, p, extra, sem_x.at[0])
        for cp in cps:
            cp.start()
        for cp in cps:
            cp.wait()
        return acc + jnp.dot(weights(p), extra[...], preferred_element_type=f32)

    moe = lax.fori_loop(1, (max_count + stride - 1) // stride, more, moe)
    o_ref[0] = x_ref[0] + gate_ref[0] * moe


def _combine(c0s, pos, aff, x, gate, y):
    B, T, D = x.shape
    E, cap = y.shape[1], y.shape[2]
    tt = min(ROUTE_TILE, T)
    nt = T // tt
    win = min(cap, LANES // 2)
    assert (cap - win) % Y_ROW_ALIGN == 0 and win & (win - 1) == 0
    spread = (jnp.arange(LANES)[:, None] == jnp.arange(E * win)[None, :] // win).astype(bf16)
    tok = lambda b, j, c0: (b, j, 0)
    return pl.pallas_call(
        functools.partial(_combine_body, nt, cap, win),
        grid_spec=pltpu.PrefetchScalarGridSpec(
            num_scalar_prefetch=1,
            grid=(B, nt),
            in_specs=[
                pl.BlockSpec((1, tt, LANES), tok), pl.BlockSpec((1, tt, LANES), tok), pl.BlockSpec((1, tt, D), tok),
                pl.BlockSpec((1, 1, D), lambda b, j, c0: (b, 0, 0)),
                pl.BlockSpec((LANES, E * win), lambda b, j, c0: (0, 0)),
                pl.BlockSpec(memory_space=pl.ANY),
            ],
            out_specs=pl.BlockSpec((1, tt, D), tok),
            scratch_shapes=[pltpu.VMEM((2, E * win, D), bf16), pltpu.VMEM((E * win, D), bf16),
                            pltpu.SemaphoreType.DMA((2,)), pltpu.SemaphoreType.DMA((1,))],
        ),
        out_shape=jax.ShapeDtypeStruct((B, T, D), f32),
        compiler_params=_params("arbitrary", "arbitrary"),
        name="combine",
    )(c0s, pos, aff, x, gate, spread, y)


def _expert_choice(h2, aff, x, gate2, p):
    T = x.shape[1]
    cap = CAPACITY_FACTOR * T // N_EXPERTS
    pos, post, c0 = _route(aff, cap)
    c0s = c0[:, :, :N_EXPERTS].reshape(-1)
    y = _expert_ffn(_slots(c0s, post, cap), h2, p['layer'], p['w_exp_gate'], p['w_exp_up'], p['w_exp_down'])
    return _combine(c0s, pos, aff, x, gate2, y)


def _rope_tables(T):
    rows = T // GRID_W
    row = jnp.repeat(jnp.arange(rows), GRID_W).astype(f32)
    col = jnp.tile(jnp.arange(GRID_W), rows).astype(f32)
    n_freq = HEAD_DIM // 4
    inv = jnp.power(ROPE_BASE, -jnp.arange(n_freq, dtype=f32) / n_freq)
    ang = jnp.concatenate([row[:, None] * inv, col[:, None] * inv], axis=-1)
    cos, sin = jnp.cos(ang), jnp.sin(ang)
    reps = LANES // HEAD_DIM
    return jnp.tile(jnp.concatenate([cos, cos], axis=-1), (1, reps)), jnp.tile(
        jnp.concatenate([-sin, sin], axis=-1), (1, reps))


def _identity_rope(T):
    return jnp.ones((T, LANES), f32), jnp.zeros((T, LANES), f32)


def _retention_tables(decay_fwd, decay_bwd):
    lgf = jax.nn.log_sigmoid(decay_fwd.astype(f32))
    lgb = jax.nn.log_sigmoid(decay_bwd.astype(f32))
    C = CHUNK
    k_scale = HEAD_DIM ** -0.5
    idx = jnp.arange(C, dtype=f32)
    npair = RET_HEADS // 2

    def lanes(t):
        return jnp.repeat(t.reshape(npair, 2, C).transpose(0, 2, 1), HEAD_DIM, axis=-1)

    diff = idx[:, None] - idx[None, :]
    dm = jnp.where(diff >= 0, jnp.exp(jnp.maximum(diff, 0.0)[None] * lgf[:, None, None]),
                   jnp.exp(jnp.maximum(-diff, 0.0)[None] * lgb[:, None, None])) * k_scale

    def chunk_decay(lg):
        rows = jnp.repeat(jnp.exp(C * lg).reshape(npair, 2), HEAD_DIM, axis=-1)
        return jnp.broadcast_to(rows[:, :, None], (npair, LANES, 2 * LANES))

    r = jnp.arange(LANES)[:, None] // HEAD_DIM
    c = jnp.arange(2 * LANES)[None, :] // LANES
    return {
        'wkf': lanes(jnp.exp((C - 1 - idx)[None, :] * lgf[:, None])) * k_scale,
        'wkb': lanes(jnp.exp(idx[None, :] * lgb[:, None])) * k_scale,
        'qdf': lanes(jnp.exp((idx + 1.0)[None, :] * lgf[:, None])),
        'qdb': lanes(jnp.exp((C - idx)[None, :] * lgb[:, None])),
        'dm': dm, 'cdf': chunk_decay(lgf), 'cdb': chunk_decay(lgb), 'bm': (r == c).astype(f32),
    }


def _attention_tables(cos, sin, q_gain, k_gain):
    reps = LANES // HEAD_DIM
    half = HEAD_DIM // 2
    swap = lambda g: jnp.concatenate([g[half:], g[:half]])
    gq, gk = jnp.tile(q_gain, reps)[None], jnp.tile(k_gain, reps)[None]
    gqs, gks = jnp.tile(swap(q_gain), reps)[None], jnp.tile(swap(k_gain), reps)[None]
    r = jnp.arange(LANES) // HEAD_DIM
    return {'qa': cos * gq, 'qb': sin * gqs, 'ka': cos * gk, 'kb': sin * gks, 'gq': gq, 'gk': gk,
            'ones': (r[:, None] == r[None, :]).astype(bf16)}


def _layer(x, xc, mod_lat, mod_ctx, rope, p, last):
    B, S, D = x.shape
    L = xc.shape[1]
    lay = _layout(D)
    sh1, sc1, g1, sh2, sc2, g2 = jnp.split(mod_lat, 6, axis=-1)
    csh1, csc1, cg1, csh2, csc2, cg2 = jnp.split(mod_ctx, 6, axis=-1)
    n1 = p['norm1_g'][None, None, :]
    n2 = p['norm2_g'][None, None, :]

    def bcast(t):
        return jnp.broadcast_to(t, (B, 1, D))

    z = _proj_in(x, n1 * (1 + sc1), sh1, p['w_in'])
    zc = _proj_in(xc, bcast(n1 * (1 + csc1)), bcast(csh1), p['w_in'])

    cos, sin = rope
    cos_c, sin_c = _identity_rope(L)
    rtabs = _retention_tables(p['ret_decay_fwd'], p['ret_decay_bwd'])
    atabs = _attention_tables(cos, sin, p['q_norm_g'], p['k_norm_g'])
    sgu_w = p['sgu_w'].astype(bf16)
    gw = D // 2 // SGU_GROUPS
    sgu_bias = jnp.repeat(p['sgu_b'].T, gw, axis=-1)

    zero = jnp.zeros((B, RET_HEADS // 2, LANES, 2 * LANES), f32)
    sfc, sbc, s_f, s_b = _ret_states(zc, lay, cos_c, sin_c, rtabs, zero, zero)
    sf, sb, _, _ = _ret_states(z, lay, cos, sin, rtabs, s_f, s_b)
    ret, sgu = _ret_sgu(z, lay, cos, sin, sf, sb, rtabs, sgu_w, sgu_bias)
    att = _attention(z, zc, lay, atabs, p['attn_sink'], True)

    x, h2, aff = _merge(ret, sgu, att, z, lay['gates'][0], x, p['w_branch'], p['w_out'], g1, n2 * (1 + sc2), sh2,
                        p['w_router'])
    x = _expert_choice(h2, aff, x, g2, p)
    if last:
        return x, None

    ret_c, sgu_c = _ret_sgu(zc, lay, cos_c, sin_c, sfc, sbc, rtabs, sgu_w, sgu_bias)
    att_c = _attention(zc, zc, lay, atabs, p['attn_sink'], False)
    xc, h2c, affc = _merge(ret_c, sgu_c, att_c, zc, lay['gates'][0], xc, p['w_branch'], p['w_out'], bcast(cg1),
                           bcast(n2 * (1 + csc2)), bcast(csh2), p['w_router'])
    xc = _expert_choice(h2c, affc, xc, bcast(cg2), p)
    return x, xc


def kernel(x, c, ctx, c_ctx, w_mod, b_mod, norm1_g, norm2_g, w_in, ret_decay_fwd, ret_decay_bwd, sgu_w, sgu_b,
           q_norm_g, k_norm_g, attn_sink, w_branch, w_out, w_router, w_exp_gate, w_exp_up, w_exp_down):
    B, S, D = x.shape
    depth = w_in.shape[0]
    rope = _rope_tables(S)
    c_act = jax.nn.silu(c)
    c_ctx_act = jax.nn.silu(c_ctx)
    xc = ctx
    hp = lax.Precision.HIGHEST
    for l in range(depth):
        wr = jnp.zeros((D, LANES), f32).at[:, :N_EXPERTS].set(w_router[l]).astype(bf16)
        p = {
            'norm1_g': norm1_g[l], 'norm2_g': norm2_g[l], 'w_in': w_in[l].astype(bf16),
            'ret_decay_fwd': ret_decay_fwd[l], 'ret_decay_bwd': ret_decay_bwd[l],
            'sgu_w': sgu_w[l], 'sgu_b': sgu_b[l],
            'q_norm_g': q_norm_g[l], 'k_norm_g': k_norm_g[l], 'attn_sink': attn_sink[l],
            'w_branch': w_branch[l].astype(bf16), 'w_out': w_out[l].astype(bf16), 'w_router': wr,
            'layer': l, 'w_exp_gate': w_exp_gate, 'w_exp_up': w_exp_up, 'w_exp_down': w_exp_down,
        }
        mod = jnp.dot(jnp.concatenate([c_act, c_ctx_act[None]], axis=0), w_mod[l], precision=hp) + b_mod[l]
        mod_lat, mod_ctx = mod[:B, None, :], mod[B:, None, :]
        x, xc = _layer(x, xc, mod_lat, mod_ctx, rope, p, l == depth - 1)
    return x
```

```python
import functools

import jax
import jax.numpy as jnp
from jax import lax
from jax.experimental import pallas as pl
from jax.experimental.pallas import tpu as pltpu

GRID_W = 64
N_BRANCHES = 3
RET_HEADS = 4
SGU_GROUPS = 4
CHUNK = 128
HEAD_DIM = 64
ATT_KV_HEADS = 2
ROPE_BASE = 10000.0
N_EXPERTS = 16
CAPACITY_FACTOR = 2
EPS = 1e-6
NEG_INF = -1e30

LANES = 128
VMEM_LIMIT_BYTES = 48 * 1024 * 1024
EXPERT_VMEM_LIMIT_BYTES = 56 * 1024 * 1024

f32 = jnp.float32
bf16 = jnp.bfloat16


def _layout(d_model):
    bw = d_model // 2
    qk = RET_HEADS * HEAD_DIM
    kv = ATT_KV_HEADS * HEAD_DIM
    widths = (('ret_q', qk), ('ret_k', qk), ('ret_v', bw), ('ret_g', bw), ('sgu_u', bw), ('sgu_v', bw),
              ('att_q', bw), ('att_k', kv), ('att_v', kv), ('gates', N_BRANCHES * d_model))
    out, start = {}, 0
    for name, w in widths:
        out[name] = (start, w)
        start += w
    return out


def _chunks_per_step(T):
    return next(c for c in (4, 2, 1) if (T // CHUNK) % c == 0)


def _pick_tile(n, pref):
    t = min(n, pref)
    while n % t:
        t //= 2
    return t


def _params(*sem):
    return pltpu.CompilerParams(dimension_semantics=sem, vmem_limit_bytes=VMEM_LIMIT_BYTES)


PROJ_NORM_ROWS = 256


def _proj_in_body(x_ref, scale_ref, shift_ref, w_ref, o_ref, h_ref):
    @pl.when(pl.program_id(2) == 0)
    def _():
        step = min(PROJ_NORM_ROWS, h_ref.shape[0])
        for r in range(0, h_ref.shape[0], step):
            x = x_ref[0, r:r + step, :]
            y = x * lax.rsqrt(jnp.mean(x * x, axis=-1, keepdims=True) + EPS)
            h = (y * scale_ref[0] + shift_ref[0]).astype(bf16)
            h_ref[r:r + step, :] = h
            o_ref[0, r:r + step, :] = jnp.dot(h, w_ref[...], preferred_element_type=f32).astype(o_ref.dtype)

    @pl.when(pl.program_id(2) != 0)
    def _():
        o_ref[0] = jnp.dot(h_ref[...], w_ref[...], preferred_element_type=f32).astype(o_ref.dtype)


def _proj_in(x, scale, shift, w):
    B, T, D = x.shape
    N = w.shape[1]
    tm = _pick_tile(T, 2048)
    tn = 1280 if N % 1280 == 0 else _pick_tile(N, 1024)
    return pl.pallas_call(
        _proj_in_body,
        grid=(B, T // tm, N // tn),
        in_specs=[
            pl.BlockSpec((1, tm, D), lambda b, i, j: (b, i, 0)),
            pl.BlockSpec((1, 1, D), lambda b, i, j: (b, 0, 0)),
            pl.BlockSpec((1, 1, D), lambda b, i, j: (b, 0, 0)),
            pl.BlockSpec((D, tn), lambda b, i, j: (0, j)),
        ],
        out_specs=pl.BlockSpec((1, tm, tn), lambda b, i, j: (b, i, j)),
        out_shape=jax.ShapeDtypeStruct((B, T, N), bf16),
        scratch_shapes=[pltpu.VMEM((tm, D), bf16)],
        compiler_params=_params("parallel", "parallel", "arbitrary"),
        name="proj_in",
    )(x, scale, shift, w)


def _lane(shape):
    return lax.broadcasted_iota(jnp.int32, shape, 1)


def _swap_halves(x):
    return jnp.where((_lane(x.shape) & (HEAD_DIM // 2)) == 0,
                     pltpu.roll(x, LANES - HEAD_DIM // 2, 1), pltpu.roll(x, HEAD_DIM // 2, 1))


def _rope(x, cos, sin_signed):
    return x * cos + _swap_halves(x) * sin_signed


def _head_sumsq(x, ones_blockdiag):
    x2 = x * x
    hi = x2.astype(bf16)
    lo = (x2 - hi.astype(f32)).astype(bf16)
    return (jnp.dot(hi, ones_blockdiag, preferred_element_type=f32)
            + jnp.dot(lo, ones_blockdiag, preferred_element_type=f32))


def _dot_nt(a, b):
    return lax.dot_general(a, b, (((1,), (1,)), ((), ())), preferred_element_type=f32)


def _dot_tn(a, b):
    return lax.dot_general(a, b, (((0,), (0,)), ((), ())), preferred_element_type=f32)


def _ret_state_body(cpb, kf_ref, vf_ref, kb_ref, vb_ref, cosf_ref, sinf_ref, cosb_ref, sinb_ref,
                    wkf_ref, wkb_ref, cdf_ref, cdb_ref, bm_ref, s0f_ref, s0b_ref,
                    sf_ref, sb_ref, ff_ref, fb_ref, accf, accb):
    n = pl.program_id(1)

    @pl.when(n == 0)
    def _():
        accf[...] = s0f_ref[0]
        accb[...] = s0b_ref[0]

    def one_chunk(c, k_ref, v_ref, cos_ref, sin_ref, wk_ref, cd_ref, out_ref, acc):
        rows = slice(CHUNK * c, CHUNK * (c + 1))
        k = k_ref[0, rows, :].astype(f32)
        v = v_ref[0, rows, :]
        for p in range(RET_HEADS // 2):
            kp = _rope(k[:, LANES * p:LANES * (p + 1)], cos_ref[rows, :], sin_ref[rows, :])
            kw = (kp * wk_ref[p]).astype(bf16)
            upd = _dot_tn(kw, v[:, 2 * LANES * p:2 * LANES * (p + 1)])
            s = acc[p]
            out_ref[0, c, p] = s.astype(out_ref.dtype)
            acc[p] = cd_ref[p] * s + bm_ref[...] * upd

    for c in range(cpb):
        one_chunk(c, kf_ref, vf_ref, cosf_ref, sinf_ref, wkf_ref, cdf_ref, sf_ref, accf)
        one_chunk(cpb - 1 - c, kb_ref, vb_ref, cosb_ref, sinb_ref, wkb_ref, cdb_ref, sb_ref, accb)

    @pl.when(n == pl.num_programs(1) - 1)
    def _():
        ff_ref[0] = accf[...]
        fb_ref[0] = accb[...]


def _ret_states(z, lay, cos, sin, tabs, s0f, s0b):
    B, T, _ = z.shape
    cpb = _chunks_per_step(T)
    rows = cpb * CHUNK
    nc = T // rows
    kq = lay['ret_k'][0] // (RET_HEADS * HEAD_DIM)
    vq = lay['ret_v'][0] // lay['ret_v'][1]
    vw = lay['ret_v'][1]
    kw = RET_HEADS * HEAD_DIM
    npair = RET_HEADS // 2
    st_shape = (npair, LANES, 2 * LANES)
    const3 = lambda b, n: (0, 0, 0)
    return pl.pallas_call(
        functools.partial(_ret_state_body, cpb),
        grid=(B, nc),
        in_specs=[
            pl.BlockSpec((1, rows, kw), lambda b, n: (b, n, kq)),
            pl.BlockSpec((1, rows, vw), lambda b, n: (b, n, vq)),
            pl.BlockSpec((1, rows, kw), lambda b, n: (b, nc - 1 - n, kq)),
            pl.BlockSpec((1, rows, vw), lambda b, n: (b, nc - 1 - n, vq)),
            pl.BlockSpec((rows, LANES), lambda b, n: (n, 0)),
            pl.BlockSpec((rows, LANES), lambda b, n: (n, 0)),
            pl.BlockSpec((rows, LANES), lambda b, n: (nc - 1 - n, 0)),
            pl.BlockSpec((rows, LANES), lambda b, n: (nc - 1 - n, 0)),
            pl.BlockSpec((npair, CHUNK, LANES), const3),
            pl.BlockSpec((npair, CHUNK, LANES), const3),
            pl.BlockSpec(st_shape, const3),
            pl.BlockSpec(st_shape, const3),
            pl.BlockSpec((LANES, 2 * LANES), lambda b, n: (0, 0)),
            pl.BlockSpec((1,) + st_shape, lambda b, n: (b, 0, 0, 0)),
            pl.BlockSpec((1,) + st_shape, lambda b, n: (b, 0, 0, 0)),
        ],
        out_specs=[
            pl.BlockSpec((1, cpb) + st_shape, lambda b, n: (b, n, 0, 0, 0)),
            pl.BlockSpec((1, cpb) + st_shape, lambda b, n: (b, nc - 1 - n, 0, 0, 0)),
            pl.BlockSpec((1,) + st_shape, lambda b, n: (b, 0, 0, 0)),
            pl.BlockSpec((1,) + st_shape, lambda b, n: (b, 0, 0, 0)),
        ],
        out_shape=[
            jax.ShapeDtypeStruct((B, nc * cpb) + st_shape, bf16),
            jax.ShapeDtypeStruct((B, nc * cpb) + st_shape, bf16),
            jax.ShapeDtypeStruct((B,) + st_shape, f32),
            jax.ShapeDtypeStruct((B,) + st_shape, f32),
        ],
        scratch_shapes=[pltpu.VMEM(st_shape, f32), pltpu.VMEM(st_shape, f32)],
        compiler_params=_params("parallel", "arbitrary"),
        name="ret_states",
    )(z, z, z, z, cos, sin, cos, sin, tabs['wkf'], tabs['wkb'], tabs['cdf'], tabs['cdb'], tabs['bm'], s0f, s0b)


def _ret_sgu_body(cpb, *refs):
    for c in range(cpb):
        _ret_sgu_chunk(c, *refs)


def _ret_sgu_chunk(c, q_ref, k_ref, v_ref, g_ref, u_ref, w_ref, cos_ref, sin_ref, sf_ref, sb_ref,
                   dm_ref, qdf_ref, qdb_ref, sw_ref, sbias_ref, ret_ref, sgu_ref):
    rows = slice(CHUNK * c, CHUNK * (c + 1))
    q = q_ref[0, rows, :].astype(f32)
    k = k_ref[0, rows, :].astype(f32)
    v = v_ref[0, rows, :]
    g = g_ref[0, rows, :].astype(f32)
    cos = cos_ref[rows, :]
    sin = sin_ref[rows, :]
    first_half = _lane((CHUNK, LANES)) < HEAD_DIM
    for p in range(RET_HEADS // 2):
        qp = _rope(q[:, LANES * p:LANES * (p + 1)], cos, sin)
        kp = _rope(k[:, LANES * p:LANES * (p + 1)], cos, sin).astype(bf16)
        cross = (jnp.dot((qp * qdf_ref[p]).astype(bf16), sf_ref[0, c, p], preferred_element_type=f32)
                 + jnp.dot((qp * qdb_ref[p]).astype(bf16), sb_ref[0, c, p], preferred_element_type=f32))
        for a in range(2):
            h = 2 * p + a
            qm = jnp.where(first_half if a == 0 else ~first_half, qp, 0.0).astype(bf16)
            sc = (_dot_nt(qm, kp) * dm_ref[h]).astype(bf16)
            o = jnp.dot(sc, v[:, LANES * h:LANES * (h + 1)], preferred_element_type=f32)
            o = o + cross[:, LANES * a:LANES * (a + 1)]
            oc = o - jnp.mean(o, axis=-1, keepdims=True)
            y = oc * lax.rsqrt(jnp.mean(oc * oc, axis=-1, keepdims=True) + EPS)
            ret_ref[0, rows, LANES * h:LANES * (h + 1)] = (
                y * jax.nn.silu(g[:, LANES * h:LANES * (h + 1)])).astype(ret_ref.dtype)

    u = jax.nn.gelu(u_ref[0, rows, :].astype(f32))
    w = jax.nn.gelu(w_ref[0, rows, :].astype(f32))
    wc = w - jnp.mean(w, axis=-1, keepdims=True)
    wn = (wc * lax.rsqrt(jnp.mean(wc * wc, axis=-1, keepdims=True) + EPS)).astype(bf16)
    gw = wn.shape[-1] // SGU_GROUPS
    for i in range(SGU_GROUPS):
        sl = slice(gw * i, gw * (i + 1))
        mixed = jnp.dot(sw_ref[i], wn[:, sl], preferred_element_type=f32) + sbias_ref[:, sl]
        sgu_ref[0, rows, sl] = (u[:, sl] * mixed).astype(sgu_ref.dtype)


def _ret_sgu(z, lay, cos, sin, sf, sb, tabs, sgu_w, sgu_bias):
    B, T, _ = z.shape
    cpb = _chunks_per_step(T)
    rows = cpb * CHUNK
    nc = T // rows
    bw = lay['ret_v'][1]

    def zs(name):
        s, w = lay[name]
        return pl.BlockSpec((1, rows, w), lambda b, n, j=s // w: (b, n, j))

    st_shape = (RET_HEADS // 2, LANES, 2 * LANES)
    const3 = lambda b, n: (0, 0, 0)
    out_spec = pl.BlockSpec((1, rows, bw), lambda b, n: (b, n, 0))
    return pl.pallas_call(
        functools.partial(_ret_sgu_body, cpb),
        grid=(B, nc),
        in_specs=[
            zs('ret_q'), zs('ret_k'), zs('ret_v'), zs('ret_g'), zs('sgu_u'), zs('sgu_v'),
            pl.BlockSpec((rows, LANES), lambda b, n: (n, 0)),
            pl.BlockSpec((rows, LANES), lambda b, n: (n, 0)),
            pl.BlockSpec((1, cpb) + st_shape, lambda b, n: (b, n, 0, 0, 0)),
            pl.BlockSpec((1, cpb) + st_shape, lambda b, n: (b, n, 0, 0, 0)),
            pl.BlockSpec((RET_HEADS, CHUNK, CHUNK), const3),
            pl.BlockSpec((RET_HEADS // 2, CHUNK, LANES), const3),
            pl.BlockSpec((RET_HEADS // 2, CHUNK, LANES), const3),
            pl.BlockSpec((SGU_GROUPS, CHUNK, CHUNK), const3),
            pl.BlockSpec((CHUNK, bw), lambda b, n: (0, 0)),
        ],
        out_specs=[out_spec, out_spec],
        out_shape=[jax.ShapeDtypeStruct((B, T, bw), bf16), jax.ShapeDtypeStruct((B, T, bw), bf16)],
        compiler_params=_params("parallel", "parallel"),
        name="ret_sgu",
    )(z, z, z, z, z, z, cos, sin, sf, sb, tabs['dm'], tabs['qdf'], tabs['qdb'], sgu_w, sgu_bias)


def _att_body(local, qb, sink_ref, q_ref, *refs):
    if local:
        (kp_ref, kc_ref, kn_ref, vp_ref, vc_ref, vn_ref, qa_ref, qb_ref,
         kap_ref, kbp_ref, kac_ref, kbc_ref, kan_ref, kbn_ref, ck_ref, cv_ref, gk_ref, ones_ref, o_ref) = refs
    else:
        qa_ref, ck_ref, cv_ref, gk_ref, ones_ref, o_ref = refs
    n = pl.program_id(1)
    n_blocks = pl.num_programs(1) * qb
    ones = ones_ref[...]
    inv_hd = 1.0 / HEAD_DIM

    def norm_rope(x_ref_val, ta, tb):
        x = x_ref_val.astype(f32)
        r = lax.rsqrt(_head_sumsq(x, ones) * inv_hd + EPS)
        y = x * ta if tb is None else x * ta + _swap_halves(x) * tb
        return y * r

    def dup(x):
        lo = _lane(x.shape) < HEAD_DIM
        xr = pltpu.roll(x, HEAD_DIM, 1)
        return jnp.where(lo, x, xr).astype(bf16), jnp.where(lo, xr, x).astype(bf16)

    def blk(i):
        return slice(CHUNK * i, CHUNK * (i + 1))

    k_ctx = dup(norm_rope(ck_ref[0], gk_ref[...], None))
    v_ctx = dup(cv_ref[0].astype(f32))
    if local:
        k_loc = ([dup(norm_rope(kp_ref[0], kap_ref[...], kbp_ref[...]))]
                 + [dup(norm_rope(kc_ref[0, blk(i), :], kac_ref[blk(i), :], kbc_ref[blk(i), :])) for i in range(qb)]
                 + [dup(norm_rope(kn_ref[0], kan_ref[...], kbn_ref[...]))])
        v_loc = ([dup(vp_ref[0].astype(f32))] + [dup(vc_ref[0, blk(i), :].astype(f32)) for i in range(qb)]
                 + [dup(vn_ref[0].astype(f32))])
    n_loc = 3 * CHUNK if local else 0

    n_groups = q_ref.shape[-1] // LANES
    groups_per_kv = n_groups // ATT_KV_HEADS
    rows = 2 * groups_per_kv * CHUNK
    log2e = 1.4426950408889634
    first_half = _lane((CHUNK, LANES)) < HEAD_DIM
    for qi in range(qb):
        if local:
            i = lax.broadcasted_iota(jnp.int32, (CHUNK, n_loc), 0)
            j = lax.broadcasted_iota(jnp.int32, (CHUNK, n_loc), 1)
            has_prev = (n * qb + qi > 0).astype(jnp.int32)
            has_next = (n * qb + qi < n_blocks - 1).astype(jnp.int32)
            valid = (j >= i * has_prev + CHUNK * (1 - has_prev)) & (j <= 2 * CHUNK - 1 + has_next * (i + 1))
            bias = jnp.where(valid, 0.0, NEG_INF)
            bias = jnp.concatenate([bias] * (rows // CHUNK), axis=0)
        for h in range(ATT_KV_HEADS):
            keys = ([k_loc[qi + d][h] for d in range(3)] if local else []) + [k_ctx[h]]
            vals = ([v_loc[qi + d][h] for d in range(3)] if local else []) + [v_ctx[h]]
            keys = jnp.concatenate(keys, axis=0) if local else keys[0]
            vals = jnp.concatenate(vals, axis=0) if local else vals[0]
            q_rows, sink_rows = [], []
            for c in range(groups_per_kv * h, groups_per_kv * (h + 1)):
                xq = q_ref[0, blk(qi), LANES * c:LANES * (c + 1)]
                qn = norm_rope(xq, qa_ref[blk(qi), :] if local else qa_ref[...],
                               qb_ref[blk(qi), :] if local else None) * (HEAD_DIM ** -0.5 * log2e)
                for a in range(2):
                    q_rows.append(jnp.where(first_half if a == 0 else ~first_half, qn, 0.0).astype(bf16))
                    sink_rows.append(jnp.full((CHUNK, 1), sink_ref[2 * c + a] * log2e, f32))
            s = _dot_nt(jnp.concatenate(q_rows, axis=0), keys)
            sk = jnp.concatenate(sink_rows, axis=0)
            s_ctx = s[:, n_loc:]
            m = jnp.maximum(jnp.max(s_ctx, axis=-1, keepdims=True), sk)
            if local:
                s_loc = s[:, :n_loc] + bias
                m = jnp.maximum(m, jnp.max(s_loc, axis=-1, keepdims=True))
            p_ctx = jnp.exp2(s_ctx - m)
            d = jnp.sum(p_ctx, axis=-1, keepdims=True) + jnp.exp2(sk - m)
            o = jnp.dot(p_ctx.astype(bf16), vals[n_loc:], preferred_element_type=f32)
            if local:
                p_loc = jnp.exp2(s_loc - m)
                d = d + jnp.sum(p_loc, axis=-1, keepdims=True)
                o = o + jnp.dot(p_loc.astype(bf16), vals[:n_loc], preferred_element_type=f32)
            o = o * (1.0 / d)
            for ci in range(groups_per_kv):
                c = groups_per_kv * h + ci
                o0 = o[2 * CHUNK * ci:2 * CHUNK * ci + CHUNK]
                o1 = o[2 * CHUNK * ci + CHUNK:2 * CHUNK * (ci + 1)]
                o_ref[0, blk(qi), LANES * c:LANES * (c + 1)] = jnp.where(first_half, o0, o1).astype(o_ref.dtype)


def _attention(z, zc, lay, att_tabs, sink, local):
    B, T, _ = z.shape
    L = zc.shape[1]
    nb = T // CHUNK
    qb = _chunks_per_step(T)
    rows = qb * CHUNK
    qs, qw = lay['att_q']
    ks, kw = lay['att_k']
    vs, _ = lay['att_v']
    kj, vj = ks // kw, vs // kw
    prev = lambda n: jnp.maximum(n * qb - 1, 0)
    nxt = lambda n: jnp.minimum(n * qb + qb, nb - 1)
    one = lambda f: pl.BlockSpec((CHUNK, LANES), lambda b, n: (f(n), 0))
    cur = pl.BlockSpec((rows, LANES), lambda b, n: (n, 0))
    in_specs = [pl.BlockSpec(memory_space=pltpu.SMEM),
                pl.BlockSpec((1, rows, qw), lambda b, n: (b, n, qs // qw))]
    args = [sink, z]
    if local:
        for j in (kj, vj):
            in_specs += [pl.BlockSpec((1, CHUNK, kw), lambda b, n, j=j: (b, prev(n), j)),
                         pl.BlockSpec((1, rows, kw), lambda b, n, j=j: (b, n, j)),
                         pl.BlockSpec((1, CHUNK, kw), lambda b, n, j=j: (b, nxt(n), j))]
        args += [z] * 6
        in_specs += [cur, cur, one(prev), one(prev), cur, cur, one(nxt), one(nxt)]
        args += [att_tabs['qa'], att_tabs['qb'], att_tabs['ka'], att_tabs['kb'], att_tabs['ka'], att_tabs['kb'],
                 att_tabs['ka'], att_tabs['kb']]
    else:
        in_specs += [pl.BlockSpec((1, LANES), lambda b, n: (0, 0))]
        args += [att_tabs['gq']]
    in_specs += [pl.BlockSpec((1, L, kw), lambda b, n: (b, 0, kj)),
                 pl.BlockSpec((1, L, kw), lambda b, n: (b, 0, vj)),
                 pl.BlockSpec((1, LANES), lambda b, n: (0, 0)),
                 pl.BlockSpec((LANES, LANES), lambda b, n: (0, 0))]
    args += [zc, zc, att_tabs['gk'], att_tabs['ones']]
    return pl.pallas_call(
        functools.partial(_att_body, local, qb),
        grid=(B, nb // qb),
        in_specs=in_specs,
        out_specs=pl.BlockSpec((1, rows, qw), lambda b, n: (b, n, 0)),
        out_shape=jax.ShapeDtypeStruct((B, T, qw), bf16),
        compiler_params=_params("parallel", "parallel"),
        name="attention" if local else "ctx_attention",
    )(*args)


def _merge_body(n_gate_blocks, ret_ref, sgu_ref, att_ref, *refs):
    g_refs = refs[:N_BRANCHES * n_gate_blocks]
    x_ref, wb_ref, wo_ref, gate_ref, scale_ref, shift_ref, wr_ref, xo_ref, h2_ref, aff_ref = refs[len(g_refs):]
    merged = None
    for i, br_ref in enumerate((ret_ref, sgu_ref, att_ref)):
        p = jnp.dot(br_ref[0], wb_ref[i], preferred_element_type=f32)
        g = jnp.concatenate([r[0] for r in g_refs[n_gate_blocks * i:n_gate_blocks * (i + 1)]], axis=1)
        t = jax.nn.sigmoid(g.astype(f32)) * p
        merged = t if merged is None else merged + t
    o = jnp.dot(merged.astype(bf16), wo_ref[...], preferred_element_type=f32)
    x = x_ref[0] + gate_ref[0] * o
    xo_ref[0] = x
    y = x * lax.rsqrt(jnp.mean(x * x, axis=-1, keepdims=True) + EPS)
    h2 = y * scale_ref[0] + shift_ref[0]
    h2_ref[0, :, 0, :] = h2
    logits = jnp.dot(h2.astype(bf16), wr_ref[...], preferred_element_type=f32)
    logits = jnp.where(_lane(logits.shape) < N_EXPERTS, logits, NEG_INF)
    m = jnp.max(logits, axis=-1, keepdims=True)
    e = jnp.exp(logits - m)
    aff_ref[0] = e / jnp.sum(e, axis=-1, keepdims=True)


def _merge(ret, sgu, att, z, gates_at, x, wb, wo, gate, scale, shift, wr):
    B, T, D = x.shape
    bw = ret.shape[-1]
    tm = _pick_tile(T, 512)
    tok = lambda b, i: (b, i, 0)
    vec = lambda b, i: (b, 0, 0)
    gw = 2 * LANES
    assert gates_at % gw == 0 and D % gw == 0
    n_gate_blocks = D // gw
    gate_specs = [pl.BlockSpec((1, tm, gw), lambda b, i, j=gates_at // gw + k: (b, i, j))
                  for k in range(N_BRANCHES * n_gate_blocks)]
    return pl.pallas_call(
        functools.partial(_merge_body, n_gate_blocks),
        grid=(B, T // tm),
        in_specs=[
            pl.BlockSpec((1, tm, bw), tok), pl.BlockSpec((1, tm, bw), tok), pl.BlockSpec((1, tm, bw), tok),
            *gate_specs,
            pl.BlockSpec((1, tm, D), tok),
            pl.BlockSpec((N_BRANCHES, bw, D), lambda b, i: (0, 0, 0)),
            pl.BlockSpec((D, D), lambda b, i: (0, 0)),
            pl.BlockSpec((1, 1, D), vec), pl.BlockSpec((1, 1, D), vec), pl.BlockSpec((1, 1, D), vec),
            pl.BlockSpec((D, LANES), lambda b, i: (0, 0)),
        ],
        out_specs=[
            pl.BlockSpec((1, tm, D), tok), pl.BlockSpec((1, tm, 1, D), lambda b, i: (b, i, 0, 0)),
            pl.BlockSpec((1, tm, LANES), tok),
        ],
        out_shape=[
            jax.ShapeDtypeStruct((B, T, D), f32),
            jax.ShapeDtypeStruct((B, T, 1, D), f32),
            jax.ShapeDtypeStruct((B, T, LANES), f32),
        ],
        compiler_params=_params("parallel", "parallel"),
        name="merge",
    )(ret, sgu, att, *([z] * len(gate_specs)), x, wb, wo, gate, scale, shift, wr)


ROUTE_TILE = 256
AFF_BITS = 31
MIN_NORMAL_BITS = 0x00800000


def _route_body(cap, aff_ref, tri_ref, pos_ref, post_ref, c0_ref, afft):
    T = aff_ref.shape[1]
    tt = tri_ref.shape[0]
    for j in range(T // tt):
        afft[:, tt * j:tt * (j + 1)] = aff_ref[0, tt * j:tt * (j + 1), :].T[:N_EXPERTS]

    def as_float(bits):
        return pltpu.bitcast(jnp.broadcast_to(bits, (N_EXPERTS, LANES)), f32)[:, :1]

    def count_at_least(v):
        return jnp.sum(jnp.where(afft[...] >= v, 1.0, 0.0), axis=1, keepdims=True)

    def refine(i, thr):
        cand = thr | jnp.left_shift(jnp.int32(1), AFF_BITS - 1 - i)
        return jnp.where(count_at_least(as_float(cand)) >= cap, cand, thr)

    thr = lax.fori_loop(0, AFF_BITS, refine, jnp.zeros((N_EXPERTS, 1), jnp.int32))
    hi_col = as_float(jnp.maximum(thr + 1, MIN_NORMAL_BITS))
    diag = (lax.broadcasted_iota(jnp.int32, (N_EXPERTS, LANES), 0)
            == lax.broadcasted_iota(jnp.int32, (N_EXPERTS, LANES), 1))

    def along_lanes(col):
        return jnp.sum(jnp.where(diag, col, 0.0), axis=0, keepdims=True)

    thr_f = along_lanes(as_float(thr))
    hi_f = along_lanes(hi_col)
    need = along_lanes(cap - count_at_least(hi_col))
    tri = tri_ref[...]
    carry_eq = jnp.zeros((1, LANES), f32)
    carry_sel = jnp.zeros((1, LANES), f32)
    for j in range(T // tt):
        a = aff_ref[0, tt * j:tt * (j + 1), :]
        eq = (a >= thr_f) & (a < hi_f)
        eqf = jnp.where(eq, 1.0, 0.0)
        eq_rank = carry_eq + jnp.dot(tri, eqf.astype(bf16), preferred_element_type=f32) - eqf
        sel = (a >= hi_f) | (eq & (eq_rank < need))
        self_ = jnp.where(sel, 1.0, 0.0)
        pos = carry_sel + jnp.dot(tri, self_.astype(bf16), preferred_element_type=f32) - self_
        pos = jnp.where(sel, pos, -1.0).astype(jnp.int32)
        pos_ref[0, tt * j:tt * (j + 1), :] = pos
        post_ref[0, j] = pos.T[:N_EXPERTS]
        c0_ref[0, j:j + 1, :] = carry_sel.astype(jnp.int32)
        carry_eq = carry_eq + jnp.sum(eqf, axis=0, keepdims=True)
        carry_sel = carry_sel + jnp.sum(self_, axis=0, keepdims=True)


def _route(aff, cap):
    B, T, _ = aff.shape
    tt = min(ROUTE_TILE, T)
    nt = T // tt
    tri = jnp.tril(jnp.ones((tt, tt), bf16))
    return pl.pallas_call(
        functools.partial(_route_body, cap),
        grid=(B,),
        in_specs=[pl.BlockSpec((1, T, LANES), lambda b: (b, 0, 0)), pl.BlockSpec((tt, tt), lambda b: (0, 0))],
        out_specs=[
            pl.BlockSpec((1, T, LANES), lambda b: (b, 0, 0)),
            pl.BlockSpec((1, nt, N_EXPERTS, tt), lambda b: (b, 0, 0, 0)),
            pl.BlockSpec((1, nt, LANES), lambda b: (b, 0, 0)),
        ],
        out_shape=[
            jax.ShapeDtypeStruct((B, T, LANES), jnp.int32),
            jax.ShapeDtypeStruct((B, nt, N_EXPERTS, tt), jnp.int32),
            jax.ShapeDtypeStruct((B, nt, LANES), jnp.int32),
        ],
        scratch_shapes=[pltpu.VMEM((N_EXPERTS, T), f32)],
        compiler_params=_params("parallel"),
        name="route",
    )(aff, tri)


def _slots_body(nt, cap, c0_ref, post_ref, idx_ref):
    b = pl.program_id(0)
    tt = post_ref.shape[-1]
    tok = lax.broadcasted_iota(jnp.int32, (LANES, tt), 1)
    diag = lax.broadcasted_iota(jnp.int32, (LANES, LANES), 0) == lax.broadcasted_iota(jnp.int32, (LANES, LANES), 1)

    def per_expert(e, carry):
        def tiles_starting_at_or_before(slot):
            return sum((c0_ref[(b * nt + j) * N_EXPERTS + e] <= slot).astype(jnp.int32) for j in range(nt))

        for k in range(idx_ref.shape[2]):
            slot = LANES * k + lax.broadcasted_iota(jnp.int32, (LANES, 1), 0)

            def match(j, acc):
                return acc + jnp.where(post_ref[0, j, pl.ds(e, 1), :] == slot, tok + j * tt, 0)

            j_lo = tiles_starting_at_or_before(LANES * k) - 1
            j_hi = tiles_starting_at_or_before(min(LANES * (k + 1), cap) - 1)
            acc = lax.fori_loop(j_lo, j_hi, match, jnp.zeros((LANES, tt), jnp.int32))
            col = jnp.sum(acc.astype(f32), axis=1, keepdims=True)
            row = jnp.sum(jnp.where(diag, col, 0.0), axis=0, keepdims=True)
            idx_ref[0, e, k:k + 1, :] = row.astype(jnp.int32)
        return carry

    lax.fori_loop(0, N_EXPERTS, per_expert, 0)


def _slots(c0s, post, cap):
    B, nt, E, tt = post.shape
    nblk = -(-cap // LANES)
    idx = pl.pallas_call(
        functools.partial(_slots_body, nt, cap),
        grid_spec=pltpu.PrefetchScalarGridSpec(
            num_scalar_prefetch=1,
            grid=(B,),
            in_specs=[pl.BlockSpec((1, nt, E, tt), lambda b, c0: (b, 0, 0, 0))],
            out_specs=pl.BlockSpec((1, E, nblk, LANES), lambda b, c0: (b, 0, 0, 0)),
        ),
        out_shape=jax.ShapeDtypeStruct((B, E, nblk, LANES), jnp.int32),
        compiler_params=_params("arbitrary"),
        name="slots",
    )(c0s, post)
    return idx.reshape(B, E, nblk * LANES)[:, :, :cap]


EXPERT_MIN_ROWS = 256
EXPERT_ROW_SEMS = 4


def _expert_body(cap, spb, idx_ref, h_ref, wg_ref, wu_ref, wd_ref, y_ref, rows, wg_s, wu_s, wd_s, sem):
    n_b = pl.num_programs(1)
    step = pl.program_id(0) * n_b + pl.program_id(1)
    last = pl.num_programs(0) * n_b - 1
    n_rows = spb * cap

    def start_rows(s, buf):
        e, g = s // n_b, s % n_b
        for r in range(n_rows):
            b = g * spb + r // cap
            row = idx_ref[(b * N_EXPERTS + e) * cap + r % cap]
            pltpu.make_async_copy(h_ref.at[row], rows.at[buf, pl.ds(r, 1), :],
                                  sem.at[buf, r % EXPERT_ROW_SEMS]).start()

    def wait_rows(buf):
        part = n_rows // EXPERT_ROW_SEMS
        for q in range(EXPERT_ROW_SEMS):
            pltpu.make_async_copy(h_ref.at[pl.ds(0, part), 0, :], rows.at[buf, pl.ds(0, part), :],
                                  sem.at[buf, q]).wait()

    @pl.when(step == 0)
    def _():
        start_rows(step, 0)

    @pl.when(pl.program_id(1) == 0)
    def _():
        wg_s[...] = wg_ref[0, 0].astype(bf16)
        wu_s[...] = wu_ref[0, 0].astype(bf16)
        wd_s[...] = wd_ref[0, 0].astype(bf16)

    for cur in range(2):
        @pl.when(step % 2 == cur)
        def _():
            wait_rows(cur)
            xs = rows[cur].astype(bf16)

            @pl.when(step < last)
            def _():
                start_rows(step + 1, 1 - cur)

            a = jnp.dot(xs, wg_s[...], preferred_element_type=f32)
            u = jnp.dot(xs, wu_s[...], preferred_element_type=f32)
            hmid = (jax.nn.silu(a) * u).astype(bf16)
            y = jnp.dot(hmid, wd_s[...], preferred_element_type=f32).astype(y_ref.dtype)
            for i in range(spb):
                y_ref[i, 0] = y[cap * i:cap * (i + 1)]


def _expert_ffn(idx, h2, layer, wg, wu, wd):
    rows_of = idx + (jnp.arange(idx.shape[0], dtype=jnp.int32) * h2.shape[1])[:, None, None]
    B, T, _, D = h2.shape
    _, E, _, F = wg.shape
    cap = idx.shape[-1]
    spb = _pick_tile(B, max(1, EXPERT_MIN_ROWS // cap))
    n_rows = spb * cap
    wsp = lambda e, g, idx: (layer, e, 0, 0)
    return pl.pallas_call(
        functools.partial(_expert_body, cap, spb),
        grid_spec=pltpu.PrefetchScalarGridSpec(
            num_scalar_prefetch=1,
            grid=(E, B // spb),
            in_specs=[
                pl.BlockSpec(memory_space=pl.ANY),
                pl.BlockSpec((1, 1, D, F), wsp), pl.BlockSpec((1, 1, D, F), wsp), pl.BlockSpec((1, 1, F, D), wsp),
            ],
            out_specs=pl.BlockSpec((spb, 1, cap, D), lambda e, g, idx: (g, e, 0, 0)),
            scratch_shapes=[pltpu.VMEM((2, n_rows, D), f32),
                            pltpu.VMEM((D, F), bf16), pltpu.VMEM((D, F), bf16), pltpu.VMEM((F, D), bf16),
                            pltpu.SemaphoreType.DMA((2, EXPERT_ROW_SEMS))],
        ),
        out_shape=jax.ShapeDtypeStruct((B, E, cap, D), bf16),
        compiler_params=pltpu.CompilerParams(dimension_semantics=("arbitrary", "arbitrary"),
                                             vmem_limit_bytes=EXPERT_VMEM_LIMIT_BYTES),
        name="expert_ffn",
    )(rows_of.reshape(-1), h2.reshape(B * T, 1, D), wg, wu, wd)


Y_ROW_ALIGN = 16


def _combine_body(nt, cap, win, c0_ref, pos_ref, aff_ref, x_ref, gate_ref, spread_ref, y_ref, o_ref, buf, extra,
                  sem, sem_x):
    b = pl.program_id(0)
    j = pl.program_id(1)
    step = b * nt + j
    tt = pos_ref.shape[1]
    stride = win if win == cap else win - Y_ROW_ALIGN

    def c0(bb, jj, e):
        return c0_ref[(bb * nt + jj) * N_EXPERTS + e]

    def window_start(lo):
        return pl.multiple_of(jnp.minimum(lo & (-Y_ROW_ALIGN), cap - win), Y_ROW_ALIGN)

    def copies(bb, jj, p, dst, s):
        return [pltpu.make_async_copy(
            y_ref.at[bb, e, pl.ds(window_start(c0(bb, jj, e) + p * stride), win), :],
            dst.at[pl.ds(e * win, win), :], s) for e in range(N_EXPERTS)]

    @pl.when(step == 0)
    def _():
        for cp in copies(b, j, 0, buf.at[0], sem.at[0]):
            cp.start()

    nxt = step + 1

    @pl.when(nxt < pl.num_programs(0) * nt)
    def _():
        for cp in copies(nxt // nt, nxt % nt, 0, buf.at[nxt % 2], sem.at[nxt % 2]):
            cp.start()

    lane = _lane((1, LANES))
    pos = pos_ref[0]
    aff = aff_ref[0]

    def lane_vector(fn):
        v = jnp.zeros((1, LANES), jnp.int32)
        for e in range(N_EXPERTS):
            v = jnp.where(lane == e, fn(e), v)
        return v

    spread = spread_ref[...]
    col_slot = _lane((1, N_EXPERTS * win)) & (win - 1)
    aff_cols = jnp.dot(aff.astype(bf16), spread, preferred_element_type=f32)

    def weights(p):
        lo = lane_vector(lambda e: c0(b, j, e) + p * stride)
        start = lane_vector(lambda e: window_start(c0(b, j, e) + p * stride))
        rel = jnp.where((pos >= lo) & (pos < lo + stride), pos - start, -1).astype(f32)
        rel_cols = jnp.dot(rel.astype(bf16), spread, preferred_element_type=f32)
        return jnp.where(rel_cols == col_slot.astype(f32), aff_cols, 0.0).astype(bf16)

    for cp in copies(b, j, 0, buf.at[step % 2], sem.at[step % 2]):
        cp.wait()
    moe = jnp.dot(weights(0), buf[step % 2], preferred_element_type=f32)

    def count(e):
        end = jnp.where(j + 1 < nt, c0(b, jnp.minimum(j + 1, nt - 1), e), cap)
        return end - c0(b, j, e)

    max_count = functools.reduce(jnp.maximum, [count(e) for e in range(N_EXPERTS)])

    def more(p, acc):
        cps = copies(b, j, p, extra, sem_x.at[0])
        for cp in cps:
            cp.start()
        for cp in cps:
            cp.wait()
        return acc + jnp.dot(weights(p), extra[...], preferred_element_type=f32)

    moe = lax.fori_loop(1, (max_count + stride - 1) // stride, more, moe)
    o_ref[0] = x_ref[0] + gate_ref[0] * moe


def _combine(c0s, pos, aff, x, gate, y):
    B, T, D = x.shape
    E, cap = y.shape[1], y.shape[2]
    tt = min(ROUTE_TILE, T)
    nt = T // tt
    win = min(cap, LANES // 2)
    assert (cap - win) % Y_ROW_ALIGN == 0 and win & (win - 1) == 0
    spread = (jnp.arange(LANES)[:, None] == jnp.arange(E * win)[None, :] // win).astype(bf16)
    tok = lambda b, j, c0: (b, j, 0)
    return pl.pallas_call(
        functools.partial(_combine_body, nt, cap, win),
        grid_spec=pltpu.PrefetchScalarGridSpec(
            num_scalar_prefetch=1,
            grid=(B, nt),
            in_specs=[
                pl.BlockSpec((1, tt, LANES), tok), pl.BlockSpec((1, tt, LANES), tok), pl.BlockSpec((1, tt, D), tok),
                pl.BlockSpec((1, 1, D), lambda b, j, c0: (b, 0, 0)),
                pl.BlockSpec((LANES, E * win), lambda b, j, c0: (0, 0)),
                pl.BlockSpec(memory_space=pl.ANY),
            ],
            out_specs=pl.BlockSpec((1, tt, D), tok),
            scratch_shapes=[pltpu.VMEM((2, E * win, D), bf16), pltpu.VMEM((E * win, D), bf16),
                            pltpu.SemaphoreType.DMA((2,)), pltpu.SemaphoreType.DMA((1,))],
        ),
        out_shape=jax.ShapeDtypeStruct((B, T, D), f32),
        compiler_params=_params("arbitrary", "arbitrary"),
        name="combine",
    )(c0s, pos, aff, x, gate, spread, y)


def _expert_choice(h2, aff, x, gate2, p):
    T = x.shape[1]
    cap = CAPACITY_FACTOR * T // N_EXPERTS
    pos, post, c0 = _route(aff, cap)
    c0s = c0[:, :, :N_EXPERTS].reshape(-1)
    y = _expert_ffn(_slots(c0s, post, cap), h2, p['layer'], p['w_exp_gate'], p['w_exp_up'], p['w_exp_down'])
    return _combine(c0s, pos, aff, x, gate2, y)


def _rope_tables(T):
    rows = T // GRID_W
    row = jnp.repeat(jnp.arange(rows), GRID_W).astype(f32)
    col = jnp.tile(jnp.arange(GRID_W), rows).astype(f32)
    n_freq = HEAD_DIM // 4
    inv = jnp.power(ROPE_BASE, -jnp.arange(n_freq, dtype=f32) / n_freq)
    ang = jnp.concatenate([row[:, None] * inv, col[:, None] * inv], axis=-1)
    cos, sin = jnp.cos(ang), jnp.sin(ang)
    reps = LANES // HEAD_DIM
    return jnp.tile(jnp.concatenate([cos, cos], axis=-1), (1, reps)), jnp.tile(
        jnp.concatenate([-sin, sin], axis=-1), (1, reps))


def _identity_rope(T):
    return jnp.ones((T, LANES), f32), jnp.zeros((T, LANES), f32)


def _retention_tables(decay_fwd, decay_bwd):
    lgf = jax.nn.log_sigmoid(decay_fwd.astype(f32))
    lgb = jax.nn.log_sigmoid(decay_bwd.astype(f32))
    C = CHUNK
    k_scale = HEAD_DIM ** -0.5
    idx = jnp.arange(C, dtype=f32)
    npair = RET_HEADS // 2

    def lanes(t):
        return jnp.repeat(t.reshape(npair, 2, C).transpose(0, 2, 1), HEAD_DIM, axis=-1)

    diff = idx[:, None] - idx[None, :]
    dm = jnp.where(diff >= 0, jnp.exp(jnp.maximum(diff, 0.0)[None] * lgf[:, None, None]),
                   jnp.exp(jnp.maximum(-diff, 0.0)[None] * lgb[:, None, None])) * k_scale

    def chunk_decay(lg):
        rows = jnp.repeat(jnp.exp(C * lg).reshape(npair, 2), HEAD_DIM, axis=-1)
        return jnp.broadcast_to(rows[:, :, None], (npair, LANES, 2 * LANES))

    r = jnp.arange(LANES)[:, None] // HEAD_DIM
    c = jnp.arange(2 * LANES)[None, :] // LANES
    return {
        'wkf': lanes(jnp.exp((C - 1 - idx)[None, :] * lgf[:, None])) * k_scale,
        'wkb': lanes(jnp.exp(idx[None, :] * lgb[:, None])) * k_scale,
        'qdf': lanes(jnp.exp((idx + 1.0)[None, :] * lgf[:, None])),
        'qdb': lanes(jnp.exp((C - idx)[None, :] * lgb[:, None])),
        'dm': dm, 'cdf': chunk_decay(lgf), 'cdb': chunk_decay(lgb), 'bm': (r == c).astype(f32),
    }


def _attention_tables(cos, sin, q_gain, k_gain):
    reps = LANES // HEAD_DIM
    half = HEAD_DIM // 2
    swap = lambda g: jnp.concatenate([g[half:], g[:half]])
    gq, gk = jnp.tile(q_gain, reps)[None], jnp.tile(k_gain, reps)[None]
    gqs, gks = jnp.tile(swap(q_gain), reps)[None], jnp.tile(swap(k_gain), reps)[None]
    r = jnp.arange(LANES) // HEAD_DIM
    return {'qa': cos * gq, 'qb': sin * gqs, 'ka': cos * gk, 'kb': sin * gks, 'gq': gq, 'gk': gk,
            'ones': (r[:, None] == r[None, :]).astype(bf16)}


def _layer(x, xc, mod_lat, mod_ctx, rope, p, last):
    B, S, D = x.shape
    L = xc.shape[1]
    lay = _layout(D)
    sh1, sc1, g1, sh2, sc2, g2 = jnp.split(mod_lat, 6, axis=-1)
    csh1, csc1, cg1, csh2, csc2, cg2 = jnp.split(mod_ctx, 6, axis=-1)
    n1 = p['norm1_g'][None, None, :]
    n2 = p['norm2_g'][None, None, :]

    def bcast(t):
        return jnp.broadcast_to(t, (B, 1, D))

    z = _proj_in(x, n1 * (1 + sc1), sh1, p['w_in'])
    zc = _proj_in(xc, bcast(n1 * (1 + csc1)), bcast(csh1), p['w_in'])

    cos, sin = rope
    cos_c, sin_c = _identity_rope(L)
    rtabs = _retention_tables(p['ret_decay_fwd'], p['ret_decay_bwd'])
    atabs = _attention_tables(cos, sin, p['q_norm_g'], p['k_norm_g'])
    sgu_w = p['sgu_w'].astype(bf16)
    gw = D // 2 // SGU_GROUPS
    sgu_bias = jnp.repeat(p['sgu_b'].T, gw, axis=-1)

    zero = jnp.zeros((B, RET_HEADS // 2, LANES, 2 * LANES), f32)
    sfc, sbc, s_f, s_b = _ret_states(zc, lay, cos_c, sin_c, rtabs, zero, zero)
    sf, sb, _, _ = _ret_states(z, lay, cos, sin, rtabs, s_f, s_b)
    ret, sgu = _ret_sgu(z, lay, cos, sin, sf, sb, rtabs, sgu_w, sgu_bias)
    att = _attention(z, zc, lay, atabs, p['attn_sink'], True)

    x, h2, aff = _merge(ret, sgu, att, z, lay['gates'][0], x, p['w_branch'], p['w_out'], g1, n2 * (1 + sc2), sh2,
                        p['w_router'])
    x = _expert_choice(h2, aff, x, g2, p)
    if last:
        return x, None

    ret_c, sgu_c = _ret_sgu(zc, lay, cos_c, sin_c, sfc, sbc, rtabs, sgu_w, sgu_bias)
    att_c = _attention(zc, zc, lay, atabs, p['attn_sink'], False)
    xc, h2c, affc = _merge(ret_c, sgu_c, att_c, zc, lay['gates'][0], xc, p['w_branch'], p['w_out'], bcast(cg1),
                           bcast(n2 * (1 + csc2)), bcast(csh2), p['w_router'])
    xc = _expert_choice(h2c, affc, xc, bcast(cg2), p)
    return x, xc


def kernel(x, c, ctx, c_ctx, w_mod, b_mod, norm1_g, norm2_g, w_in, ret_decay_fwd, ret_decay_bwd, sgu_w, sgu_b,
           q_norm_g, k_norm_g, attn_sink, w_branch, w_out, w_router, w_exp_gate, w_exp_up, w_exp_down):
    B, S, D = x.shape
    depth = w_in.shape[0]
    rope = _rope_tables(S)
    c_act = jax.nn.silu(c)
    c_ctx_act = jax.nn.silu(c_ctx)
    xc = ctx
    hp = lax.Precision.HIGHEST
    for l in range(depth):
        wr = jnp.zeros((D, LANES), f32).at[:, :N_EXPERTS].set(w_router[l]).astype(bf16)
        p = {
            'norm1_g': norm1_g[l], 'norm2_g': norm2_g[l], 'w_in': w_in[l].astype(bf16),
            'ret_decay_fwd': ret_decay_fwd[l], 'ret_decay_bwd': ret_decay_bwd[l],
            'sgu_w': sgu_w[l], 'sgu_b': sgu_b[l],
            'q_norm_g': q_norm_g[l], 'k_norm_g': k_norm_g[l], 'attn_sink': attn_sink[l],
            'w_branch': w_branch[l].astype(bf16), 'w_out': w_out[l].astype(bf16), 'w_router': wr,
            'layer': l, 'w_exp_gate': w_exp_gate, 'w_exp_up': w_exp_up, 'w_exp_down': w_exp_down,
        }
        mod = jnp.dot(jnp.concatenate([c_act, c_ctx_act[None]], axis=0), w_mod[l], precision=hp) + b_mod[l]
        mod_lat, mod_ctx = mod[:B, None, :], mod[B:, None, :]
        x, xc = _layer(x, xc, mod_lat, mod_ctx, rope, p, l == depth - 1)
    return x
```

```python
import functools

import jax
import jax.numpy as jnp
from jax import lax
from jax.experimental import pallas as pl
from jax.experimental.pallas import tpu as pltpu

GRID_W = 64
N_BRANCHES = 3
RET_HEADS = 4
SGU_GROUPS = 4
CHUNK = 128
HEAD_DIM = 64
ATT_KV_HEADS = 2
ROPE_BASE = 10000.0
N_EXPERTS = 16
CAPACITY_FACTOR = 2
EPS = 1e-6
NEG_INF = -1e30

LANES = 128
VMEM_LIMIT_BYTES = 48 * 1024 * 1024
EXPERT_VMEM_LIMIT_BYTES = 56 * 1024 * 1024

f32 = jnp.float32
bf16 = jnp.bfloat16


def _layout(d_model):
    bw = d_model // 2
    qk = RET_HEADS * HEAD_DIM
    kv = ATT_KV_HEADS * HEAD_DIM
    widths = (('ret_q', qk), ('ret_k', qk), ('ret_v', bw), ('ret_g', bw), ('sgu_u', bw), ('sgu_v', bw),
              ('att_q', bw), ('att_k', kv), ('att_v', kv), ('gates', N_BRANCHES * d_model))
    out, start = {}, 0
    for name, w in widths:
        out[name] = (start, w)
        start += w
    return out


def _chunks_per_step(T):
    return next(c for c in (8, 4, 2, 1) if (T // CHUNK) % c == 0)


def _pick_tile(n, pref):
    t = min(n, pref)
    while n % t:
        t //= 2
    return t


def _params(*sem):
    return pltpu.CompilerParams(dimension_semantics=sem, vmem_limit_bytes=VMEM_LIMIT_BYTES)


PROJ_NORM_ROWS = 256


def _proj_in_body(x_ref, scale_ref, shift_ref, w_ref, o_ref, h_ref):
    @pl.when(pl.program_id(2) == 0)
    def _():
        step = min(PROJ_NORM_ROWS, h_ref.shape[0])
        for r in range(0, h_ref.shape[0], step):
            x = x_ref[0, r:r + step, :]
            y = x * lax.rsqrt(jnp.mean(x * x, axis=-1, keepdims=True) + EPS)
            h = (y * scale_ref[0] + shift_ref[0]).astype(bf16)
            h_ref[r:r + step, :] = h
            o_ref[0, r:r + step, :] = jnp.dot(h, w_ref[...], preferred_element_type=f32).astype(o_ref.dtype)

    @pl.when(pl.program_id(2) != 0)
    def _():
        o_ref[0] = jnp.dot(h_ref[...], w_ref[...], preferred_element_type=f32).astype(o_ref.dtype)


def _proj_in(x, scale, shift, w):
    B, T, D = x.shape
    N = w.shape[1]
    tm = _pick_tile(T, 2048)
    tn = 1280 if N % 1280 == 0 else _pick_tile(N, 1024)
    return pl.pallas_call(
        _proj_in_body,
        grid=(B, T // tm, N // tn),
        in_specs=[
            pl.BlockSpec((1, tm, D), lambda b, i, j: (b, i, 0)),
            pl.BlockSpec((1, 1, D), lambda b, i, j: (b, 0, 0)),
            pl.BlockSpec((1, 1, D), lambda b, i, j: (b, 0, 0)),
            pl.BlockSpec((D, tn), lambda b, i, j: (0, j)),
        ],
        out_specs=pl.BlockSpec((1, tm, tn), lambda b, i, j: (b, i, j)),
        out_shape=jax.ShapeDtypeStruct((B, T, N), bf16),
        scratch_shapes=[pltpu.VMEM((tm, D), bf16)],
        compiler_params=_params("parallel", "parallel", "arbitrary"),
        name="proj_in",
    )(x, scale, shift, w)


def _lane(shape):
    return lax.broadcasted_iota(jnp.int32, shape, 1)


def _swap_halves(x):
    return jnp.where((_lane(x.shape) & (HEAD_DIM // 2)) == 0,
                     pltpu.roll(x, LANES - HEAD_DIM // 2, 1), pltpu.roll(x, HEAD_DIM // 2, 1))


def _rope(x, cos, sin_signed):
    return x * cos + _swap_halves(x) * sin_signed


def _head_sumsq(x, ones_blockdiag):
    x2 = x * x
    hi = x2.astype(bf16)
    lo = (x2 - hi.astype(f32)).astype(bf16)
    return (jnp.dot(hi, ones_blockdiag, preferred_element_type=f32)
            + jnp.dot(lo, ones_blockdiag, preferred_element_type=f32))


def _dot_nt(a, b):
    return lax.dot_general(a, b, (((1,), (1,)), ((), ())), preferred_element_type=f32)


def _dot_tn(a, b):
    return lax.dot_general(a, b, (((0,), (0,)), ((), ())), preferred_element_type=f32)


def _ret_state_body(cpb, kf_ref, vf_ref, kb_ref, vb_ref, cosf_ref, sinf_ref, cosb_ref, sinb_ref,
                    wkf_ref, wkb_ref, cdf_ref, cdb_ref, bm_ref, s0f_ref, s0b_ref,
                    sf_ref, sb_ref, ff_ref, fb_ref, accf, accb):
    n = pl.program_id(1)

    @pl.when(n == 0)
    def _():
        accf[...] = s0f_ref[0]
        accb[...] = s0b_ref[0]

    def one_chunk(c, k_ref, v_ref, cos_ref, sin_ref, wk_ref, cd_ref, out_ref, acc):
        rows = slice(CHUNK * c, CHUNK * (c + 1))
        k = k_ref[0, rows, :].astype(f32)
        v = v_ref[0, rows, :]
        for p in range(RET_HEADS // 2):
            kp = _rope(k[:, LANES * p:LANES * (p + 1)], cos_ref[rows, :], sin_ref[rows, :])
            kw = (kp * wk_ref[p]).astype(bf16)
            upd = _dot_tn(kw, v[:, 2 * LANES * p:2 * LANES * (p + 1)])
            s = acc[p]
            out_ref[0, c, p] = s.astype(out_ref.dtype)
            acc[p] = cd_ref[p] * s + bm_ref[...] * upd

    for c in range(cpb):
        one_chunk(c, kf_ref, vf_ref, cosf_ref, sinf_ref, wkf_ref, cdf_ref, sf_ref, accf)
        one_chunk(cpb - 1 - c, kb_ref, vb_ref, cosb_ref, sinb_ref, wkb_ref, cdb_ref, sb_ref, accb)

    @pl.when(n == pl.num_programs(1) - 1)
    def _():
        ff_ref[0] = accf[...]
        fb_ref[0] = accb[...]


def _ret_states(z, lay, cos, sin, tabs, s0f, s0b):
    B, T, _ = z.shape
    cpb = _chunks_per_step(T)
    rows = cpb * CHUNK
    nc = T // rows
    kq = lay['ret_k'][0] // (RET_HEADS * HEAD_DIM)
    vq = lay['ret_v'][0] // lay['ret_v'][1]
    vw = lay['ret_v'][1]
    kw = RET_HEADS * HEAD_DIM
    npair = RET_HEADS // 2
    st_shape = (npair, LANES, 2 * LANES)
    const3 = lambda b, n: (0, 0, 0)
    return pl.pallas_call(
        functools.partial(_ret_state_body, cpb),
        grid=(B, nc),
        in_specs=[
            pl.BlockSpec((1, rows, kw), lambda b, n: (b, n, kq)),
            pl.BlockSpec((1, rows, vw), lambda b, n: (b, n, vq)),
            pl.BlockSpec((1, rows, kw), lambda b, n: (b, nc - 1 - n, kq)),
            pl.BlockSpec((1, rows, vw), lambda b, n: (b, nc - 1 - n, vq)),
            pl.BlockSpec((rows, LANES), lambda b, n: (n, 0)),
            pl.BlockSpec((rows, LANES), lambda b, n: (n, 0)),
            pl.BlockSpec((rows, LANES), lambda b, n: (nc - 1 - n, 0)),
            pl.BlockSpec((rows, LANES), lambda b, n: (nc - 1 - n, 0)),
            pl.BlockSpec((npair, CHUNK, LANES), const3),
            pl.BlockSpec((npair, CHUNK, LANES), const3),
            pl.BlockSpec(st_shape, const3),
            pl.BlockSpec(st_shape, const3),
            pl.BlockSpec((LANES, 2 * LANES), lambda b, n: (0, 0)),
            pl.BlockSpec((1,) + st_shape, lambda b, n: (b, 0, 0, 0)),
            pl.BlockSpec((1,) + st_shape, lambda b, n: (b, 0, 0, 0)),
        ],
        out_specs=[
            pl.BlockSpec((1, cpb) + st_shape, lambda b, n: (b, n, 0, 0, 0)),
            pl.BlockSpec((1, cpb) + st_shape, lambda b, n: (b, nc - 1 - n, 0, 0, 0)),
            pl.BlockSpec((1,) + st_shape, lambda b, n: (b, 0, 0, 0)),
            pl.BlockSpec((1,) + st_shape, lambda b, n: (b, 0, 0, 0)),
        ],
        out_shape=[
            jax.ShapeDtypeStruct((B, nc * cpb) + st_shape, bf16),
            jax.ShapeDtypeStruct((B, nc * cpb) + st_shape, bf16),
            jax.ShapeDtypeStruct((B,) + st_shape, f32),
            jax.ShapeDtypeStruct((B,) + st_shape, f32),
        ],
        scratch_shapes=[pltpu.VMEM(st_shape, f32), pltpu.VMEM(st_shape, f32)],
        compiler_params=_params("parallel", "arbitrary"),
        name="ret_states",
    )(z, z, z, z, cos, sin, cos, sin, tabs['wkf'], tabs['wkb'], tabs['cdf'], tabs['cdb'], tabs['bm'], s0f, s0b)


def _ret_sgu_body(cpb, *refs):
    for c in range(cpb):
        _ret_sgu_chunk(c, *refs)


def _ret_sgu_chunk(c, q_ref, k_ref, v_ref, g_ref, u_ref, w_ref, cos_ref, sin_ref, sf_ref, sb_ref,
                   dm_ref, qdf_ref, qdb_ref, sw_ref, sbias_ref, ret_ref, sgu_ref):
    rows = slice(CHUNK * c, CHUNK * (c + 1))
    q = q_ref[0, rows, :].astype(f32)
    k = k_ref[0, rows, :].astype(f32)
    v = v_ref[0, rows, :]
    g = g_ref[0, rows, :].astype(f32)
    cos = cos_ref[rows, :]
    sin = sin_ref[rows, :]
    first_half = _lane((CHUNK, LANES)) < HEAD_DIM
    for p in range(RET_HEADS // 2):
        qp = _rope(q[:, LANES * p:LANES * (p + 1)], cos, sin)
        kp = _rope(k[:, LANES * p:LANES * (p + 1)], cos, sin).astype(bf16)
        cross = (jnp.dot((qp * qdf_ref[p]).astype(bf16), sf_ref[0, c, p], preferred_element_type=f32)
                 + jnp.dot((qp * qdb_ref[p]).astype(bf16), sb_ref[0, c, p], preferred_element_type=f32))
        for a in range(2):
            h = 2 * p + a
            qm = jnp.where(first_half if a == 0 else ~first_half, qp, 0.0).astype(bf16)
            sc = (_dot_nt(qm, kp) * dm_ref[h]).astype(bf16)
            o = jnp.dot(sc, v[:, LANES * h:LANES * (h + 1)], preferred_element_type=f32)
            o = o + cross[:, LANES * a:LANES * (a + 1)]
            oc = o - jnp.mean(o, axis=-1, keepdims=True)
            y = oc * lax.rsqrt(jnp.mean(oc * oc, axis=-1, keepdims=True) + EPS)
            ret_ref[0, rows, LANES * h:LANES * (h + 1)] = (
                y * jax.nn.silu(g[:, LANES * h:LANES * (h + 1)])).astype(ret_ref.dtype)

    u = jax.nn.gelu(u_ref[0, rows, :].astype(f32))
    w = jax.nn.gelu(w_ref[0, rows, :].astype(f32))
    wc = w - jnp.mean(w, axis=-1, keepdims=True)
    wn = (wc * lax.rsqrt(jnp.mean(wc * wc, axis=-1, keepdims=True) + EPS)).astype(bf16)
    gw = wn.shape[-1] // SGU_GROUPS
    for i in range(SGU_GROUPS):
        sl = slice(gw * i, gw * (i + 1))
        mixed = jnp.dot(sw_ref[i], wn[:, sl], preferred_element_type=f32) + sbias_ref[:, sl]
        sgu_ref[0, rows, sl] = (u[:, sl] * mixed).astype(sgu_ref.dtype)


def _ret_sgu(z, lay, cos, sin, sf, sb, tabs, sgu_w, sgu_bias):
    B, T, _ = z.shape
    cpb = _chunks_per_step(T)
    rows = cpb * CHUNK
    nc = T // rows
    bw = lay['ret_v'][1]

    def zs(name):
        s, w = lay[name]
        return pl.BlockSpec((1, rows, w), lambda b, n, j=s // w: (b, n, j))

    st_shape = (RET_HEADS // 2, LANES, 2 * LANES)
    const3 = lambda b, n: (0, 0, 0)
    out_spec = pl.BlockSpec((1, rows, bw), lambda b, n: (b, n, 0))
    return pl.pallas_call(
        functools.partial(_ret_sgu_body, cpb),
        grid=(B, nc),
        in_specs=[
            zs('ret_q'), zs('ret_k'), zs('ret_v'), zs('ret_g'), zs('sgu_u'), zs('sgu_v'),
            pl.BlockSpec((rows, LANES), lambda b, n: (n, 0)),
            pl.BlockSpec((rows, LANES), lambda b, n: (n, 0)),
            pl.BlockSpec((1, cpb) + st_shape, lambda b, n: (b, n, 0, 0, 0)),
            pl.BlockSpec((1, cpb) + st_shape, lambda b, n: (b, n, 0, 0, 0)),
            pl.BlockSpec((RET_HEADS, CHUNK, CHUNK), const3),
            pl.BlockSpec((RET_HEADS // 2, CHUNK, LANES), const3),
            pl.BlockSpec((RET_HEADS // 2, CHUNK, LANES), const3),
            pl.BlockSpec((SGU_GROUPS, CHUNK, CHUNK), const3),
            pl.BlockSpec((CHUNK, bw), lambda b, n: (0, 0)),
        ],
        out_specs=[out_spec, out_spec],
        out_shape=[jax.ShapeDtypeStruct((B, T, bw), bf16), jax.ShapeDtypeStruct((B, T, bw), bf16)],
        compiler_params=_params("parallel", "parallel"),
        name="ret_sgu",
    )(z, z, z, z, z, z, cos, sin, sf, sb, tabs['dm'], tabs['qdf'], tabs['qdb'], sgu_w, sgu_bias)


def _att_body(local, qb, sink_ref, q_ref, *refs):
    if local:
        (kp_ref, kc_ref, kn_ref, vp_ref, vc_ref, vn_ref, qa_ref, qb_ref,
         kap_ref, kbp_ref, kac_ref, kbc_ref, kan_ref, kbn_ref, ck_ref, cv_ref, gk_ref, ones_ref, o_ref) = refs
    else:
        qa_ref, ck_ref, cv_ref, gk_ref, ones_ref, o_ref = refs
    n = pl.program_id(1)
    n_blocks = pl.num_programs(1) * qb
    ones = ones_ref[...]
    inv_hd = 1.0 / HEAD_DIM

    def norm_rope(x_ref_val, ta, tb):
        x = x_ref_val.astype(f32)
        r = lax.rsqrt(_head_sumsq(x, ones) * inv_hd + EPS)
        y = x * ta if tb is None else x * ta + _swap_halves(x) * tb
        return y * r

    def dup(x):
        lo = _lane(x.shape) < HEAD_DIM
        xr = pltpu.roll(x, HEAD_DIM, 1)
        return jnp.where(lo, x, xr).astype(bf16), jnp.where(lo, xr, x).astype(bf16)

    def blk(i):
        return slice(CHUNK * i, CHUNK * (i + 1))

    k_ctx = dup(norm_rope(ck_ref[0], gk_ref[...], None))
    v_ctx = dup(cv_ref[0].astype(f32))
    if local:
        k_loc = ([dup(norm_rope(kp_ref[0], kap_ref[...], kbp_ref[...]))]
                 + [dup(norm_rope(kc_ref[0, blk(i), :], kac_ref[blk(i), :], kbc_ref[blk(i), :])) for i in range(qb)]
                 + [dup(norm_rope(kn_ref[0], kan_ref[...], kbn_ref[...]))])
        v_loc = ([dup(vp_ref[0].astype(f32))] + [dup(vc_ref[0, blk(i), :].astype(f32)) for i in range(qb)]
                 + [dup(vn_ref[0].astype(f32))])
    n_loc = 3 * CHUNK if local else 0

    n_groups = q_ref.shape[-1] // LANES
    groups_per_kv = n_groups // ATT_KV_HEADS
    rows = 2 * groups_per_kv * CHUNK
    log2e = 1.4426950408889634
    first_half = _lane((CHUNK, LANES)) < HEAD_DIM
    for qi in range(qb):
        if local:
            i = lax.broadcasted_iota(jnp.int32, (CHUNK, n_loc), 0)
            j = lax.broadcasted_iota(jnp.int32, (CHUNK, n_loc), 1)
            has_prev = (n * qb + qi > 0).astype(jnp.int32)
            has_next = (n * qb + qi < n_blocks - 1).astype(jnp.int32)
            valid = (j >= i * has_prev + CHUNK * (1 - has_prev)) & (j <= 2 * CHUNK - 1 + has_next * (i + 1))
            bias = jnp.where(valid, 0.0, NEG_INF)
            bias = jnp.concatenate([bias] * (rows // CHUNK), axis=0)
        for h in range(ATT_KV_HEADS):
            keys = ([k_loc[qi + d][h] for d in range(3)] if local else []) + [k_ctx[h]]
            vals = ([v_loc[qi + d][h] for d in range(3)] if local else []) + [v_ctx[h]]
            keys = jnp.concatenate(keys, axis=0) if local else keys[0]
            vals = jnp.concatenate(vals, axis=0) if local else vals[0]
            q_rows, sink_rows = [], []
            for c in range(groups_per_kv * h, groups_per_kv * (h + 1)):
                xq = q_ref[0, blk(qi), LANES * c:LANES * (c + 1)]
                qn = norm_rope(xq, qa_ref[blk(qi), :] if local else qa_ref[...],
                               qb_ref[blk(qi), :] if local else None) * (HEAD_DIM ** -0.5 * log2e)
                for a in range(2):
                    q_rows.append(jnp.where(first_half if a == 0 else ~first_half, qn, 0.0).astype(bf16))
                    sink_rows.append(jnp.full((CHUNK, 1), sink_ref[2 * c + a] * log2e, f32))
            s = _dot_nt(jnp.concatenate(q_rows, axis=0), keys)
            sk = jnp.concatenate(sink_rows, axis=0)
            s_ctx = s[:, n_loc:]
            m = jnp.maximum(jnp.max(s_ctx, axis=-1, keepdims=True), sk)
            if local:
                s_loc = s[:, :n_loc] + bias
                m = jnp.maximum(m, jnp.max(s_loc, axis=-1, keepdims=True))
            p_ctx = jnp.exp2(s_ctx - m)
            d = jnp.sum(p_ctx, axis=-1, keepdims=True) + jnp.exp2(sk - m)
            o = jnp.dot(p_ctx.astype(bf16), vals[n_loc:], preferred_element_type=f32)
            if local:
                p_loc = jnp.exp2(s_loc - m)
                d = d + jnp.sum(p_loc, axis=-1, keepdims=True)
                o = o + jnp.dot(p_loc.astype(bf16), vals[:n_loc], preferred_element_type=f32)
            o = o * (1.0 / d)
            for ci in range(groups_per_kv):
                c = groups_per_kv * h + ci
                o0 = o[2 * CHUNK * ci:2 * CHUNK * ci + CHUNK]
                o1 = o[2 * CHUNK * ci + CHUNK:2 * CHUNK * (ci + 1)]
                o_ref[0, blk(qi), LANES * c:LANES * (c + 1)] = jnp.where(first_half, o0, o1).astype(o_ref.dtype)


def _attention(z, zc, lay, att_tabs, sink, local):
    B, T, _ = z.shape
    L = zc.shape[1]
    nb = T // CHUNK
    qb = _chunks_per_step(T)
    rows = qb * CHUNK
    qs, qw = lay['att_q']
    ks, kw = lay['att_k']
    vs, _ = lay['att_v']
    kj, vj = ks // kw, vs // kw
    prev = lambda n: jnp.maximum(n * qb - 1, 0)
    nxt = lambda n: jnp.minimum(n * qb + qb, nb - 1)
    one = lambda f: pl.BlockSpec((CHUNK, LANES), lambda b, n: (f(n), 0))
    cur = pl.BlockSpec((rows, LANES), lambda b, n: (n, 0))
    in_specs = [pl.BlockSpec(memory_space=pltpu.SMEM),
                pl.BlockSpec((1, rows, qw), lambda b, n: (b, n, qs // qw))]
    args = [sink, z]
    if local:
        for j in (kj, vj):
            in_specs += [pl.BlockSpec((1, CHUNK, kw), lambda b, n, j=j: (b, prev(n), j)),
                         pl.BlockSpec((1, rows, kw), lambda b, n, j=j: (b, n, j)),
                         pl.BlockSpec((1, CHUNK, kw), lambda b, n, j=j: (b, nxt(n), j))]
        args += [z] * 6
        in_specs += [cur, cur, one(prev), one(prev), cur, cur, one(nxt), one(nxt)]
        args += [att_tabs['qa'], att_tabs['qb'], att_tabs['ka'], att_tabs['kb'], att_tabs['ka'], att_tabs['kb'],
                 att_tabs['ka'], att_tabs['kb']]
    else:
        in_specs += [pl.BlockSpec((1, LANES), lambda b, n: (0, 0))]
        args += [att_tabs['gq']]
    in_specs += [pl.BlockSpec((1, L, kw), lambda b, n: (b, 0, kj)),
                 pl.BlockSpec((1, L, kw), lambda b, n: (b, 0, vj)),
                 pl.BlockSpec((1, LANES), lambda b, n: (0, 0)),
                 pl.BlockSpec((LANES, LANES), lambda b, n: (0, 0))]
    args += [zc, zc, att_tabs['gk'], att_tabs['ones']]
    return pl.pallas_call(
        functools.partial(_att_body, local, qb),
        grid=(B, nb // qb),
        in_specs=in_specs,
        out_specs=pl.BlockSpec((1, rows, qw), lambda b, n: (b, n, 0)),
        out_shape=jax.ShapeDtypeStruct((B, T, qw), bf16),
        compiler_params=_params("parallel", "parallel"),
        name="attention" if local else "ctx_attention",
    )(*args)


def _merge_body(n_gate_blocks, ret_ref, sgu_ref, att_ref, *refs):
    g_refs = refs[:N_BRANCHES * n_gate_blocks]
    x_ref, wb_ref, wo_ref, gate_ref, scale_ref, shift_ref, wr_ref, xo_ref, h2_ref, aff_ref = refs[len(g_refs):]
    merged = None
    for i, br_ref in enumerate((ret_ref, sgu_ref, att_ref)):
        p = jnp.dot(br_ref[0], wb_ref[i], preferred_element_type=f32)
        g = jnp.concatenate([r[0] for r in g_refs[n_gate_blocks * i:n_gate_blocks * (i + 1)]], axis=1)
        t = jax.nn.sigmoid(g.astype(f32)) * p
        merged = t if merged is None else merged + t
    o = jnp.dot(merged.astype(bf16), wo_ref[...], preferred_element_type=f32)
    x = x_ref[0] + gate_ref[0] * o
    xo_ref[0] = x
    y = x * lax.rsqrt(jnp.mean(x * x, axis=-1, keepdims=True) + EPS)
    h2 = y * scale_ref[0] + shift_ref[0]
    h2_ref[0, :, 0, :] = h2
    logits = jnp.dot(h2.astype(bf16), wr_ref[...], preferred_element_type=f32)
    logits = jnp.where(_lane(logits.shape) < N_EXPERTS, logits, NEG_INF)
    m = jnp.max(logits, axis=-1, keepdims=True)
    e = jnp.exp(logits - m)
    aff_ref[0] = e / jnp.sum(e, axis=-1, keepdims=True)


def _merge(ret, sgu, att, z, gates_at, x, wb, wo, gate, scale, shift, wr):
    B, T, D = x.shape
    bw = ret.shape[-1]
    tm = _pick_tile(T, 512)
    tok = lambda b, i: (b, i, 0)
    vec = lambda b, i: (b, 0, 0)
    gw = 2 * LANES
    assert gates_at % gw == 0 and D % gw == 0
    n_gate_blocks = D // gw
    gate_specs = [pl.BlockSpec((1, tm, gw), lambda b, i, j=gates_at // gw + k: (b, i, j))
                  for k in range(N_BRANCHES * n_gate_blocks)]
    return pl.pallas_call(
        functools.partial(_merge_body, n_gate_blocks),
        grid=(B, T // tm),
        in_specs=[
            pl.BlockSpec((1, tm, bw), tok), pl.BlockSpec((1, tm, bw), tok), pl.BlockSpec((1, tm, bw), tok),
            *gate_specs,
            pl.BlockSpec((1, tm, D), tok),
            pl.BlockSpec((N_BRANCHES, bw, D), lambda b, i: (0, 0, 0)),
            pl.BlockSpec((D, D), lambda b, i: (0, 0)),
            pl.BlockSpec((1, 1, D), vec), pl.BlockSpec((1, 1, D), vec), pl.BlockSpec((1, 1, D), vec),
            pl.BlockSpec((D, LANES), lambda b, i: (0, 0)),
        ],
        out_specs=[
            pl.BlockSpec((1, tm, D), tok), pl.BlockSpec((1, tm, 1, D), lambda b, i: (b, i, 0, 0)),
            pl.BlockSpec((1, tm, LANES), tok),
        ],
        out_shape=[
            jax.ShapeDtypeStruct((B, T, D), f32),
            jax.ShapeDtypeStruct((B, T, 1, D), f32),
            jax.ShapeDtypeStruct((B, T, LANES), f32),
        ],
        compiler_params=_params("parallel", "parallel"),
        name="merge",
    )(ret, sgu, att, *([z] * len(gate_specs)), x, wb, wo, gate, scale, shift, wr)


ROUTE_TILE = 256
AFF_BITS = 31
MIN_NORMAL_BITS = 0x00800000


def _route_body(cap, aff_ref, tri_ref, pos_ref, post_ref, c0_ref, afft):
    T = aff_ref.shape[1]
    tt = tri_ref.shape[0]
    for j in range(T // tt):
        afft[:, tt * j:tt * (j + 1)] = aff_ref[0, tt * j:tt * (j + 1), :].T[:N_EXPERTS]

    def as_float(bits):
        return pltpu.bitcast(jnp.broadcast_to(bits, (N_EXPERTS, LANES)), f32)[:, :1]

    def count_at_least(v):
        return jnp.sum(jnp.where(afft[...] >= v, 1.0, 0.0), axis=1, keepdims=True)

    def refine(i, thr):
        cand = thr | jnp.left_shift(jnp.int32(1), AFF_BITS - 1 - i)
        return jnp.where(count_at_least(as_float(cand)) >= cap, cand, thr)

    thr = lax.fori_loop(0, AFF_BITS, refine, jnp.zeros((N_EXPERTS, 1), jnp.int32))
    hi_col = as_float(jnp.maximum(thr + 1, MIN_NORMAL_BITS))
    diag = (lax.broadcasted_iota(jnp.int32, (N_EXPERTS, LANES), 0)
            == lax.broadcasted_iota(jnp.int32, (N_EXPERTS, LANES), 1))

    def along_lanes(col):
        return jnp.sum(jnp.where(diag, col, 0.0), axis=0, keepdims=True)

    thr_f = along_lanes(as_float(thr))
    hi_f = along_lanes(hi_col)
    need = along_lanes(cap - count_at_least(hi_col))
    tri = tri_ref[...]
    carry_eq = jnp.zeros((1, LANES), f32)
    carry_sel = jnp.zeros((1, LANES), f32)
    for j in range(T // tt):
        a = aff_ref[0, tt * j:tt * (j + 1), :]
        eq = (a >= thr_f) & (a < hi_f)
        eqf = jnp.where(eq, 1.0, 0.0)
        eq_rank = carry_eq + jnp.dot(tri, eqf.astype(bf16), preferred_element_type=f32) - eqf
        sel = (a >= hi_f) | (eq & (eq_rank < need))
        self_ = jnp.where(sel, 1.0, 0.0)
        pos = carry_sel + jnp.dot(tri, self_.astype(bf16), preferred_element_type=f32) - self_
        pos = jnp.where(sel, pos, -1.0).astype(jnp.int32)
        pos_ref[0, tt * j:tt * (j + 1), :] = pos
        post_ref[0, j] = pos.T[:N_EXPERTS]
        c0_ref[0, j:j + 1, :] = carry_sel.astype(jnp.int32)
        carry_eq = carry_eq + jnp.sum(eqf, axis=0, keepdims=True)
        carry_sel = carry_sel + jnp.sum(self_, axis=0, keepdims=True)


def _route(aff, cap):
    B, T, _ = aff.shape
    tt = min(ROUTE_TILE, T)
    nt = T // tt
    tri = jnp.tril(jnp.ones((tt, tt), bf16))
    return pl.pallas_call(
        functools.partial(_route_body, cap),
        grid=(B,),
        in_specs=[pl.BlockSpec((1, T, LANES), lambda b: (b, 0, 0)), pl.BlockSpec((tt, tt), lambda b: (0, 0))],
        out_specs=[
            pl.BlockSpec((1, T, LANES), lambda b: (b, 0, 0)),
            pl.BlockSpec((1, nt, N_EXPERTS, tt), lambda b: (b, 0, 0, 0)),
            pl.BlockSpec((1, nt, LANES), lambda b: (b, 0, 0)),
        ],
        out_shape=[
            jax.ShapeDtypeStruct((B, T, LANES), jnp.int32),
            jax.ShapeDtypeStruct((B, nt, N_EXPERTS, tt), jnp.int32),
            jax.ShapeDtypeStruct((B, nt, LANES), jnp.int32),
        ],
        scratch_shapes=[pltpu.VMEM((N_EXPERTS, T), f32)],
        compiler_params=_params("parallel"),
        name="route",
    )(aff, tri)


def _slots_body(nt, cap, c0_ref, post_ref, idx_ref):
    b = pl.program_id(0)
    tt = post_ref.shape[-1]
    tok = lax.broadcasted_iota(jnp.int32, (LANES, tt), 1)
    diag = lax.broadcasted_iota(jnp.int32, (LANES, LANES), 0) == lax.broadcasted_iota(jnp.int32, (LANES, LANES), 1)

    def per_expert(e, carry):
        def tiles_starting_at_or_before(slot):
            return sum((c0_ref[(b * nt + j) * N_EXPERTS + e] <= slot).astype(jnp.int32) for j in range(nt))

        for k in range(idx_ref.shape[2]):
            slot = LANES * k + lax.broadcasted_iota(jnp.int32, (LANES, 1), 0)

            def match(j, acc):
                return acc + jnp.where(post_ref[0, j, pl.ds(e, 1), :] == slot, tok + j * tt, 0)

            j_lo = tiles_starting_at_or_before(LANES * k) - 1
            j_hi = tiles_starting_at_or_before(min(LANES * (k + 1), cap) - 1)
            acc = lax.fori_loop(j_lo, j_hi, match, jnp.zeros((LANES, tt), jnp.int32))
            col = jnp.sum(acc.astype(f32), axis=1, keepdims=True)
            row = jnp.sum(jnp.where(diag, col, 0.0), axis=0, keepdims=True)
            idx_ref[0, e, k:k + 1, :] = row.astype(jnp.int32)
        return carry

    lax.fori_loop(0, N_EXPERTS, per_expert, 0)


def _slots(c0s, post, cap):
    B, nt, E, tt = post.shape
    nblk = -(-cap // LANES)
    idx = pl.pallas_call(
        functools.partial(_slots_body, nt, cap),
        grid_spec=pltpu.PrefetchScalarGridSpec(
            num_scalar_prefetch=1,
            grid=(B,),
            in_specs=[pl.BlockSpec((1, nt, E, tt), lambda b, c0: (b, 0, 0, 0))],
            out_specs=pl.BlockSpec((1, E, nblk, LANES), lambda b, c0: (b, 0, 0, 0)),
        ),
        out_shape=jax.ShapeDtypeStruct((B, E, nblk, LANES), jnp.int32),
        compiler_params=_params("arbitrary"),
        name="slots",
    )(c0s, post)
    return idx.reshape(B, E, nblk * LANES)[:, :, :cap]


EXPERT_MIN_ROWS = 256


def _expert_body(cap, spb, idx_ref, h_ref, wg_ref, wu_ref, wd_ref, y_ref, rows, wg_s, wu_s, wd_s, sem):
    n_b = pl.num_programs(1)
    step = pl.program_id(0) * n_b + pl.program_id(1)
    last = pl.num_programs(0) * n_b - 1
    n_rows = spb * cap

    def start_rows(s, buf):
        e, g = s // n_b, s % n_b
        for r in range(n_rows):
            b = g * spb + r // cap
            row = idx_ref[(b * N_EXPERTS + e) * cap + r % cap]
            pltpu.make_async_copy(h_ref.at[row], rows.at[buf, pl.ds(r, 1), :], sem.at[buf]).start()

    def wait_rows(buf):
        pltpu.make_async_copy(h_ref.at[pl.ds(0, n_rows), 0, :], rows.at[buf], sem.at[buf]).wait()

    @pl.when(step == 0)
    def _():
        start_rows(step, 0)

    @pl.when(pl.program_id(1) == 0)
    def _():
        wg_s[...] = wg_ref[0, 0].astype(bf16)
        wu_s[...] = wu_ref[0, 0].astype(bf16)
        wd_s[...] = wd_ref[0, 0].astype(bf16)

    for cur in range(2):
        @pl.when(step % 2 == cur)
        def _():
            wait_rows(cur)
            xs = rows[cur].astype(bf16)

            @pl.when(step < last)
            def _():
                start_rows(step + 1, 1 - cur)

            a = jnp.dot(xs, wg_s[...], preferred_element_type=f32)
            u = jnp.dot(xs, wu_s[...], preferred_element_type=f32)
            hmid = (jax.nn.silu(a) * u).astype(bf16)
            y = jnp.dot(hmid, wd_s[...], preferred_element_type=f32).astype(y_ref.dtype)
            for i in range(spb):
                y_ref[i, 0] = y[cap * i:cap * (i + 1)]


def _expert_ffn(idx, h2, layer, wg, wu, wd):
    rows_of = idx + (jnp.arange(idx.shape[0], dtype=jnp.int32) * h2.shape[1])[:, None, None]
    B, T, _, D = h2.shape
    _, E, _, F = wg.shape
    cap = idx.shape[-1]
    spb = _pick_tile(B, max(1, EXPERT_MIN_ROWS // cap))
    n_rows = spb * cap
    wsp = lambda e, g, idx: (layer, e, 0, 0)
    return pl.pallas_call(
        functools.partial(_expert_body, cap, spb),
        grid_spec=pltpu.PrefetchScalarGridSpec(
            num_scalar_prefetch=1,
            grid=(E, B // spb),
            in_specs=[
                pl.BlockSpec(memory_space=pl.ANY),
                pl.BlockSpec((1, 1, D, F), wsp), pl.BlockSpec((1, 1, D, F), wsp), pl.BlockSpec((1, 1, F, D), wsp),
            ],
            out_specs=pl.BlockSpec((spb, 1, cap, D), lambda e, g, idx: (g, e, 0, 0)),
            scratch_shapes=[pltpu.VMEM((2, n_rows, D), f32),
                            pltpu.VMEM((D, F), bf16), pltpu.VMEM((D, F), bf16), pltpu.VMEM((F, D), bf16),
                            pltpu.SemaphoreType.DMA((2,))],
        ),
        out_shape=jax.ShapeDtypeStruct((B, E, cap, D), bf16),
        compiler_params=pltpu.CompilerParams(dimension_semantics=("arbitrary", "arbitrary"),
                                             vmem_limit_bytes=EXPERT_VMEM_LIMIT_BYTES),
        name="expert_ffn",
    )(rows_of.reshape(-1), h2.reshape(B * T, 1, D), wg, wu, wd)


Y_ROW_ALIGN = 16


def _combine_body(nt, cap, win, c0_ref, pos_ref, aff_ref, x_ref, gate_ref, spread_ref, y_ref, o_ref, buf, extra,
                  sem, sem_x):
    b = pl.program_id(0)
    j = pl.program_id(1)
    step = b * nt + j
    tt = pos_ref.shape[1]
    stride = win if win == cap else win - Y_ROW_ALIGN

    def c0(bb, jj, e):
        return c0_ref[(bb * nt + jj) * N_EXPERTS + e]

    def window_start(lo):
        return pl.multiple_of(jnp.minimum(lo & (-Y_ROW_ALIGN), cap - win), Y_ROW_ALIGN)

    def copies(bb, jj, p, dst, s):
        return [pltpu.make_async_copy(
            y_ref.at[bb, e, pl.ds(window_start(c0(bb, jj, e) + p * stride), win), :],
            dst.at[pl.ds(e * win, win), :], s) for e in range(N_EXPERTS)]

    @pl.when(step == 0)
    def _():
        for cp in copies(b, j, 0, buf.at[0], sem.at[0]):
            cp.start()

    nxt = step + 1

    @pl.when(nxt < pl.num_programs(0) * nt)
    def _():
        for cp in copies(nxt // nt, nxt % nt, 0, buf.at[nxt % 2], sem.at[nxt % 2]):
            cp.start()

    lane = _lane((1, LANES))
    pos = pos_ref[0]
    aff = aff_ref[0]

    def lane_vector(fn):
        v = jnp.zeros((1, LANES), jnp.int32)
        for e in range(N_EXPERTS):
            v = jnp.where(lane == e, fn(e), v)
        return v

    spread = spread_ref[...]
    col_slot = _lane((1, N_EXPERTS * win)) & (win - 1)
    aff_cols = jnp.dot(aff.astype(bf16), spread, preferred_element_type=f32)

    def weights(p):
        lo = lane_vector(lambda e: c0(b, j, e) + p * stride)
        start = lane_vector(lambda e: window_start(c0(b, j, e) + p * stride))
        rel = jnp.where((pos >= lo) & (pos < lo + stride), pos - start, -1).astype(f32)
        rel_cols = jnp.dot(rel.astype(bf16), spread, preferred_element_type=f32)
        return jnp.where(rel_cols == col_slot.astype(f32), aff_cols, 0.0).astype(bf16)

    for cp in copies(b, j, 0, buf.at[step % 2], sem.at[step % 2]):
        cp.wait()
    moe = jnp.dot(weights(0), buf[step % 2], preferred_element_type=f32)

    def count(e):
        end = jnp.where(j + 1 < nt, c0(b, jnp.minimum(j + 1, nt - 1), e), cap)
        return end - c0(b, j, e)

    max_count = functools.reduce(jnp.maximum, [count(e) for e in range(N_EXPERTS)])

    def more(p, acc):
        cps = copies(b, j, p, extra, sem_x.at[0])
        for cp in cps:
            cp.start()
        for cp in cps:
            cp.wait()
        return acc + jnp.dot(weights(p), extra[...], preferred_element_type=f32)

    moe = lax.fori_loop(1, (max_count + stride - 1) // stride, more, moe)
    o_ref[0] = x_ref[0] + gate_ref[0] * moe


def _combine(c0s, pos, aff, x, gate, y):
    B, T, D = x.shape
    E, cap = y.shape[1], y.shape[2]
    tt = min(ROUTE_TILE, T)
    nt = T // tt
    win = min(cap, LANES // 2)
    assert (cap - win) % Y_ROW_ALIGN == 0 and win & (win - 1) == 0
    spread = (jnp.arange(LANES)[:, None] == jnp.arange(E * win)[None, :] // win).astype(bf16)
    tok = lambda b, j, c0: (b, j, 0)
    return pl.pallas_call(
        functools.partial(_combine_body, nt, cap, win),
        grid_spec=pltpu.PrefetchScalarGridSpec(
            num_scalar_prefetch=1,
            grid=(B, nt),
            in_specs=[
                pl.BlockSpec((1, tt, LANES), tok), pl.BlockSpec((1, tt, LANES), tok), pl.BlockSpec((1, tt, D), tok),
                pl.BlockSpec((1, 1, D), lambda b, j, c0: (b, 0, 0)),
                pl.BlockSpec((LANES, E * win), lambda b, j, c0: (0, 0)),
                pl.BlockSpec(memory_space=pl.ANY),
            ],
            out_specs=pl.BlockSpec((1, tt, D), tok),
            scratch_shapes=[pltpu.VMEM((2, E * win, D), bf16), pltpu.VMEM((E * win, D), bf16),
                            pltpu.SemaphoreType.DMA((2,)), pltpu.SemaphoreType.DMA((1,))],
        ),
        out_shape=jax.ShapeDtypeStruct((B, T, D), f32),
        compiler_params=_params("arbitrary", "arbitrary"),
        name="combine",
    )(c0s, pos, aff, x, gate, spread, y)


def _expert_choice(h2, aff, x, gate2, p):
    T = x.shape[1]
    cap = CAPACITY_FACTOR * T // N_EXPERTS
    pos, post, c0 = _route(aff, cap)
    c0s = c0[:, :, :N_EXPERTS].reshape(-1)
    y = _expert_ffn(_slots(c0s, post, cap), h2, p['layer'], p['w_exp_gate'], p['w_exp_up'], p['w_exp_down'])
    return _combine(c0s, pos, aff, x, gate2, y)


def _rope_tables(T):
    rows = T // GRID_W
    row = jnp.repeat(jnp.arange(rows), GRID_W).astype(f32)
    col = jnp.tile(jnp.arange(GRID_W), rows).astype(f32)
    n_freq = HEAD_DIM // 4
    inv = jnp.power(ROPE_BASE, -jnp.arange(n_freq, dtype=f32) / n_freq)
    ang = jnp.concatenate([row[:, None] * inv, col[:, None] * inv], axis=-1)
    cos, sin = jnp.cos(ang), jnp.sin(ang)
    reps = LANES // HEAD_DIM
    return jnp.tile(jnp.concatenate([cos, cos], axis=-1), (1, reps)), jnp.tile(
        jnp.concatenate([-sin, sin], axis=-1), (1, reps))


def _identity_rope(T):
    return jnp.ones((T, LANES), f32), jnp.zeros((T, LANES), f32)


def _retention_tables(decay_fwd, decay_bwd):
    lgf = jax.nn.log_sigmoid(decay_fwd.astype(f32))
    lgb = jax.nn.log_sigmoid(decay_bwd.astype(f32))
    C = CHUNK
    k_scale = HEAD_DIM ** -0.5
    idx = jnp.arange(C, dtype=f32)
    npair = RET_HEADS // 2

    def lanes(t):
        return jnp.repeat(t.reshape(npair, 2, C).transpose(0, 2, 1), HEAD_DIM, axis=-1)

    diff = idx[:, None] - idx[None, :]
    dm = jnp.where(diff >= 0, jnp.exp(jnp.maximum(diff, 0.0)[None] * lgf[:, None, None]),
                   jnp.exp(jnp.maximum(-diff, 0.0)[None] * lgb[:, None, None])) * k_scale

    def chunk_decay(lg):
        rows = jnp.repeat(jnp.exp(C * lg).reshape(npair, 2), HEAD_DIM, axis=-1)
        return jnp.broadcast_to(rows[:, :, None], (npair, LANES, 2 * LANES))

    r = jnp.arange(LANES)[:, None] // HEAD_DIM
    c = jnp.arange(2 * LANES)[None, :] // LANES
    return {
        'wkf': lanes(jnp.exp((C - 1 - idx)[None, :] * lgf[:, None])) * k_scale,
        'wkb': lanes(jnp.exp(idx[None, :] * lgb[:, None])) * k_scale,
        'qdf': lanes(jnp.exp((idx + 1.0)[None, :] * lgf[:, None])),
        'qdb': lanes(jnp.exp((C - idx)[None, :] * lgb[:, None])),
        'dm': dm, 'cdf': chunk_decay(lgf), 'cdb': chunk_decay(lgb), 'bm': (r == c).astype(f32),
    }


def _attention_tables(cos, sin, q_gain, k_gain):
    reps = LANES // HEAD_DIM
    half = HEAD_DIM // 2
    swap = lambda g: jnp.concatenate([g[half:], g[:half]])
    gq, gk = jnp.tile(q_gain, reps)[None], jnp.tile(k_gain, reps)[None]
    gqs, gks = jnp.tile(swap(q_gain), reps)[None], jnp.tile(swap(k_gain), reps)[None]
    r = jnp.arange(LANES) // HEAD_DIM
    return {'qa': cos * gq, 'qb': sin * gqs, 'ka': cos * gk, 'kb': sin * gks, 'gq': gq, 'gk': gk,
            'ones': (r[:, None] == r[None, :]).astype(bf16)}


def _layer(x, xc, mod_lat, mod_ctx, rope, p, last):
    B, S, D = x.shape
    L = xc.shape[1]
    lay = _layout(D)
    sh1, sc1, g1, sh2, sc2, g2 = jnp.split(mod_lat, 6, axis=-1)
    csh1, csc1, cg1, csh2, csc2, cg2 = jnp.split(mod_ctx, 6, axis=-1)
    n1 = p['norm1_g'][None, None, :]
    n2 = p['norm2_g'][None, None, :]

    def bcast(t):
        return jnp.broadcast_to(t, (B, 1, D))

    z = _proj_in(x, n1 * (1 + sc1), sh1, p['w_in'])
    zc = _proj_in(xc, bcast(n1 * (1 + csc1)), bcast(csh1), p['w_in'])

    cos, sin = rope
    cos_c, sin_c = _identity_rope(L)
    rtabs = _retention_tables(p['ret_decay_fwd'], p['ret_decay_bwd'])
    atabs = _attention_tables(cos, sin, p['q_norm_g'], p['k_norm_g'])
    sgu_w = p['sgu_w'].astype(bf16)
    gw = D // 2 // SGU_GROUPS
    sgu_bias = jnp.repeat(p['sgu_b'].T, gw, axis=-1)

    zero = jnp.zeros((B, RET_HEADS // 2, LANES, 2 * LANES), f32)
    sfc, sbc, s_f, s_b = _ret_states(zc, lay, cos_c, sin_c, rtabs, zero, zero)
    sf, sb, _, _ = _ret_states(z, lay, cos, sin, rtabs, s_f, s_b)
    ret, sgu = _ret_sgu(z, lay, cos, sin, sf, sb, rtabs, sgu_w, sgu_bias)
    att = _attention(z, zc, lay, atabs, p['attn_sink'], True)

    x, h2, aff = _merge(ret, sgu, att, z, lay['gates'][0], x, p['w_branch'], p['w_out'], g1, n2 * (1 + sc2), sh2,
                        p['w_router'])
    x = _expert_choice(h2, aff, x, g2, p)
    if last:
        return x, None

    ret_c, sgu_c = _ret_sgu(zc, lay, cos_c, sin_c, sfc, sbc, rtabs, sgu_w, sgu_bias)
    att_c = _attention(zc, zc, lay, atabs, p['attn_sink'], False)
    xc, h2c, affc = _merge(ret_c, sgu_c, att_c, zc, lay['gates'][0], xc, p['w_branch'], p['w_out'], bcast(cg1),
                           bcast(n2 * (1 + csc2)), bcast(csh2), p['w_router'])
    xc = _expert_choice(h2c, affc, xc, bcast(cg2), p)
    return x, xc


def kernel(x, c, ctx, c_ctx, w_mod, b_mod, norm1_g, norm2_g, w_in, ret_decay_fwd, ret_decay_bwd, sgu_w, sgu_b,
           q_norm_g, k_norm_g, attn_sink, w_branch, w_out, w_router, w_exp_gate, w_exp_up, w_exp_down):
    B, S, D = x.shape
    depth = w_in.shape[0]
    rope = _rope_tables(S)
    c_act = jax.nn.silu(c)
    c_ctx_act = jax.nn.silu(c_ctx)
    xc = ctx
    hp = lax.Precision.HIGHEST
    for l in range(depth):
        wr = jnp.zeros((D, LANES), f32).at[:, :N_EXPERTS].set(w_router[l]).astype(bf16)
        p = {
            'norm1_g': norm1_g[l], 'norm2_g': norm2_g[l], 'w_in': w_in[l].astype(bf16),
            'ret_decay_fwd': ret_decay_fwd[l], 'ret_decay_bwd': ret_decay_bwd[l],
            'sgu_w': sgu_w[l], 'sgu_b': sgu_b[l],
            'q_norm_g': q_norm_g[l], 'k_norm_g': k_norm_g[l], 'attn_sink': attn_sink[l],
            'w_branch': w_branch[l].astype(bf16), 'w_out': w_out[l].astype(bf16), 'w_router': wr,
            'layer': l, 'w_exp_gate': w_exp_gate, 'w_exp_up': w_exp_up, 'w_exp_down': w_exp_down,
        }
        mod = jnp.dot(jnp.concatenate([c_act, c_ctx_act[None]], axis=0), w_mod[l], precision=hp) + b_mod[l]
        mod_lat, mod_ctx = mod[:B, None, :], mod[B:, None, :]
        x, xc = _layer(x, xc, mod_lat, mod_ctx, rope, p, l == depth - 1)
    return x
```
